```python
import jax, jax.numpy as jnp
from jax import lax
import numpy as np

D_MODEL = 2048
BATCH = 2
SEQ = 4096
DEPTH = 1

CHUNK = 64
D_RET = 1024
D_CONV = 1024
D_MIX = D_RET + D_CONV
N_RET_HEADS = 8
HEAD_DIM = D_RET // N_RET_HEADS
CONV_WIDTH = 3
D_IN_PROJ = 4 * D_RET + 3 * D_CONV
N_EXPERTS = 32
TOP_K = 4
D_FF = 2048
SWIGLU_LIMIT = 7.0
SWIGLU_ALPHA = 1.702
ROPE_BASE = 10000.0
EPS = 1e-6
EXPERT_BLOCK = 128

kernel_name = "hybrid_retention_shortconv_moe_adaln"


def rms_norm(x, g):
    xf = x.astype(jnp.float32)
    y = xf * lax.rsqrt(jnp.mean(xf * xf, axis=-1, keepdims=True) + EPS)
    return (y * g.astype(jnp.float32)).astype(x.dtype)


def rope(x, pos):
    half = x.shape[-1] // 2
    freqs = ROPE_BASE ** (-jnp.arange(half, dtype=jnp.float32) / half)
    ang = pos.astype(jnp.float32)[:, None] * freqs[None, :]
    cos = jnp.cos(ang)[:, None, :]
    sin = jnp.sin(ang)[:, None, :]
    x1, x2 = x[..., :half], x[..., half:]
    return jnp.concatenate([x1 * cos - x2 * sin, x2 * cos + x1 * sin], axis=-1)


def retention_chunkwise(q, k, v):
    b, t, h, dh = q.shape
    nc = t // CHUNK
    log_gamma = jnp.log1p(-jnp.exp2(-5.0 - jnp.arange(h, dtype=jnp.float32)))
    idx = jnp.arange(CHUNK, dtype=jnp.float32)
    dist = jnp.abs(idx[:, None] - idx[None, :])
    d_intra = jnp.exp(log_gamma[:, None, None] * dist)
    d_kv = jnp.exp(log_gamma[:, None] * (CHUNK - 1 - idx)[None])
    d_q = jnp.exp(log_gamma[:, None] * (idx + 1.0)[None])
    d_chunk = jnp.exp(log_gamma * CHUNK)
    qc = q.reshape(b, nc, CHUNK, h, dh)
    kc = k.reshape(b, nc, CHUNK, h, dh)
    vc = v.reshape(b, nc, CHUNK, h, dh)
    scores = jnp.einsum('bzqhd,bzkhd->bzhqk', qc, kc) * d_intra
    intra = jnp.einsum('bzhqk,bzkhe->bzqhe', scores, vc)
    kv = jnp.einsum('bzkhd,bzkhe,hk->zbhde', kc, vc, d_kv)

    def step(state, kv_z):
        return d_chunk[None, :, None, None] * state + kv_z, state

    _, s_prev = lax.scan(step, jnp.zeros((b, h, dh, dh), jnp.float32), kv)
    cross = jnp.einsum('bzqhd,zbhde,hq->bzqhe', qc, s_prev, d_q)
    return (intra + cross).reshape(b, t, h, dh)


def short_conv(u, w):
    return lax.conv_general_dilated(
        u, w[:, None, :].astype(u.dtype), window_strides=(1,),
        padding=[(CONV_WIDTH - 1, 0)],
        dimension_numbers=('NWC', 'WIO', 'NWC'),
        feature_group_count=u.shape[-1])


def hybrid_mixer(h, w_in, conv_w, w_out):
    b, t, _ = h.shape
    pos = jnp.arange(t)
    proj = h @ w_in
    q, k, v, g, bg, cg, u = jnp.split(
        proj, [D_RET, 2 * D_RET, 3 * D_RET, 4 * D_RET,
               4 * D_RET + D_CONV, 4 * D_RET + 2 * D_CONV], axis=-1)
    q = rope(q.reshape(b, t, N_RET_HEADS, HEAD_DIM).astype(jnp.float32), pos)
    k = rope(k.reshape(b, t, N_RET_HEADS, HEAD_DIM).astype(jnp.float32), pos) * (HEAD_DIM ** -0.5)
    v = v.reshape(b, t, N_RET_HEADS, HEAD_DIM).astype(jnp.float32)
    o = retention_chunkwise(q, k, v)
    mu = jnp.mean(o, axis=-1, keepdims=True)
    var = jnp.mean(jnp.square(o - mu), axis=-1, keepdims=True)
    o = ((o - mu) * lax.rsqrt(var + EPS)).reshape(b, t, D_RET).astype(h.dtype)
    y_ret = jax.nn.silu(g) * o
    y_conv = bg * short_conv(cg * u, conv_w)
    return jnp.concatenate([y_ret, y_conv], axis=-1) @ w_out


def moe_ffn(h, w_router, b_router, w_gate_up, b_gate_up, w_down, b_down):
    b, t, d = h.shape
    n = b * t
    xf = h.reshape(n, d)
    logits = xf.astype(jnp.float32) @ w_router.astype(jnp.float32) + b_router.astype(jnp.float32)
    top_val, top_idx = lax.top_k(logits, TOP_K)
    top_w = jax.nn.softmax(top_val, axis=-1)
    n_assign = n * TOP_K
    e_flat = top_idx.reshape(-1)
    tok_flat = jnp.repeat(jnp.arange(n, dtype=jnp.int32), TOP_K)
    w_flat = top_w.reshape(-1)
    order = jnp.argsort(e_flat)
    e_sorted, tok_sorted, w_sorted = e_flat[order], tok_flat[order], w_flat[order]
    counts = jnp.bincount(e_flat, length=N_EXPERTS)
    start = jnp.cumsum(counts) - counts
    padded = (counts + EXPERT_BLOCK - 1) // EXPERT_BLOCK * EXPERT_BLOCK
    padded_end = jnp.cumsum(padded)
    padded_start = padded_end - padded
    dest = padded_start[e_sorted] + jnp.arange(n_assign) - start[e_sorted]
    n_blocks = -(-n_assign // EXPERT_BLOCK) + N_EXPERTS
    p = n_blocks * EXPERT_BLOCK
    tok_buf = jnp.full((p,), n, jnp.int32).at[dest].set(tok_sorted)
    w_buf = jnp.zeros((p,), jnp.float32).at[dest].set(w_sorted)
    block_expert = jnp.minimum(
        jnp.searchsorted(padded_end, jnp.arange(n_blocks) * EXPERT_BLOCK, side='right'),
        N_EXPERTS - 1)
    x_pad = jnp.concatenate([xf, jnp.zeros((1, d), xf.dtype)], axis=0)

    def expert_block(args):
        tok, e = args
        xb = x_pad[tok]
        gu = xb @ w_gate_up[e] + b_gate_up[e]
        gate, up = gu[:, :D_FF], gu[:, D_FF:]
        gate = jnp.minimum(gate, SWIGLU_LIMIT)
        up = jnp.clip(up, -SWIGLU_LIMIT, SWIGLU_LIMIT)
        act = (up + 1.0) * (gate * jax.nn.sigmoid(SWIGLU_ALPHA * gate))
        return act @ w_down[e] + b_down[e]

    y_blocks = lax.map(expert_block, (tok_buf.reshape(n_blocks, EXPERT_BLOCK), block_expert))
    y = jnp.zeros((n + 1, d), h.dtype).at[tok_buf].add(
        y_blocks.reshape(p, d) * w_buf[:, None].astype(h.dtype))
    return y[:n].reshape(b, t, d)


def setup_inputs(seed: int = 0) -> dict:
    key = jax.random.key(seed)
    ks = jax.random.split(key, 16)
    f32 = jnp.float32
    nrm = lambda k, s, sc: jax.random.normal(k, s, f32) * sc
    return {
        "x": nrm(ks[0], (BATCH, SEQ, D_MODEL), 1.0),
        "c": nrm(ks[1], (BATCH, D_MODEL), 1.0),
        "norm1_g": 1.0 + nrm(ks[2], (DEPTH, D_MODEL), 0.01),
        "w_mod": nrm(ks[3], (DEPTH, D_MODEL, 6 * D_MODEL), 0.5 * D_MODEL ** -0.5),
        "b_mod": nrm(ks[4], (DEPTH, 6 * D_MODEL), 0.01),
        "w_in": nrm(ks[5], (DEPTH, D_MODEL, D_IN_PROJ), D_MODEL ** -0.5),
        "conv_w": nrm(ks[6], (DEPTH, CONV_WIDTH, D_CONV), CONV_WIDTH ** -0.5),
        "w_out": nrm(ks[7], (DEPTH, D_MIX, D_MODEL), D_MIX ** -0.5),
        "norm2_g": 1.0 + nrm(ks[8], (DEPTH, D_MODEL), 0.01),
        "w_router": nrm(ks[9], (DEPTH, D_MODEL, N_EXPERTS), D_MODEL ** -0.5),
        "b_router": nrm(ks[10], (DEPTH, N_EXPERTS), 0.01),
        "w_gate_up": nrm(ks[11], (DEPTH, N_EXPERTS, D_MODEL, 2 * D_FF), D_MODEL ** -0.5),
        "b_gate_up": nrm(ks[12], (DEPTH, N_EXPERTS, 2 * D_FF), 0.01),
        "w_down": nrm(ks[13], (DEPTH, N_EXPERTS, D_FF, D_MODEL), D_FF ** -0.5),
        "b_down": nrm(ks[14], (DEPTH, N_EXPERTS, D_MODEL), 0.01),
        "final_g": 1.0 + nrm(ks[15], (D_MODEL,), 0.01),
    }


def reference(x, c, norm1_g, w_mod, b_mod, w_in, conv_w, w_out, norm2_g,
              w_router, b_router, w_gate_up, b_gate_up, w_down, b_down, final_g):
    c_act = jax.nn.silu(c)
    for l in range(DEPTH):
        mod = c_act @ w_mod[l] + b_mod[l]
        shift1, scale1, gate1, shift2, scale2, gate2 = [
            m[:, None, :] for m in jnp.split(mod, 6, axis=-1)]
        h = rms_norm(x, norm1_g[l]) * (1.0 + scale1) + shift1
        x = x + gate1 * hybrid_mixer(h, w_in[l], conv_w[l], w_out[l])
        h = rms_norm(x, norm2_g[l]) * (1.0 + scale2) + shift2
        x = x + gate2 * moe_ffn(h, w_router[l], b_router[l], w_gate_up[l],
                                b_gate_up[l], w_down[l], b_down[l])
    return rms_norm(x, final_g)
```

```python
import functools
import math

import jax
import jax.numpy as jnp
from jax import lax
from jax.experimental import pallas as pl
from jax.experimental.pallas import tpu as pltpu
from jax.experimental.pallas import tpu_sc as plsc

F32 = jnp.float32
BF16 = jnp.bfloat16

CHUNK = 64
N_HEADS = 8
HEAD_DIM = 128
D_RET = N_HEADS * HEAD_DIM
D_CONV = 1024
CONV_WIDTH = 3
N_EXPERTS = 32
TOP_K = 4
SWIGLU_LIMIT = 7.0
SWIGLU_ALPHA = 1.702
ROPE_BASE = 10000.0
EPS = 1e-6

VMEM_LIMIT_BYTES = 58 * 1024 * 1024
SC_WORKERS = 32
SC_ROWS_PER_COPY = 32

MIX_BLOCK = 256
ROW_BLOCK = 128
SLOT_ROWS = 9 * ROW_BLOCK
FF_TILE = 256
DN_TILE = 512


def _tc_params(sem):
    return pltpu.CompilerParams(dimension_semantics=sem, vmem_limit_bytes=VMEM_LIMIT_BYTES)


def _mod_kernel(c_ref, w_ref, b_ref, o_ref):
    c = c_ref[...]
    ca = (c * jax.nn.sigmoid(c)).astype(BF16)
    o_ref[...] = jnp.dot(ca, w_ref[...].astype(BF16), preferred_element_type=F32) + b_ref[...]


def _modulation(c, w_mod, b_mod):
    b, d = c.shape
    n = w_mod.shape[1]
    tn = 1024
    c8 = jnp.zeros((8, d), F32).at[:b].set(c)
    out = pl.pallas_call(
        _mod_kernel,
        grid=(n // tn,),
        in_specs=[pl.BlockSpec((8, d), lambda j: (0, 0)),
                  pl.BlockSpec((d, tn), lambda j: (0, j)),
                  pl.BlockSpec((1, tn), lambda j: (0, j))],
        out_specs=pl.BlockSpec((8, tn), lambda j: (0, j)),
        out_shape=jax.ShapeDtypeStruct((8, n), F32),
        compiler_params=_tc_params(("arbitrary",)),
        name="mod",
    )(c8, w_mod, b_mod.reshape(1, n))
    return out[:b].reshape(b, 6, d)


def _norm_mod(x, g, scale, shift):
    y = x * lax.rsqrt(jnp.mean(x * x, axis=-1, keepdims=True) + EPS) * g
    return y * (1.0 + scale) + shift


def _inproj_kernel(x_ref, mod_ref, g_ref, w_ref, o_ref, h_ref, *, rows):
    @pl.when(pl.program_id(1) == 0)
    def _():
        g = g_ref[...]
        scale = mod_ref[0, 1:2, :]
        shift = mod_ref[0, 0:1, :]

        def body(r, carry):
            r0 = pl.multiple_of(r * rows, rows)
            h = _norm_mod(x_ref[pl.ds(r0, rows), :], g, scale, shift)
            h_ref[pl.ds(r0, rows), :] = h.astype(BF16)
            return carry

        lax.fori_loop(0, x_ref.shape[0] // rows, body, 0)

    o_ref[...] = jnp.dot(h_ref[...], w_ref[...], preferred_element_type=F32).astype(o_ref.dtype)


def _in_projection(x2, mod, g1, w_in_bf, seq):
    n, d = x2.shape
    p = w_in_bf.shape[1]
    tm, tn = min(1024, seq), 1024
    per_batch = seq // tm
    return pl.pallas_call(
        functools.partial(_inproj_kernel, rows=128),
        grid=(n // tm, p // tn),
        in_specs=[pl.BlockSpec((tm, d), lambda i, j: (i, 0)),
                  pl.BlockSpec((1, 6, d), lambda i, j: (i // per_batch, 0, 0)),
                  pl.BlockSpec((1, d), lambda i, j: (0, 0)),
                  pl.BlockSpec((d, tn), lambda i, j: (0, j))],
        out_specs=pl.BlockSpec((tm, tn), lambda i, j: (i, j)),
        out_shape=jax.ShapeDtypeStruct((n, p), BF16),
        scratch_shapes=[pltpu.VMEM((tm, d), BF16)],
        compiler_params=_tc_params(("arbitrary", "arbitrary")),
        name="inproj",
    )(x2, mod, g1.reshape(1, d), w_in_bf)


def _mixer_tables(seq, blk):
    half = HEAD_DIM // 2
    freqs = ROPE_BASE ** (-jnp.arange(half, dtype=F32) / half)
    ang = jnp.arange(seq, dtype=F32)[:, None] * freqs[None, :]
    cos = jnp.concatenate([jnp.cos(ang), jnp.cos(ang)], axis=-1)
    sin = jnp.concatenate([-jnp.sin(ang), jnp.sin(ang)], axis=-1)
    log_gamma = jnp.log1p(-jnp.exp2(-5.0 - jnp.arange(N_HEADS, dtype=F32)))
    idx = jnp.arange(blk, dtype=F32)
    dist = jnp.abs(idx[:, None] - idx[None, :])
    ck = jnp.arange(blk) // CHUNK
    visible = (ck[None, :] <= ck[:, None]).astype(F32)
    dmask = jnp.exp(log_gamma[:, None, None] * dist) * visible
    ones = jnp.ones((1, 1, HEAD_DIM), F32)
    dq = jnp.exp(log_gamma[:, None] * (idx + 1.0)[None])[:, :, None] * ones
    dkv = jnp.exp(log_gamma[:, None] * (blk - 1 - idx)[None])[:, :, None] * ones
    dblk = jnp.exp(log_gamma * blk)[:, None, None] * ones
    return cos, sin, dmask, dq, dkv, dblk


def _mixer_kernel(proj_ref, x_ref, mod_ref, cos_ref, sin_ref, dmask_ref, dq_ref, dkv_ref, dblk_ref,
                  convw_ref, wout_ref, g2_ref, wr_ref, br_ref,
                  x1_ref, h2_ref, lg_ref,
                  s_ref, zbuf_ref, y_ref, *, per_batch):
    blk = x_ref.shape[0]

    @pl.when(pl.program_id(0) % per_batch == 0)
    def _():
        s_ref[...] = jnp.zeros_like(s_ref)
        zbuf_ref[0:8, :] = jnp.zeros((8, D_CONV), F32)

    cos = cos_ref[...]
    sin = sin_ref[...]
    k_scale = HEAD_DIM ** -0.5
    nt = (((1,), (1,)), ((), ()))
    tn = (((0,), (0,)), ((), ()))
    for h in range(N_HEADS):
        c0 = h * HEAD_DIM
        q = proj_ref[:, c0:c0 + HEAD_DIM].astype(F32)
        k = proj_ref[:, D_RET + c0:D_RET + c0 + HEAD_DIM].astype(F32)
        v = proj_ref[:, 2 * D_RET + c0:2 * D_RET + c0 + HEAD_DIM]
        g = proj_ref[:, 3 * D_RET + c0:3 * D_RET + c0 + HEAD_DIM].astype(F32)
        qr = q * cos + pltpu.roll(q, HEAD_DIM // 2, 1) * sin
        kr = (k * cos + pltpu.roll(k, HEAD_DIM // 2, 1) * sin) * k_scale
        qb = qr.astype(BF16)
        kb = kr.astype(BF16)
        scores = lax.dot_general(qb, kb, nt, preferred_element_type=F32) * dmask_ref[h]
        intra = jnp.dot(scores.astype(BF16), v, preferred_element_type=F32)
        state = s_ref[h]
        cross = jnp.dot(qb, state.astype(BF16), preferred_element_type=F32) * dq_ref[h]
        kd = (kr * dkv_ref[h]).astype(BF16)
        kv = lax.dot_general(kd, v, tn, preferred_element_type=F32)
        s_ref[h] = dblk_ref[h] * state + kv
        o = intra + cross
        mu = jnp.mean(o, axis=-1, keepdims=True)
        dev = o - mu
        var = jnp.mean(dev * dev, axis=-1, keepdims=True)
        on = dev * lax.rsqrt(var + EPS)
        y_ref[:, c0:c0 + HEAD_DIM] = (g * jax.nn.sigmoid(g) * on).astype(BF16)

    cw = 256
    base = 4 * D_RET
    for cb in range(D_CONV // cw):
        lo = cb * cw
        cg = proj_ref[:, base + D_CONV + lo:base + D_CONV + lo + cw].astype(F32)
        u = proj_ref[:, base + 2 * D_CONV + lo:base + 2 * D_CONV + lo + cw].astype(F32)
        zbuf_ref[8:blk + 8, lo:lo + cw] = cg * u
    for cb in range(D_CONV // cw):
        lo = cb * cw
        z0 = zbuf_ref[8:blk + 8, lo:lo + cw]
        z1 = zbuf_ref[7:blk + 7, lo:lo + cw]
        z2 = zbuf_ref[6:blk + 6, lo:lo + cw]
        z = (convw_ref[2:3, lo:lo + cw] * z0 + convw_ref[1:2, lo:lo + cw] * z1
             + convw_ref[0:1, lo:lo + cw] * z2)
        bg = proj_ref[:, base + lo:base + lo + cw].astype(F32)
        y_ref[:, D_RET + lo:D_RET + lo + cw] = (bg * z).astype(BF16)
    zbuf_ref[0:8, :] = zbuf_ref[blk:blk + 8, :]

    mix = jnp.dot(y_ref[...], wout_ref[...], preferred_element_type=F32)
    x1 = x_ref[...] + mod_ref[0, 2:3, :] * mix
    x1_ref[...] = x1
    h2 = _norm_mod(x1, g2_ref[...], mod_ref[0, 4:5, :], mod_ref[0, 3:4, :])
    h2_ref[...] = h2
    lg_ref[...] = lax.dot_general(wr_ref[...], h2, nt, precision=lax.Precision.HIGHEST,
                                  preferred_element_type=F32) + br_ref[...]


def _mixer(proj, x2, mod, conv_w, w_out_bf, g2, w_router, b_router, seq):
    n, d = x2.shape
    blk = MIX_BLOCK
    per_batch = seq // blk
    cos, sin, dmask, dq, dkv, dblk = _mixer_tables(seq, blk)
    const2 = lambda i: (0, 0)
    const3 = lambda i: (0, 0, 0)
    return pl.pallas_call(
        functools.partial(_mixer_kernel, per_batch=per_batch),
        grid=(n // blk,),
        in_specs=[pl.BlockSpec((blk, proj.shape[1]), lambda i: (i, 0)),
                  pl.BlockSpec((blk, d), lambda i: (i, 0)),
                  pl.BlockSpec((1, 6, d), lambda i: (i // per_batch, 0, 0)),
                  pl.BlockSpec((blk, HEAD_DIM), lambda i: (i % per_batch, 0)),
                  pl.BlockSpec((blk, HEAD_DIM), lambda i: (i % per_batch, 0)),
                  pl.BlockSpec((N_HEADS, blk, blk), const3),
                  pl.BlockSpec((N_HEADS, blk, HEAD_DIM), const3),
                  pl.BlockSpec((N_HEADS, blk, HEAD_DIM), const3),
                  pl.BlockSpec((N_HEADS, 1, HEAD_DIM), const3),
                  pl.BlockSpec((CONV_WIDTH, D_CONV), const2),
                  pl.BlockSpec((D_RET + D_CONV, d), const2),
                  pl.BlockSpec((1, d), const2),
                  pl.BlockSpec((N_EXPERTS, d), const2),
                  pl.BlockSpec((N_EXPERTS, 1), const2)],
        out_specs=[pl.BlockSpec((blk, d), lambda i: (i, 0)),
                   pl.BlockSpec((blk, d), lambda i: (i, 0)),
                   pl.BlockSpec((N_EXPERTS, blk), lambda i: (0, i))],
        out_shape=[jax.ShapeDtypeStruct((n, d), F32),
                   jax.ShapeDtypeStruct((n, d), F32),
                   jax.ShapeDtypeStruct((N_EXPERTS, n), F32)],
        scratch_shapes=[pltpu.VMEM((N_HEADS, HEAD_DIM, HEAD_DIM), F32),
                        pltpu.VMEM((blk + 8, D_CONV), F32),
                        pltpu.VMEM((blk, D_RET + D_CONV), BF16)],
        compiler_params=_tc_params(("arbitrary",)),
        name="mixer",
    )(proj, x2, mod, cos, sin, dmask, dq, dkv, dblk, conv_w, w_out_bf, g2.reshape(1, d),
      w_router.T, b_router.reshape(N_EXPERTS, 1))


def _top_k_block(lg_ref, tri_ref, run_ref):
    n_e, bw = lg_ref.shape

    @pl.when(pl.program_id(0) == 0)
    def _():
        run_ref[...] = jnp.zeros_like(run_ref)

    l = lg_ref[...]
    eio = lax.broadcasted_iota(jnp.int32, (n_e, bw), 0)
    vals, hots = [], []
    for _ in range(TOP_K):
        m = jnp.max(l, axis=0, keepdims=True)
        idx = jnp.min(jnp.where(l == m, eio, n_e), axis=0, keepdims=True)
        hot = eio == idx
        vals.append(m)
        hots.append(hot)
        l = jnp.where(hot, -jnp.inf, l)
    sel = sum(jnp.where(hot, 1.0, 0.0) for hot in hots)
    incl = jnp.dot(sel.astype(BF16), tri_ref[...], preferred_element_type=F32)
    run = run_ref[:, 0:1]
    rank = incl - sel + run
    run_ref[...] = jnp.broadcast_to(run + incl[:, bw - 1:bw], run_ref.shape)
    return vals, hots, rank


def _count_kernel(lg_ref, tri_ref, cnt_ref, run_ref):
    _top_k_block(lg_ref, tri_ref, run_ref)
    cnt_ref[...] = run_ref[...]


def _route_kernel(lg_ref, tri_ref, low_ref, tot_ref, pos_ref, w_ref, run_ref, *, max_slots):
    vals, hots, rank = _top_k_block(lg_ref, tri_ref, run_ref)
    ex = [jnp.exp(v - vals[0]) for v in vals]
    den = ex[0] + ex[1] + ex[2] + ex[3]
    for k in range(TOP_K):
        w_ref[k:k + 1, :] = ex[k] / den
    tot = tot_ref[...]
    n_slots = sum(jnp.where(tot > float(m * SLOT_ROWS), 1.0, 0.0) for m in range(max_slots))
    start = jnp.dot(low_ref[...], n_slots.astype(BF16), preferred_element_type=F32)
    dest = start[:, 0:1] * float(SLOT_ROWS) + rank
    for k in range(TOP_K):
        pos_ref[k:k + 1, :] = jnp.sum(jnp.where(hots[k], dest, 0.0), axis=0,
                                      keepdims=True).astype(jnp.int32)


def _route(logits_t):
    n_e, n = logits_t.shape
    bw = 512
    max_slots = -(-n // SLOT_ROWS)
    tri = (jnp.arange(bw)[:, None] <= jnp.arange(bw)[None, :]).astype(BF16)
    low = (jnp.arange(n_e)[None, :] < jnp.arange(n_e)[:, None]).astype(BF16)
    lg_spec = pl.BlockSpec((n_e, bw), lambda c: (0, c))
    tri_spec = pl.BlockSpec((bw, bw), lambda c: (0, 0))
    cnt_spec = pl.BlockSpec((n_e, 128), lambda c: (0, 0))
    run = pltpu.VMEM((n_e, 128), F32)
    counts = pl.pallas_call(
        _count_kernel,
        grid=(n // bw,),
        in_specs=[lg_spec, tri_spec],
        out_specs=cnt_spec,
        out_shape=jax.ShapeDtypeStruct((n_e, 128), F32),
        scratch_shapes=[run],
        compiler_params=_tc_params(("arbitrary",)),
        name="route_count",
    )(logits_t, tri)
    pos, w_top = pl.pallas_call(
        functools.partial(_route_kernel, max_slots=max_slots),
        grid=(n // bw,),
        in_specs=[lg_spec, tri_spec, pl.BlockSpec((n_e, n_e), lambda c: (0, 0)), cnt_spec],
        out_specs=[pl.BlockSpec((TOP_K, bw), lambda c: (0, c)),
                   pl.BlockSpec((TOP_K, bw), lambda c: (0, c))],
        out_shape=[jax.ShapeDtypeStruct((TOP_K, n), jnp.int32),
                   jax.ShapeDtypeStruct((TOP_K, n), F32)],
        scratch_shapes=[run],
        compiler_params=_tc_params(("arbitrary",)),
        name="route",
    )(logits_t, tri, low, counts)
    return pos, w_top, counts


def _sc_mesh():
    return plsc.VectorSubcoreMesh(core_axis_name="c", subcore_axis_name="s")


def _sc_worker_id():
    return lax.axis_index("s") * lax.axis_size("c") + lax.axis_index("c")


def _dispatch_rows(h2, pos):
    n, d = h2.shape
    ch = SC_ROWS_PER_COPY
    per_w = n // SC_WORKERS
    n_ch = per_w // ch
    total_rows = (_max_slots(n) + 1) * SLOT_ROWS
    idx = pos.reshape(TOP_K, SC_WORKERS, n_ch, ch).transpose(1, 2, 0, 3).reshape(
        SC_WORKERS, n_ch * TOP_K, ch)

    @functools.partial(
        pl.kernel, mesh=_sc_mesh(),
        out_type=jax.ShapeDtypeStruct((total_rows, d), h2.dtype),
        scratch_types=[pltpu.VMEM((n_ch * TOP_K, ch), jnp.int32),
                       pltpu.VMEM((ch, d), h2.dtype),
                       pltpu.SemaphoreType.DMA])
    def scatter(x_hbm, idx_hbm, o_hbm, idx_v, rows_v, sem):
        wid = _sc_worker_id()
        pltpu.sync_copy(idx_hbm.at[wid], idx_v)

        @pl.loop(0, n_ch)
        def _(c):
            r0 = pl.multiple_of(wid * per_w + c * ch, ch)
            pltpu.sync_copy(x_hbm.at[pl.ds(r0, ch)], rows_v)
            for k in range(TOP_K):
                pltpu.async_copy(rows_v, o_hbm.at[idx_v.at[c * TOP_K + k]], sem).wait()

    return scatter(h2, idx)


def _combine_rows(ys, pos_flat):
    n = pos_flat.shape[0]
    d = ys.shape[1]
    ch = SC_ROWS_PER_COPY
    per_w = n // SC_WORKERS
    n_ch = per_w // ch
    idx = pos_flat.reshape(SC_WORKERS, n_ch, ch)

    @functools.partial(
        pl.kernel, mesh=_sc_mesh(),
        out_type=jax.ShapeDtypeStruct((n, d), ys.dtype),
        scratch_types=[pltpu.VMEM((n_ch, ch), jnp.int32),
                       pltpu.VMEM((ch, d), ys.dtype),
                       pltpu.SemaphoreType.DMA])
    def gather(y_hbm, idx_hbm, o_hbm, idx_v, rows_v, sem):
        wid = _sc_worker_id()
        pltpu.sync_copy(idx_hbm.at[wid], idx_v)

        @pl.loop(0, n_ch)
        def _(c):
            r0 = pl.multiple_of(wid * per_w + c * ch, ch)
            pltpu.async_copy(y_hbm.at[idx_v.at[c]], rows_v, sem).wait()
            pltpu.sync_copy(rows_v, o_hbm.at[pl.ds(r0, ch)])

    return gather(ys, idx)


def _max_slots(n_tokens):
    return N_EXPERTS + (n_tokens * TOP_K) // SLOT_ROWS


def _expert_kernel(se_ref, sr_ref, nu_ref,
                   xs_ref, wg_ref, wu_ref, bg_ref, bu_ref, wd_ref, bd_ref,
                   ys_ref,
                   xb_ref, wgu_ref, wdb_ref, act_ref, *, n_a):
    s = pl.program_id(0)
    j = pl.program_id(1)
    valid = s < nu_ref[0]
    rows = sr_ref[s]
    n_blk = (rows + ROW_BLOCK - 1) // ROW_BLOCK
    tf = wg_ref.shape[2]

    @pl.when(valid & (j == 0))
    def _():
        def body(rb, carry):
            r0 = pl.multiple_of(rb * ROW_BLOCK, ROW_BLOCK)
            rid = r0 + lax.broadcasted_iota(jnp.int32, (ROW_BLOCK, 1), 0)
            x = jnp.where(rid < rows, xs_ref[pl.ds(r0, ROW_BLOCK), :], 0.0)
            xb_ref[pl.ds(r0, ROW_BLOCK), :] = x.astype(BF16)
            return carry

        lax.fori_loop(0, n_blk, body, 0)

    @pl.when(valid & (j < n_a))
    def _():
        wgu_ref[:, 0:tf] = wg_ref[0].astype(BF16)
        wgu_ref[:, tf:2 * tf] = wu_ref[0].astype(BF16)
        b_gate = bg_ref[0]
        b_up = bu_ref[0]

        def body(rb, carry):
            r0 = pl.multiple_of(rb * ROW_BLOCK, ROW_BLOCK)
            gu = jnp.dot(xb_ref[pl.ds(r0, ROW_BLOCK), :], wgu_ref[...], preferred_element_type=F32)
            gate = jnp.minimum(gu[:, 0:tf] + b_gate, SWIGLU_LIMIT)
            up = jnp.clip(gu[:, tf:2 * tf] + b_up, -SWIGLU_LIMIT, SWIGLU_LIMIT)
            act = (up + 1.0) * (gate * jax.nn.sigmoid(SWIGLU_ALPHA * gate))
            act_ref[j, pl.ds(r0, ROW_BLOCK), :] = act.astype(BF16)
            return carry

        lax.fori_loop(0, n_blk, body, 0)

    @pl.when(valid & (j >= n_a))
    def _():
        wdb_ref[...] = wd_ref[0].astype(BF16)
        b_down = bd_ref[0]

        def body(rb, carry):
            r0 = pl.multiple_of(rb * ROW_BLOCK, ROW_BLOCK)
            acc = jnp.broadcast_to(b_down, (ROW_BLOCK, b_down.shape[-1]))
            for a in range(n_a):
                acc = acc + jnp.dot(act_ref[a, pl.ds(r0, ROW_BLOCK), :],
                                    wdb_ref[a * tf:(a + 1) * tf, :], preferred_element_type=F32)
            ys_ref[pl.ds(r0, ROW_BLOCK), :] = acc
            return carry

        lax.fori_loop(0, n_blk, body, 0)

        def fill(rb, carry):
            r0 = pl.multiple_of(rb * ROW_BLOCK, ROW_BLOCK)
            ys_ref[pl.ds(r0, ROW_BLOCK), :] = jnp.zeros((ROW_BLOCK, ys_ref.shape[1]), F32)
            return carry

        lax.fori_loop(n_blk, SLOT_ROWS // ROW_BLOCK, fill, 0)

    @pl.when(jnp.logical_not(valid))
    def _():
        ys_ref[...] = jnp.zeros_like(ys_ref)


def _experts(xs, slot_expert, slot_rows, n_used, w_gate_up, b_gate_up, w_down, b_down):
    n_e, d, two_ff = w_gate_up.shape
    d_ff = two_ff // 2
    tf, tn = FF_TILE, DN_TILE
    n_a, n_b = d_ff // tf, d // tn
    max_slots = slot_expert.shape[0]

    def slot_c(s, nu):
        return jnp.minimum(s, nu[0] - 1)

    def j_a(s, j, nu):
        return jnp.where(s < nu[0], jnp.minimum(j, n_a - 1), n_a - 1)

    def j_b(s, j, nu):
        return jnp.where(s < nu[0], jnp.clip(j - n_a, 0, n_b - 1), n_b - 1)

    grid_spec = pltpu.PrefetchScalarGridSpec(
        num_scalar_prefetch=3,
        grid=(max_slots, n_a + n_b),
        in_specs=[
            pl.BlockSpec((SLOT_ROWS, d), lambda s, j, se, sr, nu: (slot_c(s, nu), 0)),
            pl.BlockSpec((1, d, tf), lambda s, j, se, sr, nu: (se[slot_c(s, nu)], 0, j_a(s, j, nu))),
            pl.BlockSpec((1, d, tf), lambda s, j, se, sr, nu: (se[slot_c(s, nu)], 0, n_a + j_a(s, j, nu))),
            pl.BlockSpec((1, 1, tf), lambda s, j, se, sr, nu: (se[slot_c(s, nu)], 0, j_a(s, j, nu))),
            pl.BlockSpec((1, 1, tf), lambda s, j, se, sr, nu: (se[slot_c(s, nu)], 0, n_a + j_a(s, j, nu))),
            pl.BlockSpec((1, d_ff, tn), lambda s, j, se, sr, nu: (se[slot_c(s, nu)], 0, j_b(s, j, nu))),
            pl.BlockSpec((1, 1, tn), lambda s, j, se, sr, nu: (se[slot_c(s, nu)], 0, j_b(s, j, nu))),
        ],
        out_specs=pl.BlockSpec(
            (SLOT_ROWS, tn),
            lambda s, j, se, sr, nu: (jnp.where(s < nu[0], s, max_slots), j_b(s, j, nu))),
        scratch_shapes=[pltpu.VMEM((SLOT_ROWS, d), BF16),
                        pltpu.VMEM((d, 2 * tf), BF16),
                        pltpu.VMEM((d_ff, tn), BF16),
                        pltpu.VMEM((n_a, SLOT_ROWS, tf), BF16)],
    )
    return pl.pallas_call(
        functools.partial(_expert_kernel, n_a=n_a),
        grid_spec=grid_spec,
        out_shape=jax.ShapeDtypeStruct(((max_slots + 1) * SLOT_ROWS, d), F32),
        compiler_params=_tc_params(("arbitrary", "arbitrary")),
        name="experts",
    )(slot_expert, slot_rows, n_used, xs, w_gate_up, w_gate_up,
      b_gate_up.reshape(n_e, 1, two_ff), b_gate_up.reshape(n_e, 1, two_ff),
      w_down, b_down.reshape(n_e, 1, d))


def _slot_tables(counts, n_tokens):
    max_slots = _max_slots(n_tokens)
    n_slots = (counts + SLOT_ROWS - 1) // SLOT_ROWS
    slot_end = jnp.cumsum(n_slots)
    slot_start = slot_end - n_slots
    n_used = slot_end[-1]
    sid = jnp.arange(max_slots, dtype=jnp.int32)
    expert = jnp.minimum(jnp.searchsorted(slot_end, sid, side="right"), N_EXPERTS - 1).astype(jnp.int32)
    local = sid - slot_start[expert]
    rows = jnp.clip(counts[expert] - local * SLOT_ROWS, 0, SLOT_ROWS)
    rows = jnp.where(sid < n_used, rows, 0).astype(jnp.int32)
    return expert, rows, n_used.reshape(1).astype(jnp.int32)


def _final_kernel(x1_ref, y4_ref, w_ref, mod_ref, g_ref, o_ref):
    d = x1_ref.shape[1]
    moe = w_ref[:, 0:1] * y4_ref[:, 0:d]
    for k in range(1, TOP_K):
        moe = moe + w_ref[:, k:k + 1] * y4_ref[:, k * d:(k + 1) * d]
    x = x1_ref[...] + mod_ref[0, 5:6, :] * moe
    o_ref[...] = x * lax.rsqrt(jnp.mean(x * x, axis=-1, keepdims=True) + EPS) * g_ref[...]


def _final(x1, y4, w_tok, mod, final_g, seq):
    n, d = x1.shape
    tm = 256
    per_batch = seq // tm
    return pl.pallas_call(
        _final_kernel,
        grid=(n // tm,),
        in_specs=[pl.BlockSpec((tm, d), lambda i: (i, 0)),
                  pl.BlockSpec((tm, TOP_K * d), lambda i: (i, 0)),
                  pl.BlockSpec((tm, TOP_K), lambda i: (i, 0)),
                  pl.BlockSpec((1, 6, d), lambda i: (i // per_batch, 0, 0)),
                  pl.BlockSpec((1, d), lambda i: (0, 0))],
        out_specs=pl.BlockSpec((tm, d), lambda i: (i, 0)),
        out_shape=jax.ShapeDtypeStruct((n, d), F32),
        compiler_params=_tc_params(("arbitrary",)),
        name="final",
    )(x1, y4, w_tok, mod, final_g.reshape(1, d))


def kernel(x, c, norm1_g, w_mod, b_mod, w_in, conv_w, w_out, norm2_g, w_router, b_router,
           w_gate_up, b_gate_up, w_down, b_down, final_g):
    b, seq, d = x.shape
    n = b * seq
    x2 = x.reshape(n, d)
    mod = _modulation(c, w_mod[0], b_mod[0])
    proj = _in_projection(x2, mod, norm1_g[0], w_in[0].astype(BF16), seq)
    x1, h2, logits_t = _mixer(proj, x2, mod, conv_w[0], w_out[0].astype(BF16), norm2_g[0],
                              w_router[0], b_router[0], seq)
    pos, w_top, counts = _route(logits_t)
    slot_expert, slot_rows, n_used = _slot_tables(counts[:, 0].astype(jnp.int32), n)
    xs = _dispatch_rows(h2, pos)
    ys = _experts(xs, slot_expert, slot_rows, n_used, w_gate_up[0], b_gate_up[0], w_down[0], b_down[0])
    y4 = _combine_rows(ys, pos.T.reshape(n * TOP_K))
    out = _final(x1, y4.reshape(n, TOP_K * d), w_top.T, mod, final_g, seq)
    return out.reshape(b, seq, d)
```

```python
import functools
import math

import jax
import jax.numpy as jnp
from jax import lax
from jax.experimental import pallas as pl
from jax.experimental.pallas import tpu as pltpu
from jax.experimental.pallas import tpu_sc as plsc

F32 = jnp.float32
BF16 = jnp.bfloat16

CHUNK = 64
N_HEADS = 8
HEAD_DIM = 128
D_RET = N_HEADS * HEAD_DIM
D_CONV = 1024
CONV_WIDTH = 3
N_EXPERTS = 32
TOP_K = 4
SWIGLU_LIMIT = 7.0
SWIGLU_ALPHA = 1.702
ROPE_BASE = 10000.0
EPS = 1e-6

VMEM_LIMIT_BYTES = 58 * 1024 * 1024
SC_WORKERS = 32
SC_ROWS_PER_COPY = 32

MIX_BLOCK = 256
ROW_BLOCK = 128
SLOT_ROWS = 9 * ROW_BLOCK
FF_TILE = 256
DN_TILE = 512


def _tc_params(sem):
    return pltpu.CompilerParams(dimension_semantics=sem, vmem_limit_bytes=VMEM_LIMIT_BYTES)


def _mod_kernel(c_ref, w_ref, b_ref, o_ref):
    c = c_ref[...]
    ca = (c * jax.nn.sigmoid(c)).astype(BF16)
    o_ref[...] = jnp.dot(ca, w_ref[...].astype(BF16), preferred_element_type=F32) + b_ref[...]


def _modulation(c, w_mod, b_mod):
    b, d = c.shape
    n = w_mod.shape[1]
    tn = 1024
    c8 = jnp.zeros((8, d), F32).at[:b].set(c)
    out = pl.pallas_call(
        _mod_kernel,
        grid=(n // tn,),
        in_specs=[pl.BlockSpec((8, d), lambda j: (0, 0)),
                  pl.BlockSpec((d, tn), lambda j: (0, j)),
                  pl.BlockSpec((1, tn), lambda j: (0, j))],
        out_specs=pl.BlockSpec((8, tn), lambda j: (0, j)),
        out_shape=jax.ShapeDtypeStruct((8, n), F32),
        compiler_params=_tc_params(("arbitrary",)),
        name="mod",
    )(c8, w_mod, b_mod.reshape(1, n))
    return out[:b].reshape(b, 6, d)


def _norm_mod(x, g, scale, shift):
    y = x * lax.rsqrt(jnp.mean(x * x, axis=-1, keepdims=True) + EPS) * g
    return y * (1.0 + scale) + shift


def _inproj_kernel(x_ref, mod_ref, g_ref, w_ref, o_ref, h_ref, *, rows):
    @pl.when(pl.program_id(1) == 0)
    def _():
        g = g_ref[...]
        scale = mod_ref[0, 1:2, :]
        shift = mod_ref[0, 0:1, :]

        def body(r, carry):
            r0 = pl.multiple_of(r * rows, rows)
            h = _norm_mod(x_ref[pl.ds(r0, rows), :], g, scale, shift)
            h_ref[pl.ds(r0, rows), :] = h.astype(BF16)
            return carry

        lax.fori_loop(0, x_ref.shape[0] // rows, body, 0)

    o_ref[...] = jnp.dot(h_ref[...], w_ref[...], preferred_element_type=F32).astype(o_ref.dtype)


def _in_projection(x2, mod, g1, w_in_bf, seq):
    n, d = x2.shape
    p = w_in_bf.shape[1]
    tm, tn = min(1024, seq), 1024
    per_batch = seq // tm
    return pl.pallas_call(
        functools.partial(_inproj_kernel, rows=128),
        grid=(n // tm, p // tn),
        in_specs=[pl.BlockSpec((tm, d), lambda i, j: (i, 0)),
                  pl.BlockSpec((1, 6, d), lambda i, j: (i // per_batch, 0, 0)),
                  pl.BlockSpec((1, d), lambda i, j: (0, 0)),
                  pl.BlockSpec((d, tn), lambda i, j: (0, j))],
        out_specs=pl.BlockSpec((tm, tn), lambda i, j: (i, j)),
        out_shape=jax.ShapeDtypeStruct((n, p), BF16),
        scratch_shapes=[pltpu.VMEM((tm, d), BF16)],
        compiler_params=_tc_params(("arbitrary", "arbitrary")),
        name="inproj",
    )(x2, mod, g1.reshape(1, d), w_in_bf)


def _mixer_tables(seq, blk):
    half = HEAD_DIM // 2
    freqs = ROPE_BASE ** (-jnp.arange(half, dtype=F32) / half)
    ang = jnp.arange(seq, dtype=F32)[:, None] * freqs[None, :]
    cos = jnp.concatenate([jnp.cos(ang), jnp.cos(ang)], axis=-1)
    sin = jnp.concatenate([-jnp.sin(ang), jnp.sin(ang)], axis=-1)
    log_gamma = jnp.log1p(-jnp.exp2(-5.0 - jnp.arange(N_HEADS, dtype=F32)))
    idx = jnp.arange(blk, dtype=F32)
    dist = jnp.abs(idx[:, None] - idx[None, :])
    ck = jnp.arange(blk) // CHUNK
    visible = (ck[None, :] <= ck[:, None]).astype(F32)
    dmask = jnp.exp(log_gamma[:, None, None] * dist) * visible
    ones = jnp.ones((1, 1, HEAD_DIM), F32)
    dq = jnp.exp(log_gamma[:, None] * (idx + 1.0)[None])[:, :, None] * ones
    dkv = jnp.exp(log_gamma[:, None] * (blk - 1 - idx)[None])[:, :, None] * ones
    dblk = jnp.exp(log_gamma * blk)[:, None, None] * ones
    return cos, sin, dmask, dq, dkv, dblk


def _mixer_kernel(proj_ref, x_ref, mod_ref, cos_ref, sin_ref, dmask_ref, dq_ref, dkv_ref, dblk_ref,
                  convw_ref, wout_ref, g2_ref, wr_ref, br_ref,
                  x1_ref, h2_ref, lg_ref,
                  s_ref, zbuf_ref, y_ref, *, per_batch):
    blk = x_ref.shape[0]

    @pl.when(pl.program_id(0) % per_batch == 0)
    def _():
        s_ref[...] = jnp.zeros_like(s_ref)
        zbuf_ref[0:8, :] = jnp.zeros((8, D_CONV), F32)

    cos = cos_ref[...]
    sin = sin_ref[...]
    k_scale = HEAD_DIM ** -0.5
    nt = (((1,), (1,)), ((), ()))
    tn = (((0,), (0,)), ((), ()))
    for h in range(N_HEADS):
        c0 = h * HEAD_DIM
        q = proj_ref[:, c0:c0 + HEAD_DIM].astype(F32)
        k = proj_ref[:, D_RET + c0:D_RET + c0 + HEAD_DIM].astype(F32)
        v = proj_ref[:, 2 * D_RET + c0:2 * D_RET + c0 + HEAD_DIM]
        g = proj_ref[:, 3 * D_RET + c0:3 * D_RET + c0 + HEAD_DIM].astype(F32)
        qr = q * cos + pltpu.roll(q, HEAD_DIM // 2, 1) * sin
        kr = (k * cos + pltpu.roll(k, HEAD_DIM // 2, 1) * sin) * k_scale
        qb = qr.astype(BF16)
        kb = kr.astype(BF16)
        scores = lax.dot_general(qb, kb, nt, preferred_element_type=F32) * dmask_ref[h]
        intra = jnp.dot(scores.astype(BF16), v, preferred_element_type=F32)
        state = s_ref[h]
        cross = jnp.dot(qb, state.astype(BF16), preferred_element_type=F32) * dq_ref[h]
        kd = (kr * dkv_ref[h]).astype(BF16)
        kv = lax.dot_general(kd, v, tn, preferred_element_type=F32)
        s_ref[h] = dblk_ref[h] * state + kv
        o = intra + cross
        mu = jnp.mean(o, axis=-1, keepdims=True)
        dev = o - mu
        var = jnp.mean(dev * dev, axis=-1, keepdims=True)
        on = dev * lax.rsqrt(var + EPS)
        y_ref[:, c0:c0 + HEAD_DIM] = (g * jax.nn.sigmoid(g) * on).astype(BF16)

    cw = 256
    base = 4 * D_RET
    for cb in range(D_CONV // cw):
        lo = cb * cw
        cg = proj_ref[:, base + D_CONV + lo:base + D_CONV + lo + cw].astype(F32)
        u = proj_ref[:, base + 2 * D_CONV + lo:base + 2 * D_CONV + lo + cw].astype(F32)
        zbuf_ref[8:blk + 8, lo:lo + cw] = cg * u
    for cb in range(D_CONV // cw):
        lo = cb * cw
        z0 = zbuf_ref[8:blk + 8, lo:lo + cw]
        z1 = zbuf_ref[7:blk + 7, lo:lo + cw]
        z2 = zbuf_ref[6:blk + 6, lo:lo + cw]
        z = (convw_ref[2:3, lo:lo + cw] * z0 + convw_ref[1:2, lo:lo + cw] * z1
             + convw_ref[0:1, lo:lo + cw] * z2)
        bg = proj_ref[:, base + lo:base + lo + cw].astype(F32)
        y_ref[:, D_RET + lo:D_RET + lo + cw] = (bg * z).astype(BF16)
    zbuf_ref[0:8, :] = zbuf_ref[blk:blk + 8, :]

    mix = jnp.dot(y_ref[...], wout_ref[...], preferred_element_type=F32)
    x1 = x_ref[...] + mod_ref[0, 2:3, :] * mix
    x1_ref[...] = x1
    h2 = _norm_mod(x1, g2_ref[...], mod_ref[0, 4:5, :], mod_ref[0, 3:4, :])
    h2_ref[...] = h2
    lg_ref[...] = lax.dot_general(wr_ref[...], h2, nt, precision=lax.Precision.HIGHEST,
                                  preferred_element_type=F32) + br_ref[...]


def _mixer(proj, x2, mod, conv_w, w_out_bf, g2, w_router, b_router, seq):
    n, d = x2.shape
    blk = MIX_BLOCK
    per_batch = seq // blk
    cos, sin, dmask, dq, dkv, dblk = _mixer_tables(seq, blk)
    const2 = lambda i: (0, 0)
    const3 = lambda i: (0, 0, 0)
    return pl.pallas_call(
        functools.partial(_mixer_kernel, per_batch=per_batch),
        grid=(n // blk,),
        in_specs=[pl.BlockSpec((blk, proj.shape[1]), lambda i: (i, 0)),
                  pl.BlockSpec((blk, d), lambda i: (i, 0)),
                  pl.BlockSpec((1, 6, d), lambda i: (i // per_batch, 0, 0)),
                  pl.BlockSpec((blk, HEAD_DIM), lambda i: (i % per_batch, 0)),
                  pl.BlockSpec((blk, HEAD_DIM), lambda i: (i % per_batch, 0)),
                  pl.BlockSpec((N_HEADS, blk, blk), const3),
                  pl.BlockSpec((N_HEADS, blk, HEAD_DIM), const3),
                  pl.BlockSpec((N_HEADS, blk, HEAD_DIM), const3),
                  pl.BlockSpec((N_HEADS, 1, HEAD_DIM), const3),
                  pl.BlockSpec((CONV_WIDTH, D_CONV), const2),
                  pl.BlockSpec((D_RET + D_CONV, d), const2),
                  pl.BlockSpec((1, d), const2),
                  pl.BlockSpec((N_EXPERTS, d), const2),
                  pl.BlockSpec((N_EXPERTS, 1), const2)],
        out_specs=[pl.BlockSpec((blk, d), lambda i: (i, 0)),
                   pl.BlockSpec((blk, d), lambda i: (i, 0)),
                   pl.BlockSpec((N_EXPERTS, blk), lambda i: (0, i))],
        out_shape=[jax.ShapeDtypeStruct((n, d), F32),
                   jax.ShapeDtypeStruct((n, d), F32),
                   jax.ShapeDtypeStruct((N_EXPERTS, n), F32)],
        scratch_shapes=[pltpu.VMEM((N_HEADS, HEAD_DIM, HEAD_DIM), F32),
                        pltpu.VMEM((blk + 8, D_CONV), F32),
                        pltpu.VMEM((blk, D_RET + D_CONV), BF16)],
        compiler_params=_tc_params(("arbitrary",)),
        name="mixer",
    )(proj, x2, mod, cos, sin, dmask, dq, dkv, dblk, conv_w, w_out_bf, g2.reshape(1, d),
      w_router.T, b_router.reshape(N_EXPERTS, 1))


def _top_k_block(lg_ref, tri_ref, run_ref):
    n_e, bw = lg_ref.shape

    @pl.when(pl.program_id(0) == 0)
    def _():
        run_ref[...] = jnp.zeros_like(run_ref)

    l = lg_ref[...]
    eio = lax.broadcasted_iota(jnp.int32, (n_e, bw), 0)
    vals, hots = [], []
    for _ in range(TOP_K):
        m = jnp.max(l, axis=0, keepdims=True)
        idx = jnp.min(jnp.where(l == m, eio, n_e), axis=0, keepdims=True)
        hot = eio == idx
        vals.append(m)
        hots.append(hot)
        l = jnp.where(hot, -jnp.inf, l)
    sel = sum(jnp.where(hot, 1.0, 0.0) for hot in hots)
    incl = jnp.dot(sel.astype(BF16), tri_ref[...], preferred_element_type=F32)
    run = run_ref[:, 0:1]
    rank = incl - sel + run
    run_ref[...] = jnp.broadcast_to(run + incl[:, bw - 1:bw], run_ref.shape)
    return vals, hots, rank


def _count_kernel(lg_ref, tri_ref, cnt_ref, run_ref):
    _top_k_block(lg_ref, tri_ref, run_ref)
    cnt_ref[...] = run_ref[...]


def _route_kernel(lg_ref, tri_ref, low_ref, tot_ref, pos_ref, w_ref, run_ref, *, max_slots):
    vals, hots, rank = _top_k_block(lg_ref, tri_ref, run_ref)
    ex = [jnp.exp(v - vals[0]) for v in vals]
    den = ex[0] + ex[1] + ex[2] + ex[3]
    for k in range(TOP_K):
        w_ref[k:k + 1, :] = ex[k] / den
    tot = tot_ref[...]
    n_slots = sum(jnp.where(tot > float(m * SLOT_ROWS), 1.0, 0.0) for m in range(max_slots))
    start = jnp.dot(low_ref[...], n_slots.astype(BF16), preferred_element_type=F32)
    dest = start[:, 0:1] * float(SLOT_ROWS) + rank
    for k in range(TOP_K):
        pos_ref[k:k + 1, :] = jnp.sum(jnp.where(hots[k], dest, 0.0), axis=0,
                                      keepdims=True).astype(jnp.int32)


def _route(logits_t):
    n_e, n = logits_t.shape
    bw = 512
    max_slots = -(-n // SLOT_ROWS)
    tri = (jnp.arange(bw)[:, None] <= jnp.arange(bw)[None, :]).astype(BF16)
    low = (jnp.arange(n_e)[None, :] < jnp.arange(n_e)[:, None]).astype(BF16)
    lg_spec = pl.BlockSpec((n_e, bw), lambda c: (0, c))
    tri_spec = pl.BlockSpec((bw, bw), lambda c: (0, 0))
    cnt_spec = pl.BlockSpec((n_e, 128), lambda c: (0, 0))
    run = pltpu.VMEM((n_e, 128), F32)
    counts = pl.pallas_call(
        _count_kernel,
        grid=(n // bw,),
        in_specs=[lg_spec, tri_spec],
        out_specs=cnt_spec,
        out_shape=jax.ShapeDtypeStruct((n_e, 128), F32),
        scratch_shapes=[run],
        compiler_params=_tc_params(("arbitrary",)),
        name="route_count",
    )(logits_t, tri)
    pos, w_top = pl.pallas_call(
        functools.partial(_route_kernel, max_slots=max_slots),
        grid=(n // bw,),
        in_specs=[lg_spec, tri_spec, pl.BlockSpec((n_e, n_e), lambda c: (0, 0)), cnt_spec],
        out_specs=[pl.BlockSpec((TOP_K, bw), lambda c: (0, c)),
                   pl.BlockSpec((TOP_K, bw), lambda c: (0, c))],
        out_shape=[jax.ShapeDtypeStruct((TOP_K, n), jnp.int32),
                   jax.ShapeDtypeStruct((TOP_K, n), F32)],
        scratch_shapes=[run],
        compiler_params=_tc_params(("arbitrary",)),
        name="route",
    )(logits_t, tri, low, counts)
    return pos, w_top, counts


def _sc_mesh():
    return plsc.VectorSubcoreMesh(core_axis_name="c", subcore_axis_name="s")


def _sc_worker_id():
    return lax.axis_index("s") * lax.axis_size("c") + lax.axis_index("c")


def _dispatch_rows(h2, pos):
    n, d = h2.shape
    ch = SC_ROWS_PER_COPY
    per_w = n // SC_WORKERS
    n_ch = per_w // ch
    total_rows = (_max_slots(n) + 1) * SLOT_ROWS
    idx = pos.reshape(TOP_K, SC_WORKERS, n_ch, ch).transpose(1, 2, 0, 3).reshape(
        SC_WORKERS, n_ch * TOP_K, ch)

    @functools.partial(
        pl.kernel, mesh=_sc_mesh(),
        out_type=jax.ShapeDtypeStruct((total_rows, d), h2.dtype),
        scratch_types=[pltpu.VMEM((n_ch * TOP_K, ch), jnp.int32),
                       pltpu.VMEM((ch, d), h2.dtype),
                       pltpu.SemaphoreType.DMA])
    def scatter(x_hbm, idx_hbm, o_hbm, idx_v, rows_v, sem):
        wid = _sc_worker_id()
        pltpu.sync_copy(idx_hbm.at[wid], idx_v)

        @pl.loop(0, n_ch)
        def _(c):
            r0 = pl.multiple_of(wid * per_w + c * ch, ch)
            pltpu.sync_copy(x_hbm.at[pl.ds(r0, ch)], rows_v)
            for k in range(TOP_K):
                pltpu.async_copy(rows_v, o_hbm.at[idx_v.at[c * TOP_K + k]], sem).wait()

    return scatter(h2, idx)


def _combine_rows(ys, pos_flat):
    n = pos_flat.shape[0]
    d = ys.shape[1]
    ch = SC_ROWS_PER_COPY
    per_w = n // SC_WORKERS
    n_ch = per_w // ch
    idx = pos_flat.reshape(SC_WORKERS, n_ch, ch)

    @functools.partial(
        pl.kernel, mesh=_sc_mesh(),
        out_type=jax.ShapeDtypeStruct((n, d), ys.dtype),
        scratch_types=[pltpu.VMEM((n_ch, ch), jnp.int32),
                       pltpu.VMEM((ch, d), ys.dtype),
                       pltpu.SemaphoreType.DMA])
    def gather(y_hbm, idx_hbm, o_hbm, idx_v, rows_v, sem):
        wid = _sc_worker_id()
        pltpu.sync_copy(idx_hbm.at[wid], idx_v)

        @pl.loop(0, n_ch)
        def _(c):
            r0 = pl.multiple_of(wid * per_w + c * ch, ch)
            pltpu.async_copy(y_hbm.at[idx_v.at[c]], rows_v, sem).wait()
            pltpu.sync_copy(rows_v, o_hbm.at[pl.ds(r0, ch)])

    return gather(ys, idx)


def _max_slots(n_tokens):
    return N_EXPERTS + (n_tokens * TOP_K) // SLOT_ROWS


def _expert_kernel(se_ref, sr_ref, nu_ref,
                   xs_ref, wg_ref, wu_ref, bg_ref, bu_ref, wd_ref, bd_ref,
                   ys_ref,
                   xb_ref, wgu_ref, wdb_ref, act_ref, *, n_a):
    s = pl.program_id(0)
    j = pl.program_id(1)
    valid = s < nu_ref[0]
    rows = sr_ref[s]
    n_blk = (rows + ROW_BLOCK - 1) // ROW_BLOCK
    tf = wg_ref.shape[2]

    def for_row_blocks(fn):
        n_quads = n_blk // 4

        def body(rb, carry):
            fn(pl.multiple_of(rb * 4 * ROW_BLOCK, 4 * ROW_BLOCK), 4 * ROW_BLOCK)
            return carry

        lax.fori_loop(0, n_quads, body, 0)

        @pl.when((n_blk // 2) % 2 == 1)
        def _():
            fn(pl.multiple_of(n_quads * 4 * ROW_BLOCK, 4 * ROW_BLOCK), 2 * ROW_BLOCK)

        @pl.when(n_blk % 2 == 1)
        def _():
            fn(pl.multiple_of((n_blk // 2) * 2 * ROW_BLOCK, 2 * ROW_BLOCK), ROW_BLOCK)

    @pl.when(valid & (j == 0))
    def _():
        def body(rb, carry):
            r0 = pl.multiple_of(rb * ROW_BLOCK, ROW_BLOCK)
            rid = r0 + lax.broadcasted_iota(jnp.int32, (ROW_BLOCK, 1), 0)
            x = jnp.where(rid < rows, xs_ref[pl.ds(r0, ROW_BLOCK), :], 0.0)
            xb_ref[pl.ds(r0, ROW_BLOCK), :] = x.astype(BF16)
            return carry

        lax.fori_loop(0, n_blk, body, 0)

    @pl.when(valid & (j < n_a))
    def _():
        wgu_ref[:, 0:tf] = wg_ref[0].astype(BF16)
        wgu_ref[:, tf:2 * tf] = wu_ref[0].astype(BF16)
        b_gate = bg_ref[0]
        b_up = bu_ref[0]
        col = pl.multiple_of(j * tf, tf)

        def gate_up(r0, m):
            gu = jnp.dot(xb_ref[pl.ds(r0, m), :], wgu_ref[...], preferred_element_type=F32)
            gate = jnp.minimum(gu[:, 0:tf] + b_gate, SWIGLU_LIMIT)
            up = jnp.clip(gu[:, tf:2 * tf] + b_up, -SWIGLU_LIMIT, SWIGLU_LIMIT)
            act = (up + 1.0) * (gate * jax.nn.sigmoid(SWIGLU_ALPHA * gate))
            act_ref[pl.ds(r0, m), pl.ds(col, tf)] = act.astype(BF16)

        for_row_blocks(gate_up)

    @pl.when(valid & (j >= n_a))
    def _():
        wdb_ref[...] = wd_ref[0].astype(BF16)
        b_down = bd_ref[0]

        def down(r0, m):
            ys_ref[pl.ds(r0, m), :] = jnp.dot(act_ref[pl.ds(r0, m), :], wdb_ref[...],
                                              preferred_element_type=F32) + b_down

        for_row_blocks(down)

        def fill(rb, carry):
            r0 = pl.multiple_of(rb * ROW_BLOCK, ROW_BLOCK)
            ys_ref[pl.ds(r0, ROW_BLOCK), :] = jnp.zeros((ROW_BLOCK, ys_ref.shape[1]), F32)
            return carry

        lax.fori_loop(n_blk, SLOT_ROWS // ROW_BLOCK, fill, 0)

    @pl.when(jnp.logical_not(valid))
    def _():
        ys_ref[...] = jnp.zeros_like(ys_ref)


def _experts(xs, slot_expert, slot_rows, n_used, w_gate_up, b_gate_up, w_down, b_down):
    n_e, d, two_ff = w_gate_up.shape
    d_ff = two_ff // 2
    tf, tn = FF_TILE, DN_TILE
    n_a, n_b = d_ff // tf, d // tn
    max_slots = slot_expert.shape[0]

    def slot_c(s, nu):
        return jnp.minimum(s, nu[0] - 1)

    def j_a(s, j, nu):
        return jnp.where(s < nu[0], jnp.minimum(j, n_a - 1), n_a - 1)

    def j_b(s, j, nu):
        return jnp.where(s < nu[0], jnp.clip(j - n_a, 0, n_b - 1), n_b - 1)

    grid_spec = pltpu.PrefetchScalarGridSpec(
        num_scalar_prefetch=3,
        grid=(max_slots, n_a + n_b),
        in_specs=[
            pl.BlockSpec((SLOT_ROWS, d), lambda s, j, se, sr, nu: (slot_c(s, nu), 0)),
            pl.BlockSpec((1, d, tf), lambda s, j, se, sr, nu: (se[slot_c(s, nu)], 0, j_a(s, j, nu))),
            pl.BlockSpec((1, d, tf), lambda s, j, se, sr, nu: (se[slot_c(s, nu)], 0, n_a + j_a(s, j, nu))),
            pl.BlockSpec((1, 1, tf), lambda s, j, se, sr, nu: (se[slot_c(s, nu)], 0, j_a(s, j, nu))),
            pl.BlockSpec((1, 1, tf), lambda s, j, se, sr, nu: (se[slot_c(s, nu)], 0, n_a + j_a(s, j, nu))),
            pl.BlockSpec((1, d_ff, tn), lambda s, j, se, sr, nu: (se[slot_c(s, nu)], 0, j_b(s, j, nu))),
            pl.BlockSpec((1, 1, tn), lambda s, j, se, sr, nu: (se[slot_c(s, nu)], 0, j_b(s, j, nu))),
        ],
        out_specs=pl.BlockSpec(
            (SLOT_ROWS, tn),
            lambda s, j, se, sr, nu: (jnp.where(s < nu[0], s, max_slots), j_b(s, j, nu))),
        scratch_shapes=[pltpu.VMEM((SLOT_ROWS, d), BF16),
                        pltpu.VMEM((d, 2 * tf), BF16),
                        pltpu.VMEM((d_ff, tn), BF16),
                        pltpu.VMEM((SLOT_ROWS, d_ff), BF16)],
    )
    return pl.pallas_call(
        functools.partial(_expert_kernel, n_a=n_a),
        grid_spec=grid_spec,
        out_shape=jax.ShapeDtypeStruct(((max_slots + 1) * SLOT_ROWS, d), F32),
        compiler_params=_tc_params(("arbitrary", "arbitrary")),
        name="experts",
    )(slot_expert, slot_rows, n_used, xs, w_gate_up, w_gate_up,
      b_gate_up.reshape(n_e, 1, two_ff), b_gate_up.reshape(n_e, 1, two_ff),
      w_down, b_down.reshape(n_e, 1, d))


def _slot_tables(counts, n_tokens):
    max_slots = _max_slots(n_tokens)
    n_slots = (counts + SLOT_ROWS - 1) // SLOT_ROWS
    slot_end = jnp.cumsum(n_slots)
    slot_start = slot_end - n_slots
    n_used = slot_end[-1]
    sid = jnp.arange(max_slots, dtype=jnp.int32)
    expert = jnp.minimum(jnp.sum(sid[:, None] >= slot_end[None, :], axis=1), N_EXPERTS - 1).astype(jnp.int32)
    local = sid - slot_start[expert]
    rows = jnp.clip(counts[expert] - local * SLOT_ROWS, 0, SLOT_ROWS)
    rows = jnp.where(sid < n_used, rows, 0).astype(jnp.int32)
    return expert, rows, n_used.reshape(1).astype(jnp.int32)


def _final_kernel(x1_ref, y0_ref, y1_ref, y2_ref, y3_ref, w_ref, mod_ref, g_ref, o_ref):
    moe = w_ref[:, 0:1] * y0_ref[...]
    for k, y_ref in enumerate((y1_ref, y2_ref, y3_ref), start=1):
        moe = moe + w_ref[:, k:k + 1] * y_ref[...]
    x = x1_ref[...] + mod_ref[0, 5:6, :] * moe
    o_ref[...] = x * lax.rsqrt(jnp.mean(x * x, axis=-1, keepdims=True) + EPS) * g_ref[...]


def _final(x1, y4, w_tok, mod, final_g, seq):
    n, d = x1.shape
    tm = 256
    per_batch = seq // tm
    n_i = n // tm
    y_specs = [pl.BlockSpec((tm, d), functools.partial(lambda i, k: (k * n_i + i, 0), k=k))
               for k in range(TOP_K)]
    return pl.pallas_call(
        _final_kernel,
        grid=(n_i,),
        in_specs=[pl.BlockSpec((tm, d), lambda i: (i, 0))] + y_specs + [
                  pl.BlockSpec((tm, TOP_K), lambda i: (i, 0)),
                  pl.BlockSpec((1, 6, d), lambda i: (i // per_batch, 0, 0)),
                  pl.BlockSpec((1, d), lambda i: (0, 0))],
        out_specs=pl.BlockSpec((tm, d), lambda i: (i, 0)),
        out_shape=jax.ShapeDtypeStruct((n, d), F32),
        compiler_params=_tc_params(("arbitrary",)),
        name="final",
    )(x1, y4, y4, y4, y4, w_tok, mod, final_g.reshape(1, d))


def kernel(x, c, norm1_g, w_mod, b_mod, w_in, conv_w, w_out, norm2_g, w_router, b_router,
           w_gate_up, b_gate_up, w_down, b_down, final_g):
    b, seq, d = x.shape
    n = b * seq
    x2 = x.reshape(n, d)
    mod = _modulation(c, w_mod[0], b_mod[0])
    proj = _in_projection(x2, mod, norm1_g[0], w_in[0].astype(BF16), seq)
    x1, h2, logits_t = _mixer(proj, x2, mod, conv_w[0], w_out[0].astype(BF16), norm2_g[0],
                              w_router[0], b_router[0], seq)
    pos, w_top, counts = _route(logits_t)
    slot_expert, slot_rows, n_used = _slot_tables(counts[:, 0].astype(jnp.int32), n)
    xs = _dispatch_rows(h2, pos)
    ys = _experts(xs, slot_expert, slot_rows, n_used, w_gate_up[0], b_gate_up[0], w_down[0], b_down[0])
    y4 = _combine_rows(ys, pos.reshape(TOP_K * n))
    out = _final(x1, y4, w_top.T, mod, final_g, seq)
    return out.reshape(b, seq, d)
```

```python
import functools
import math

import jax
import jax.numpy as jnp
from jax import lax
from jax.experimental import pallas as pl
from jax.experimental.pallas import tpu as pltpu
from jax.experimental.pallas import tpu_sc as plsc

F32 = jnp.float32
BF16 = jnp.bfloat16

CHUNK = 64
N_HEADS = 8
HEAD_DIM = 128
D_RET = N_HEADS * HEAD_DIM
D_CONV = 1024
CONV_WIDTH = 3
N_EXPERTS = 32
TOP_K = 4
SWIGLU_LIMIT = 7.0
SWIGLU_ALPHA = 1.702
ROPE_BASE = 10000.0
EPS = 1e-6

VMEM_LIMIT_BYTES = 58 * 1024 * 1024
SC_WORKERS = 32
SC_ROWS_PER_COPY = 32

MIX_BLOCK = 256
ROW_BLOCK = 128
SLOT_ROWS = 9 * ROW_BLOCK
FF_TILE = 512
DN_TILE = 512


def _tc_params(sem):
    return pltpu.CompilerParams(dimension_semantics=sem, vmem_limit_bytes=VMEM_LIMIT_BYTES)


def _bf16_bits(x):
    b = lax.bitcast_convert_type(x, jnp.uint32)
    return (b + jnp.uint32(0x7FFF) + ((b >> 16) & jnp.uint32(1))) & jnp.uint32(0xFFFF0000)


def _pack_bf16_pair(lo, hi):
    return (_bf16_bits(lo) >> 16) | _bf16_bits(hi)


def _unpack_bf16_pair(p):
    lo = lax.bitcast_convert_type(p << 16, F32)
    hi = lax.bitcast_convert_type(p & jnp.uint32(0xFFFF0000), F32)
    return lo, hi


def _mod_kernel(c_ref, w_ref, b_ref, o_ref):
    c = c_ref[...]
    ca = (c * jax.nn.sigmoid(c)).astype(BF16)
    o_ref[...] = jnp.dot(ca, w_ref[...].astype(BF16), preferred_element_type=F32) + b_ref[...]


def _modulation(c, w_mod, b_mod):
    b, d = c.shape
    n = w_mod.shape[1]
    tn = 1024
    c8 = jnp.zeros((8, d), F32).at[:b].set(c)
    out = pl.pallas_call(
        _mod_kernel,
        grid=(n // tn,),
        in_specs=[pl.BlockSpec((8, d), lambda j: (0, 0)),
                  pl.BlockSpec((d, tn), lambda j: (0, j)),
                  pl.BlockSpec((1, tn), lambda j: (0, j))],
        out_specs=pl.BlockSpec((8, tn), lambda j: (0, j)),
        out_shape=jax.ShapeDtypeStruct((8, n), F32),
        compiler_params=_tc_params(("arbitrary",)),
        name="mod",
    )(c8, w_mod, b_mod.reshape(1, n))
    return out[:b].reshape(b, 6, d)


def _norm_mod(x, g, scale, shift):
    y = x * lax.rsqrt(jnp.mean(x * x, axis=-1, keepdims=True) + EPS) * g
    return y * (1.0 + scale) + shift


def _inproj_kernel(x_ref, mod_ref, g_ref, w_ref, o_ref, h_ref, *, rows):
    @pl.when(pl.program_id(1) == 0)
    def _():
        g = g_ref[...]
        scale = mod_ref[0, 1:2, :]
        shift = mod_ref[0, 0:1, :]

        def body(r, carry):
            r0 = pl.multiple_of(r * rows, rows)
            h = _norm_mod(x_ref[pl.ds(r0, rows), :], g, scale, shift)
            h_ref[pl.ds(r0, rows), :] = h.astype(BF16)
            return carry

        lax.fori_loop(0, x_ref.shape[0] // rows, body, 0)

    o_ref[...] = jnp.dot(h_ref[...], w_ref[...], preferred_element_type=F32).astype(o_ref.dtype)


def _in_projection(x2, mod, g1, w_in_bf, seq):
    n, d = x2.shape
    p = w_in_bf.shape[1]
    tm, tn = min(1024, seq), 1024
    per_batch = seq // tm
    return pl.pallas_call(
        functools.partial(_inproj_kernel, rows=128),
        grid=(n // tm, p // tn),
        in_specs=[pl.BlockSpec((tm, d), lambda i, j: (i, 0)),
                  pl.BlockSpec((1, 6, d), lambda i, j: (i // per_batch, 0, 0)),
                  pl.BlockSpec((1, d), lambda i, j: (0, 0)),
                  pl.BlockSpec((d, tn), lambda i, j: (0, j))],
        out_specs=pl.BlockSpec((tm, tn), lambda i, j: (i, j)),
        out_shape=jax.ShapeDtypeStruct((n, p), BF16),
        scratch_shapes=[pltpu.VMEM((tm, d), BF16)],
        compiler_params=_tc_params(("arbitrary", "arbitrary")),
        name="inproj",
    )(x2, mod, g1.reshape(1, d), w_in_bf)


def _mixer_tables(seq, blk):
    half = HEAD_DIM // 2
    freqs = ROPE_BASE ** (-jnp.arange(half, dtype=F32) / half)
    ang = jnp.arange(seq, dtype=F32)[:, None] * freqs[None, :]
    cos = jnp.concatenate([jnp.cos(ang), jnp.cos(ang)], axis=-1)
    sin = jnp.concatenate([-jnp.sin(ang), jnp.sin(ang)], axis=-1)
    log_gamma = jnp.log1p(-jnp.exp2(-5.0 - jnp.arange(N_HEADS, dtype=F32)))
    idx = jnp.arange(blk, dtype=F32)
    dist = jnp.abs(idx[:, None] - idx[None, :])
    ck = jnp.arange(blk) // CHUNK
    visible = (ck[None, :] <= ck[:, None]).astype(F32)
    dmask = jnp.exp(log_gamma[:, None, None] * dist) * visible
    ones = jnp.ones((1, 1, HEAD_DIM), F32)
    dq = jnp.exp(log_gamma[:, None] * (idx + 1.0)[None])[:, :, None] * ones
    dkv = jnp.exp(log_gamma[:, None] * (blk - 1 - idx)[None])[:, :, None] * ones
    dblk = jnp.exp(log_gamma * blk)[:, None, None] * ones
    return cos, sin, dmask, dq, dkv, dblk


def _mixer_kernel(proj_ref, x_ref, mod_ref, cos_ref, sin_ref, dmask_ref, dq_ref, dkv_ref, dblk_ref,
                  convw_ref, wout_ref, g2_ref, wr_ref, br_ref,
                  x1_ref, h2_ref, lg_ref,
                  s_ref, zbuf_ref, y_ref, *, per_batch):
    blk = x_ref.shape[0]

    @pl.when(pl.program_id(0) % per_batch == 0)
    def _():
        s_ref[...] = jnp.zeros_like(s_ref)
        zbuf_ref[0:8, :] = jnp.zeros((8, D_CONV), F32)

    cos = cos_ref[...]
    sin = sin_ref[...]
    k_scale = HEAD_DIM ** -0.5
    nt = (((1,), (1,)), ((), ()))
    tn = (((0,), (0,)), ((), ()))
    for h in range(N_HEADS):
        c0 = h * HEAD_DIM
        q = proj_ref[:, c0:c0 + HEAD_DIM].astype(F32)
        k = proj_ref[:, D_RET + c0:D_RET + c0 + HEAD_DIM].astype(F32)
        v = proj_ref[:, 2 * D_RET + c0:2 * D_RET + c0 + HEAD_DIM]
        g = proj_ref[:, 3 * D_RET + c0:3 * D_RET + c0 + HEAD_DIM].astype(F32)
        qr = q * cos + pltpu.roll(q, HEAD_DIM // 2, 1) * sin
        kr = (k * cos + pltpu.roll(k, HEAD_DIM // 2, 1) * sin) * k_scale
        qb = qr.astype(BF16)
        kb = kr.astype(BF16)
        scores = lax.dot_general(qb, kb, nt, preferred_element_type=F32) * dmask_ref[h]
        intra = jnp.dot(scores.astype(BF16), v, preferred_element_type=F32)
        state = s_ref[h]
        cross = jnp.dot(qb, state.astype(BF16), preferred_element_type=F32) * dq_ref[h]
        kd = (kr * dkv_ref[h]).astype(BF16)
        kv = lax.dot_general(kd, v, tn, preferred_element_type=F32)
        s_ref[h] = dblk_ref[h] * state + kv
        o = intra + cross
        mu = jnp.mean(o, axis=-1, keepdims=True)
        dev = o - mu
        var = jnp.mean(dev * dev, axis=-1, keepdims=True)
        on = dev * lax.rsqrt(var + EPS)
        y_ref[:, c0:c0 + HEAD_DIM] = (g * jax.nn.sigmoid(g) * on).astype(BF16)

    cw = 256
    base = 4 * D_RET
    for cb in range(D_CONV // cw):
        lo = cb * cw
        cg = proj_ref[:, base + D_CONV + lo:base + D_CONV + lo + cw].astype(F32)
        u = proj_ref[:, base + 2 * D_CONV + lo:base + 2 * D_CONV + lo + cw].astype(F32)
        zbuf_ref[8:blk + 8, lo:lo + cw] = cg * u
    for cb in range(D_CONV // cw):
        lo = cb * cw
        z0 = zbuf_ref[8:blk + 8, lo:lo + cw]
        z1 = zbuf_ref[7:blk + 7, lo:lo + cw]
        z2 = zbuf_ref[6:blk + 6, lo:lo + cw]
        z = (convw_ref[2:3, lo:lo + cw] * z0 + convw_ref[1:2, lo:lo + cw] * z1
             + convw_ref[0:1, lo:lo + cw] * z2)
        bg = proj_ref[:, base + lo:base + lo + cw].astype(F32)
        y_ref[:, D_RET + lo:D_RET + lo + cw] = (bg * z).astype(BF16)
    zbuf_ref[0:8, :] = zbuf_ref[blk:blk + 8, :]

    mix = jnp.dot(y_ref[...], wout_ref[...], preferred_element_type=F32)
    x1 = x_ref[...] + mod_ref[0, 2:3, :] * mix
    x1_ref[...] = x1
    h2 = _norm_mod(x1, g2_ref[...], mod_ref[0, 4:5, :], mod_ref[0, 3:4, :])
    half = h2.shape[1] // 2
    h2_ref[...] = _pack_bf16_pair(h2[:, :half], h2[:, half:])
    lg_ref[...] = lax.dot_general(wr_ref[...], h2, nt, precision=lax.Precision.HIGHEST,
                                  preferred_element_type=F32) + br_ref[...]


def _mixer(proj, x2, mod, conv_w, w_out_bf, g2, w_router, b_router, seq):
    n, d = x2.shape
    blk = MIX_BLOCK
    per_batch = seq // blk
    cos, sin, dmask, dq, dkv, dblk = _mixer_tables(seq, blk)
    const2 = lambda i: (0, 0)
    const3 = lambda i: (0, 0, 0)
    return pl.pallas_call(
        functools.partial(_mixer_kernel, per_batch=per_batch),
        grid=(n // blk,),
        in_specs=[pl.BlockSpec((blk, proj.shape[1]), lambda i: (i, 0)),
                  pl.BlockSpec((blk, d), lambda i: (i, 0)),
                  pl.BlockSpec((1, 6, d), lambda i: (i // per_batch, 0, 0)),
                  pl.BlockSpec((blk, HEAD_DIM), lambda i: (i % per_batch, 0)),
                  pl.BlockSpec((blk, HEAD_DIM), lambda i: (i % per_batch, 0)),
                  pl.BlockSpec((N_HEADS, blk, blk), const3),
                  pl.BlockSpec((N_HEADS, blk, HEAD_DIM), const3),
                  pl.BlockSpec((N_HEADS, blk, HEAD_DIM), const3),
                  pl.BlockSpec((N_HEADS, 1, HEAD_DIM), const3),
                  pl.BlockSpec((CONV_WIDTH, D_CONV), const2),
                  pl.BlockSpec((D_RET + D_CONV, d), const2),
                  pl.BlockSpec((1, d), const2),
                  pl.BlockSpec((N_EXPERTS, d), const2),
                  pl.BlockSpec((N_EXPERTS, 1), const2)],
        out_specs=[pl.BlockSpec((blk, d), lambda i: (i, 0)),
                   pl.BlockSpec((blk, d // 2), lambda i: (i, 0)),
                   pl.BlockSpec((N_EXPERTS, blk), lambda i: (0, i))],
        out_shape=[jax.ShapeDtypeStruct((n, d), F32),
                   jax.ShapeDtypeStruct((n, d // 2), jnp.uint32),
                   jax.ShapeDtypeStruct((N_EXPERTS, n), F32)],
        scratch_shapes=[pltpu.VMEM((N_HEADS, HEAD_DIM, HEAD_DIM), F32),
                        pltpu.VMEM((blk + 8, D_CONV), F32),
                        pltpu.VMEM((blk, D_RET + D_CONV), BF16)],
        compiler_params=_tc_params(("arbitrary",)),
        name="mixer",
    )(proj, x2, mod, cos, sin, dmask, dq, dkv, dblk, conv_w, w_out_bf, g2.reshape(1, d),
      w_router.T, b_router.reshape(N_EXPERTS, 1))


def _top_k_block(lg_ref, tri_ref, run_ref):
    n_e, bw = lg_ref.shape

    @pl.when(pl.program_id(0) == 0)
    def _():
        run_ref[...] = jnp.zeros_like(run_ref)

    l = lg_ref[...]
    eio = lax.broadcasted_iota(jnp.int32, (n_e, bw), 0)
    vals, hots = [], []
    for _ in range(TOP_K):
        m = jnp.max(l, axis=0, keepdims=True)
        idx = jnp.min(jnp.where(l == m, eio, n_e), axis=0, keepdims=True)
        hot = eio == idx
        vals.append(m)
        hots.append(hot)
        l = jnp.where(hot, -jnp.inf, l)
    sel = sum(jnp.where(hot, 1.0, 0.0) for hot in hots)
    incl = jnp.dot(sel.astype(BF16), tri_ref[...], preferred_element_type=F32)
    run = run_ref[:, 0:1]
    rank = incl - sel + run
    run_ref[...] = jnp.broadcast_to(run + incl[:, bw - 1:bw], run_ref.shape)
    return vals, hots, rank


def _count_kernel(lg_ref, tri_ref, cnt_ref, run_ref):
    _top_k_block(lg_ref, tri_ref, run_ref)
    cnt_ref[...] = run_ref[...]


def _route_kernel(lg_ref, tri_ref, low_ref, tot_ref, pos_ref, w_ref, run_ref, *, max_slots):
    vals, hots, rank = _top_k_block(lg_ref, tri_ref, run_ref)
    ex = [jnp.exp(v - vals[0]) for v in vals]
    den = ex[0] + ex[1] + ex[2] + ex[3]
    for k in range(TOP_K):
        w_ref[k:k + 1, :] = ex[k] / den
    tot = tot_ref[...]
    n_slots = sum(jnp.where(tot > float(m * SLOT_ROWS), 1.0, 0.0) for m in range(max_slots))
    start = jnp.dot(low_ref[...], n_slots.astype(BF16), preferred_element_type=F32)
    dest = start[:, 0:1] * float(SLOT_ROWS) + rank
    for k in range(TOP_K):
        pos_ref[k:k + 1, :] = jnp.sum(jnp.where(hots[k], dest, 0.0), axis=0,
                                      keepdims=True).astype(jnp.int32)


def _route(logits_t):
    n_e, n = logits_t.shape
    bw = 512
    max_slots = -(-n // SLOT_ROWS)
    tri = (jnp.arange(bw)[:, None] <= jnp.arange(bw)[None, :]).astype(BF16)
    low = (jnp.arange(n_e)[None, :] < jnp.arange(n_e)[:, None]).astype(BF16)
    lg_spec = pl.BlockSpec((n_e, bw), lambda c: (0, c))
    tri_spec = pl.BlockSpec((bw, bw), lambda c: (0, 0))
    cnt_spec = pl.BlockSpec((n_e, 128), lambda c: (0, 0))
    run = pltpu.VMEM((n_e, 128), F32)
    counts = pl.pallas_call(
        _count_kernel,
        grid=(n // bw,),
        in_specs=[lg_spec, tri_spec],
        out_specs=cnt_spec,
        out_shape=jax.ShapeDtypeStruct((n_e, 128), F32),
        scratch_shapes=[run],
        compiler_params=_tc_params(("arbitrary",)),
        name="route_count",
    )(logits_t, tri)
    pos, w_top = pl.pallas_call(
        functools.partial(_route_kernel, max_slots=max_slots),
        grid=(n // bw,),
        in_specs=[lg_spec, tri_spec, pl.BlockSpec((n_e, n_e), lambda c: (0, 0)), cnt_spec],
        out_specs=[pl.BlockSpec((TOP_K, bw), lambda c: (0, c)),
                   pl.BlockSpec((TOP_K, bw), lambda c: (0, c))],
        out_shape=[jax.ShapeDtypeStruct((TOP_K, n), jnp.int32),
                   jax.ShapeDtypeStruct((TOP_K, n), F32)],
        scratch_shapes=[run],
        compiler_params=_tc_params(("arbitrary",)),
        name="route",
    )(logits_t, tri, low, counts)
    return pos, w_top, counts


def _sc_mesh():
    return plsc.VectorSubcoreMesh(core_axis_name="c", subcore_axis_name="s")


def _sc_worker_id():
    return lax.axis_index("s") * lax.axis_size("c") + lax.axis_index("c")


def _dispatch_rows(h2, pos):
    n, d = h2.shape
    ch = SC_ROWS_PER_COPY
    per_w = n // SC_WORKERS
    n_ch = per_w // ch
    total_rows = _max_slots(n) * SLOT_ROWS
    idx = pos.reshape(TOP_K, SC_WORKERS, n_ch, ch).transpose(1, 2, 0, 3).reshape(
        SC_WORKERS, n_ch * TOP_K, ch)

    @functools.partial(
        pl.kernel, mesh=_sc_mesh(),
        out_type=jax.ShapeDtypeStruct((total_rows, d), h2.dtype),
        scratch_types=[pltpu.VMEM((n_ch * TOP_K, ch), jnp.int32),
                       pltpu.VMEM((ch, d), h2.dtype),
                       pltpu.SemaphoreType.DMA])
    def scatter(x_hbm, idx_hbm, o_hbm, idx_v, rows_v, sem):
        wid = _sc_worker_id()
        pltpu.sync_copy(idx_hbm.at[wid], idx_v)

        @pl.loop(0, n_ch)
        def _(c):
            r0 = pl.multiple_of(wid * per_w + c * ch, ch)
            pltpu.sync_copy(x_hbm.at[pl.ds(r0, ch)], rows_v)
            for k in range(TOP_K):
                pltpu.async_copy(rows_v, o_hbm.at[idx_v.at[c * TOP_K + k]], sem).wait()

    return scatter(h2, idx)


def _combine_rows(ys, pos_flat):
    n = pos_flat.shape[0]
    d = ys.shape[1]
    ch = SC_ROWS_PER_COPY
    per_w = n // SC_WORKERS
    n_ch = per_w // ch
    idx = pos_flat.reshape(SC_WORKERS, n_ch, ch)

    @functools.partial(
        pl.kernel, mesh=_sc_mesh(),
        out_type=jax.ShapeDtypeStruct((n, d), ys.dtype),
        scratch_types=[pltpu.VMEM((n_ch, ch), jnp.int32),
                       pltpu.VMEM((ch, d), ys.dtype),
                       pltpu.SemaphoreType.DMA])
    def gather(y_hbm, idx_hbm, o_hbm, idx_v, rows_v, sem):
        wid = _sc_worker_id()
        pltpu.sync_copy(idx_hbm.at[wid], idx_v)

        @pl.loop(0, n_ch)
        def _(c):
            r0 = pl.multiple_of(wid * per_w + c * ch, ch)
            pltpu.async_copy(y_hbm.at[idx_v.at[c]], rows_v, sem).wait()
            pltpu.sync_copy(rows_v, o_hbm.at[pl.ds(r0, ch)])

    return gather(ys, idx)


def _max_slots(n_tokens):
    return N_EXPERTS + (n_tokens * TOP_K) // SLOT_ROWS


def _expert_kernel(se_ref, sr_ref,
                   xs_ref, wg_ref, wu_ref, bg_ref, bu_ref, wdl_ref, wdh_ref, bdl_ref, bdh_ref,
                   ys_ref,
                   xb_ref, wgu_ref, wdb_ref, act_ref, *, n_a):
    s = pl.program_id(0)
    j = pl.program_id(1)
    rows = sr_ref[s]
    n_blk = (rows + ROW_BLOCK - 1) // ROW_BLOCK
    tf = wg_ref.shape[2]

    def for_row_blocks(fn):
        n_quads = n_blk // 4

        def body(rb, carry):
            fn(pl.multiple_of(rb * 4 * ROW_BLOCK, 4 * ROW_BLOCK), 4 * ROW_BLOCK)
            return carry

        lax.fori_loop(0, n_quads, body, 0)

        @pl.when((n_blk // 2) % 2 == 1)
        def _():
            fn(pl.multiple_of(n_quads * 4 * ROW_BLOCK, 4 * ROW_BLOCK), 2 * ROW_BLOCK)

        @pl.when(n_blk % 2 == 1)
        def _():
            fn(pl.multiple_of((n_blk // 2) * 2 * ROW_BLOCK, 2 * ROW_BLOCK), ROW_BLOCK)

    @pl.when(j == 0)
    def _():
        half = xs_ref.shape[1]

        def body(rb, carry):
            r0 = pl.multiple_of(rb * ROW_BLOCK, ROW_BLOCK)
            keep = (r0 + lax.broadcasted_iota(jnp.int32, (ROW_BLOCK, 1), 0)) < rows
            lo, hi = _unpack_bf16_pair(xs_ref[pl.ds(r0, ROW_BLOCK), :])
            xb_ref[pl.ds(r0, ROW_BLOCK), 0:half] = jnp.where(keep, lo, 0.0).astype(BF16)
            xb_ref[pl.ds(r0, ROW_BLOCK), half:2 * half] = jnp.where(keep, hi, 0.0).astype(BF16)
            return carry

        lax.fori_loop(0, n_blk, body, 0)

    @pl.when(j < n_a)
    def _():
        wgu_ref[:, 0:tf] = wg_ref[0].astype(BF16)
        wgu_ref[:, tf:2 * tf] = wu_ref[0].astype(BF16)
        b_gate = bg_ref[0]
        b_up = bu_ref[0]
        col = pl.multiple_of(j * tf, tf)

        def gate_up(r0, m):
            gu = jnp.dot(xb_ref[pl.ds(r0, m), :], wgu_ref[...], preferred_element_type=F32)
            gate = jnp.minimum(gu[:, 0:tf] + b_gate, SWIGLU_LIMIT)
            up = jnp.clip(gu[:, tf:2 * tf] + b_up, -SWIGLU_LIMIT, SWIGLU_LIMIT)
            act = (up + 1.0) * (gate * jax.nn.sigmoid(SWIGLU_ALPHA * gate))
            act_ref[pl.ds(r0, m), pl.ds(col, tf)] = act.astype(BF16)

        for_row_blocks(gate_up)

    @pl.when(j >= n_a)
    def _():
        tq = wdl_ref.shape[2]
        wdb_ref[:, 0:tq] = wdl_ref[0].astype(BF16)
        wdb_ref[:, tq:2 * tq] = wdh_ref[0].astype(BF16)
        b_lo = bdl_ref[0]
        b_hi = bdh_ref[0]

        def down(r0, m):
            y = jnp.dot(act_ref[pl.ds(r0, m), :], wdb_ref[...], preferred_element_type=F32)
            ys_ref[pl.ds(r0, m), :] = _pack_bf16_pair(y[:, 0:tq] + b_lo, y[:, tq:2 * tq] + b_hi)

        for_row_blocks(down)

        def fill(rb, carry):
            r0 = pl.multiple_of(rb * ROW_BLOCK, ROW_BLOCK)
            ys_ref[pl.ds(r0, ROW_BLOCK), :] = jnp.zeros((ROW_BLOCK, ys_ref.shape[1]), jnp.uint32)
            return carry

        lax.fori_loop(n_blk, SLOT_ROWS // ROW_BLOCK, fill, 0)


def _experts(xs, slot_expert, slot_rows, n_used, w_gate_up, b_gate_up, w_down, b_down):
    n_e, d, two_ff = w_gate_up.shape
    d_ff = two_ff // 2
    tf, tn = FF_TILE, DN_TILE
    n_a, n_b = d_ff // tf, d // tn
    tq = tn // 2
    total_rows = xs.shape[0]

    def j_a(j):
        return jnp.minimum(j, n_a - 1)

    def j_b(j):
        return jnp.clip(j - n_a, 0, n_b - 1)

    grid_spec = pltpu.PrefetchScalarGridSpec(
        num_scalar_prefetch=2,
        grid=(n_used, n_a + n_b),
        in_specs=[
            pl.BlockSpec((SLOT_ROWS, d // 2), lambda s, j, se, sr: (s, 0)),
            pl.BlockSpec((1, d, tf), lambda s, j, se, sr: (se[s], 0, j_a(j))),
            pl.BlockSpec((1, d, tf), lambda s, j, se, sr: (se[s], 0, n_a + j_a(j))),
            pl.BlockSpec((1, 1, tf), lambda s, j, se, sr: (se[s], 0, j_a(j))),
            pl.BlockSpec((1, 1, tf), lambda s, j, se, sr: (se[s], 0, n_a + j_a(j))),
            pl.BlockSpec((1, d_ff, tq), lambda s, j, se, sr: (se[s], 0, j_b(j))),
            pl.BlockSpec((1, d_ff, tq), lambda s, j, se, sr: (se[s], 0, n_b + j_b(j))),
            pl.BlockSpec((1, 1, tq), lambda s, j, se, sr: (se[s], 0, j_b(j))),
            pl.BlockSpec((1, 1, tq), lambda s, j, se, sr: (se[s], 0, n_b + j_b(j))),
        ],
        out_specs=pl.BlockSpec((SLOT_ROWS, tq), lambda s, j, se, sr: (s, j_b(j))),
        scratch_shapes=[pltpu.VMEM((SLOT_ROWS, d), BF16),
                        pltpu.VMEM((d, 2 * tf), BF16),
                        pltpu.VMEM((d_ff, tn), BF16),
                        pltpu.VMEM((SLOT_ROWS, d_ff), BF16)],
    )
    b_gu = b_gate_up.reshape(n_e, 1, two_ff)
    b_dn = b_down.reshape(n_e, 1, d)
    return pl.pallas_call(
        functools.partial(_expert_kernel, n_a=n_a),
        grid_spec=grid_spec,
        out_shape=jax.ShapeDtypeStruct((total_rows, d // 2), jnp.uint32),
        compiler_params=_tc_params(("arbitrary", "arbitrary")),
        name="experts",
    )(slot_expert, slot_rows, xs, w_gate_up, w_gate_up, b_gu, b_gu, w_down, w_down, b_dn, b_dn)


def _slot_tables(counts, n_tokens):
    max_slots = _max_slots(n_tokens)
    n_slots = (counts + SLOT_ROWS - 1) // SLOT_ROWS
    slot_end = jnp.cumsum(n_slots)
    slot_start = slot_end - n_slots
    n_used = slot_end[-1]
    sid = jnp.arange(max_slots, dtype=jnp.int32)
    expert = jnp.minimum(jnp.sum(sid[:, None] >= slot_end[None, :], axis=1), N_EXPERTS - 1).astype(jnp.int32)
    local = sid - slot_start[expert]
    rows = jnp.clip(counts[expert] - local * SLOT_ROWS, 0, SLOT_ROWS)
    rows = jnp.where(sid < n_used, rows, 0).astype(jnp.int32)
    return expert, rows, n_used.reshape(1).astype(jnp.int32)


def _final_kernel(x1_ref, y0_ref, y1_ref, y2_ref, y3_ref, w_ref, mod_ref, g_ref, o_ref):
    half = y0_ref.shape[1]
    d = 2 * half
    moe_lo = moe_hi = None
    for k, y_ref in enumerate((y0_ref, y1_ref, y2_ref, y3_ref)):
        lo, hi = _unpack_bf16_pair(y_ref[...])
        wk = w_ref[:, k:k + 1]
        moe_lo = wk * lo if k == 0 else moe_lo + wk * lo
        moe_hi = wk * hi if k == 0 else moe_hi + wk * hi
    x_lo = x1_ref[:, 0:half] + mod_ref[0, 5:6, 0:half] * moe_lo
    x_hi = x1_ref[:, half:d] + mod_ref[0, 5:6, half:d] * moe_hi
    ss = jnp.sum(x_lo * x_lo, axis=-1, keepdims=True) + jnp.sum(x_hi * x_hi, axis=-1, keepdims=True)
    r = lax.rsqrt(ss / d + EPS)
    o_ref[:, 0:half] = x_lo * r * g_ref[:, 0:half]
    o_ref[:, half:d] = x_hi * r * g_ref[:, half:d]


def _final(x1, y4, w_tok, mod, final_g, seq):
    n, d = x1.shape
    tm = 256
    per_batch = seq // tm
    n_i = n // tm
    y_specs = [pl.BlockSpec((tm, d // 2), functools.partial(lambda i, k: (k * n_i + i, 0), k=k))
               for k in range(TOP_K)]
    return pl.pallas_call(
        _final_kernel,
        grid=(n_i,),
        in_specs=[pl.BlockSpec((tm, d), lambda i: (i, 0))] + y_specs + [
                  pl.BlockSpec((tm, TOP_K), lambda i: (i, 0)),
                  pl.BlockSpec((1, 6, d), lambda i: (i // per_batch, 0, 0)),
                  pl.BlockSpec((1, d), lambda i: (0, 0))],
        out_specs=pl.BlockSpec((tm, d), lambda i: (i, 0)),
        out_shape=jax.ShapeDtypeStruct((n, d), F32),
        compiler_params=_tc_params(("arbitrary",)),
        name="final",
    )(x1, y4, y4, y4, y4, w_tok, mod, final_g.reshape(1, d))


def kernel(x, c, norm1_g, w_mod, b_mod, w_in, conv_w, w_out, norm2_g, w_router, b_router,
           w_gate_up, b_gate_up, w_down, b_down, final_g):
    b, seq, d = x.shape
    n = b * seq
    x2 = x.reshape(n, d)
    mod = _modulation(c, w_mod[0], b_mod[0])
    proj = _in_projection(x2, mod, norm1_g[0], w_in[0].astype(BF16), seq)
    x1, h2, logits_t = _mixer(proj, x2, mod, conv_w[0], w_out[0].astype(BF16), norm2_g[0],
                              w_router[0], b_router[0], seq)
    pos, w_top, counts = _route(logits_t)
    slot_expert, slot_rows, n_used = _slot_tables(counts[:, 0].astype(jnp.int32), n)
    xs = _dispatch_rows(h2, pos)
    ys = _experts(xs, slot_expert, slot_rows, n_used[0], w_gate_up[0], b_gate_up[0], w_down[0], b_down[0])
    y4 = _combine_rows(ys, pos.reshape(TOP_K * n))
    out = _final(x1, y4, w_top.T, mod, final_g, seq)
    return out.reshape(b, seq, d)
```

```python
import functools
import math

import jax
import jax.numpy as jnp
from jax import lax
from jax.experimental import pallas as pl
from jax.experimental.pallas import tpu as pltpu
from jax.experimental.pallas import tpu_sc as plsc

F32 = jnp.float32
BF16 = jnp.bfloat16

CHUNK = 64
N_HEADS = 8
HEAD_DIM = 128
D_RET = N_HEADS * HEAD_DIM
D_CONV = 1024
CONV_WIDTH = 3
N_EXPERTS = 32
TOP_K = 4
SWIGLU_LIMIT = 7.0
SWIGLU_ALPHA = 1.702
ROPE_BASE = 10000.0
EPS = 1e-6

VMEM_LIMIT_BYTES = 58 * 1024 * 1024
SC_WORKERS = 32
SC_ROWS_PER_COPY = 32

MIX_BLOCK = 256
ROW_BLOCK = 128
SLOT_ROWS = 9 * ROW_BLOCK
W_PANEL = 512


def _tc_params(sem):
    return pltpu.CompilerParams(dimension_semantics=sem, vmem_limit_bytes=VMEM_LIMIT_BYTES)


def _bf16_bits(x):
    b = lax.bitcast_convert_type(x, jnp.uint32)
    return (b + jnp.uint32(0x7FFF) + ((b >> 16) & jnp.uint32(1))) & jnp.uint32(0xFFFF0000)


def _pack_bf16_pair(lo, hi):
    return (_bf16_bits(lo) >> 16) | _bf16_bits(hi)


def _unpack_bf16_pair(p):
    lo = lax.bitcast_convert_type(p << 16, F32)
    hi = lax.bitcast_convert_type(p & jnp.uint32(0xFFFF0000), F32)
    return lo, hi


def _mod_kernel(c_ref, w_ref, b_ref, o_ref):
    c = c_ref[...]
    ca = (c * jax.nn.sigmoid(c)).astype(BF16)
    o_ref[...] = jnp.dot(ca, w_ref[...].astype(BF16), preferred_element_type=F32) + b_ref[...]


def _modulation(c, w_mod, b_mod):
    b, d = c.shape
    n = w_mod.shape[1]
    tn = 1024
    c8 = jnp.zeros((8, d), F32).at[:b].set(c)
    out = pl.pallas_call(
        _mod_kernel,
        grid=(n // tn,),
        in_specs=[pl.BlockSpec((8, d), lambda j: (0, 0)),
                  pl.BlockSpec((d, tn), lambda j: (0, j)),
                  pl.BlockSpec((1, tn), lambda j: (0, j))],
        out_specs=pl.BlockSpec((8, tn), lambda j: (0, j)),
        out_shape=jax.ShapeDtypeStruct((8, n), F32),
        compiler_params=_tc_params(("arbitrary",)),
        name="mod",
    )(c8, w_mod, b_mod.reshape(1, n))
    return out[:b].reshape(b, 6, d)


def _norm_mod(x, g, scale, shift):
    y = x * lax.rsqrt(jnp.mean(x * x, axis=-1, keepdims=True) + EPS) * g
    return y * (1.0 + scale) + shift


def _inproj_kernel(x_ref, mod_ref, g_ref, w_ref, o_ref, h_ref, *, rows):
    @pl.when(pl.program_id(1) == 0)
    def _():
        g = g_ref[...]
        scale = mod_ref[0, 1:2, :]
        shift = mod_ref[0, 0:1, :]

        def body(r, carry):
            r0 = pl.multiple_of(r * rows, rows)
            h = _norm_mod(x_ref[pl.ds(r0, rows), :], g, scale, shift)
            h_ref[pl.ds(r0, rows), :] = h.astype(BF16)
            return carry

        lax.fori_loop(0, x_ref.shape[0] // rows, body, 0)

    o_ref[...] = jnp.dot(h_ref[...], w_ref[...], preferred_element_type=F32).astype(o_ref.dtype)


def _in_projection(x2, mod, g1, w_in_bf, seq):
    n, d = x2.shape
    p = w_in_bf.shape[1]
    tm, tn = min(1024, seq), 1024
    per_batch = seq // tm
    return pl.pallas_call(
        functools.partial(_inproj_kernel, rows=128),
        grid=(n // tm, p // tn),
        in_specs=[pl.BlockSpec((tm, d), lambda i, j: (i, 0)),
                  pl.BlockSpec((1, 6, d), lambda i, j: (i // per_batch, 0, 0)),
                  pl.BlockSpec((1, d), lambda i, j: (0, 0)),
                  pl.BlockSpec((d, tn), lambda i, j: (0, j))],
        out_specs=pl.BlockSpec((tm, tn), lambda i, j: (i, j)),
        out_shape=jax.ShapeDtypeStruct((n, p), BF16),
        scratch_shapes=[pltpu.VMEM((tm, d), BF16)],
        compiler_params=_tc_params(("arbitrary", "arbitrary")),
        name="inproj",
    )(x2, mod, g1.reshape(1, d), w_in_bf)


def _mixer_tables(seq, blk):
    half = HEAD_DIM // 2
    freqs = ROPE_BASE ** (-jnp.arange(half, dtype=F32) / half)
    ang = jnp.arange(seq, dtype=F32)[:, None] * freqs[None, :]
    cos = jnp.concatenate([jnp.cos(ang), jnp.cos(ang)], axis=-1)
    sin = jnp.concatenate([-jnp.sin(ang), jnp.sin(ang)], axis=-1)
    log_gamma = jnp.log1p(-jnp.exp2(-5.0 - jnp.arange(N_HEADS, dtype=F32)))
    idx = jnp.arange(blk, dtype=F32)
    dist = jnp.abs(idx[:, None] - idx[None, :])
    ck = jnp.arange(blk) // CHUNK
    visible = (ck[None, :] <= ck[:, None]).astype(F32)
    dmask = jnp.exp(log_gamma[:, None, None] * dist) * visible
    ones = jnp.ones((1, 1, HEAD_DIM), F32)
    dq = jnp.exp(log_gamma[:, None] * (idx + 1.0)[None])[:, :, None] * ones
    dkv = jnp.exp(log_gamma[:, None] * (blk - 1 - idx)[None])[:, :, None] * ones
    dblk = jnp.exp(log_gamma * blk)[:, None, None] * ones
    return cos, sin, dmask, dq, dkv, dblk


def _mixer_kernel(proj_ref, x_ref, mod_ref, cos_ref, sin_ref, dmask_ref, dq_ref, dkv_ref, dblk_ref,
                  convw_ref, wout_ref, g2_ref, wr_ref, br_ref,
                  x1_ref, h2_ref, lg_ref,
                  s_ref, zbuf_ref, y_ref, *, per_batch):
    blk = x_ref.shape[0]

    @pl.when(pl.program_id(0) % per_batch == 0)
    def _():
        s_ref[...] = jnp.zeros_like(s_ref)
        zbuf_ref[0:8, :] = jnp.zeros((8, D_CONV), F32)

    cos = cos_ref[...]
    sin = sin_ref[...]
    k_scale = HEAD_DIM ** -0.5
    nt = (((1,), (1,)), ((), ()))
    tn = (((0,), (0,)), ((), ()))
    for h in range(N_HEADS):
        c0 = h * HEAD_DIM
        q = proj_ref[:, c0:c0 + HEAD_DIM].astype(F32)
        k = proj_ref[:, D_RET + c0:D_RET + c0 + HEAD_DIM].astype(F32)
        v = proj_ref[:, 2 * D_RET + c0:2 * D_RET + c0 + HEAD_DIM]
        g = proj_ref[:, 3 * D_RET + c0:3 * D_RET + c0 + HEAD_DIM].astype(F32)
        qr = q * cos + pltpu.roll(q, HEAD_DIM // 2, 1) * sin
        kr = (k * cos + pltpu.roll(k, HEAD_DIM // 2, 1) * sin) * k_scale
        qb = qr.astype(BF16)
        kb = kr.astype(BF16)
        scores = lax.dot_general(qb, kb, nt, preferred_element_type=F32) * dmask_ref[h]
        intra = jnp.dot(scores.astype(BF16), v, preferred_element_type=F32)
        state = s_ref[h]
        cross = jnp.dot(qb, state.astype(BF16), preferred_element_type=F32) * dq_ref[h]
        kd = (kr * dkv_ref[h]).astype(BF16)
        kv = lax.dot_general(kd, v, tn, preferred_element_type=F32)
        s_ref[h] = dblk_ref[h] * state + kv
        o = intra + cross
        mu = jnp.mean(o, axis=-1, keepdims=True)
        dev = o - mu
        var = jnp.mean(dev * dev, axis=-1, keepdims=True)
        on = dev * lax.rsqrt(var + EPS)
        y_ref[:, c0:c0 + HEAD_DIM] = (g * jax.nn.sigmoid(g) * on).astype(BF16)

    cw = 256
    base = 4 * D_RET
    for cb in range(D_CONV // cw):
        lo = cb * cw
        cg = proj_ref[:, base + D_CONV + lo:base + D_CONV + lo + cw].astype(F32)
        u = proj_ref[:, base + 2 * D_CONV + lo:base + 2 * D_CONV + lo + cw].astype(F32)
        zbuf_ref[8:blk + 8, lo:lo + cw] = cg * u
    for cb in range(D_CONV // cw):
        lo = cb * cw
        z0 = zbuf_ref[8:blk + 8, lo:lo + cw]
        z1 = zbuf_ref[7:blk + 7, lo:lo + cw]
        z2 = zbuf_ref[6:blk + 6, lo:lo + cw]
        z = (convw_ref[2:3, lo:lo + cw] * z0 + convw_ref[1:2, lo:lo + cw] * z1
             + convw_ref[0:1, lo:lo + cw] * z2)
        bg = proj_ref[:, base + lo:base + lo + cw].astype(F32)
        y_ref[:, D_RET + lo:D_RET + lo + cw] = (bg * z).astype(BF16)
    zbuf_ref[0:8, :] = zbuf_ref[blk:blk + 8, :]

    mix = jnp.dot(y_ref[...], wout_ref[...], preferred_element_type=F32)
    x1 = x_ref[...] + mod_ref[0, 2:3, :] * mix
    x1_ref[...] = x1
    h2 = _norm_mod(x1, g2_ref[...], mod_ref[0, 4:5, :], mod_ref[0, 3:4, :])
    half = h2.shape[1] // 2
    h2_ref[...] = _pack_bf16_pair(h2[:, :half], h2[:, half:])
    lg_ref[...] = lax.dot_general(wr_ref[...], h2, nt, precision=lax.Precision.HIGHEST,
                                  preferred_element_type=F32) + br_ref[...]


def _mixer(proj, x2, mod, conv_w, w_out_bf, g2, w_router, b_router, seq):
    n, d = x2.shape
    blk = MIX_BLOCK
    per_batch = seq // blk
    cos, sin, dmask, dq, dkv, dblk = _mixer_tables(seq, blk)
    const2 = lambda i: (0, 0)
    const3 = lambda i: (0, 0, 0)
    return pl.pallas_call(
        functools.partial(_mixer_kernel, per_batch=per_batch),
        grid=(n // blk,),
        in_specs=[pl.BlockSpec((blk, proj.shape[1]), lambda i: (i, 0)),
                  pl.BlockSpec((blk, d), lambda i: (i, 0)),
                  pl.BlockSpec((1, 6, d), lambda i: (i // per_batch, 0, 0)),
                  pl.BlockSpec((blk, HEAD_DIM), lambda i: (i % per_batch, 0)),
                  pl.BlockSpec((blk, HEAD_DIM), lambda i: (i % per_batch, 0)),
                  pl.BlockSpec((N_HEADS, blk, blk), const3),
                  pl.BlockSpec((N_HEADS, blk, HEAD_DIM), const3),
                  pl.BlockSpec((N_HEADS, blk, HEAD_DIM), const3),
                  pl.BlockSpec((N_HEADS, 1, HEAD_DIM), const3),
                  pl.BlockSpec((CONV_WIDTH, D_CONV), const2),
                  pl.BlockSpec((D_RET + D_CONV, d), const2),
                  pl.BlockSpec((1, d), const2),
                  pl.BlockSpec((N_EXPERTS, d), const2),
                  pl.BlockSpec((N_EXPERTS, 1), const2)],
        out_specs=[pl.BlockSpec((blk, d), lambda i: (i, 0)),
                   pl.BlockSpec((blk, d // 2), lambda i: (i, 0)),
                   pl.BlockSpec((N_EXPERTS, blk), lambda i: (0, i))],
        out_shape=[jax.ShapeDtypeStruct((n, d), F32),
                   jax.ShapeDtypeStruct((n, d // 2), jnp.uint32),
                   jax.ShapeDtypeStruct((N_EXPERTS, n), F32)],
        scratch_shapes=[pltpu.VMEM((N_HEADS, HEAD_DIM, HEAD_DIM), F32),
                        pltpu.VMEM((blk + 8, D_CONV), F32),
                        pltpu.VMEM((blk, D_RET + D_CONV), BF16)],
        compiler_params=_tc_params(("arbitrary",)),
        name="mixer",
    )(proj, x2, mod, cos, sin, dmask, dq, dkv, dblk, conv_w, w_out_bf, g2.reshape(1, d),
      w_router.T, b_router.reshape(N_EXPERTS, 1))


def _top_k_block(lg_ref, tri_ref, run_ref):
    n_e, bw = lg_ref.shape

    @pl.when(pl.program_id(0) == 0)
    def _():
        run_ref[...] = jnp.zeros_like(run_ref)

    l = lg_ref[...]
    eio = lax.broadcasted_iota(jnp.int32, (n_e, bw), 0)
    vals, hots = [], []
    for _ in range(TOP_K):
        m = jnp.max(l, axis=0, keepdims=True)
        idx = jnp.min(jnp.where(l == m, eio, n_e), axis=0, keepdims=True)
        hot = eio == idx
        vals.append(m)
        hots.append(hot)
        l = jnp.where(hot, -jnp.inf, l)
    sel = sum(jnp.where(hot, 1.0, 0.0) for hot in hots)
    incl = jnp.dot(sel.astype(BF16), tri_ref[...], preferred_element_type=F32)
    run = run_ref[:, 0:1]
    rank = incl - sel + run
    run_ref[...] = jnp.broadcast_to(run + incl[:, bw - 1:bw], run_ref.shape)
    return vals, hots, rank


def _count_kernel(lg_ref, tri_ref, cnt_ref, run_ref):
    _top_k_block(lg_ref, tri_ref, run_ref)
    cnt_ref[...] = run_ref[...]


def _route_kernel(lg_ref, tri_ref, low_ref, tot_ref, pos_ref, w_ref, run_ref, *, max_slots):
    vals, hots, rank = _top_k_block(lg_ref, tri_ref, run_ref)
    ex = [jnp.exp(v - vals[0]) for v in vals]
    den = ex[0] + ex[1] + ex[2] + ex[3]
    for k in range(TOP_K):
        w_ref[k:k + 1, :] = ex[k] / den
    tot = tot_ref[...]
    n_slots = sum(jnp.where(tot > float(m * SLOT_ROWS), 1.0, 0.0) for m in range(max_slots))
    start = jnp.dot(low_ref[...], n_slots.astype(BF16), preferred_element_type=F32)
    dest = start[:, 0:1] * float(SLOT_ROWS) + rank
    for k in range(TOP_K):
        pos_ref[k:k + 1, :] = jnp.sum(jnp.where(hots[k], dest, 0.0), axis=0,
                                      keepdims=True).astype(jnp.int32)


def _route(logits_t):
    n_e, n = logits_t.shape
    bw = 512
    max_slots = -(-n // SLOT_ROWS)
    tri = (jnp.arange(bw)[:, None] <= jnp.arange(bw)[None, :]).astype(BF16)
    low = (jnp.arange(n_e)[None, :] < jnp.arange(n_e)[:, None]).astype(BF16)
    lg_spec = pl.BlockSpec((n_e, bw), lambda c: (0, c))
    tri_spec = pl.BlockSpec((bw, bw), lambda c: (0, 0))
    cnt_spec = pl.BlockSpec((n_e, 128), lambda c: (0, 0))
    run = pltpu.VMEM((n_e, 128), F32)
    counts = pl.pallas_call(
        _count_kernel,
        grid=(n // bw,),
        in_specs=[lg_spec, tri_spec],
        out_specs=cnt_spec,
        out_shape=jax.ShapeDtypeStruct((n_e, 128), F32),
        scratch_shapes=[run],
        compiler_params=_tc_params(("arbitrary",)),
        name="route_count",
    )(logits_t, tri)
    pos, w_top = pl.pallas_call(
        functools.partial(_route_kernel, max_slots=max_slots),
        grid=(n // bw,),
        in_specs=[lg_spec, tri_spec, pl.BlockSpec((n_e, n_e), lambda c: (0, 0)), cnt_spec],
        out_specs=[pl.BlockSpec((TOP_K, bw), lambda c: (0, c)),
                   pl.BlockSpec((TOP_K, bw), lambda c: (0, c))],
        out_shape=[jax.ShapeDtypeStruct((TOP_K, n), jnp.int32),
                   jax.ShapeDtypeStruct((TOP_K, n), F32)],
        scratch_shapes=[run],
        compiler_params=_tc_params(("arbitrary",)),
        name="route",
    )(logits_t, tri, low, counts)
    return pos, w_top, counts


def _sc_mesh():
    return plsc.VectorSubcoreMesh(core_axis_name="c", subcore_axis_name="s")


def _sc_worker_id():
    return lax.axis_index("s") * lax.axis_size("c") + lax.axis_index("c")


def _dispatch_rows(h2, pos):
    n, d = h2.shape
    ch = SC_ROWS_PER_COPY
    per_w = n // SC_WORKERS
    n_ch = per_w // ch
    total_rows = _max_slots(n) * SLOT_ROWS
    idx = pos.reshape(TOP_K, SC_WORKERS, n_ch, ch).transpose(1, 2, 0, 3).reshape(
        SC_WORKERS, n_ch * TOP_K, ch)

    @functools.partial(
        pl.kernel, mesh=_sc_mesh(),
        out_type=jax.ShapeDtypeStruct((total_rows, d), h2.dtype),
        scratch_types=[pltpu.VMEM((n_ch * TOP_K, ch), jnp.int32),
                       pltpu.VMEM((ch, d), h2.dtype),
                       pltpu.SemaphoreType.DMA])
    def scatter(x_hbm, idx_hbm, o_hbm, idx_v, rows_v, sem):
        wid = _sc_worker_id()
        pltpu.sync_copy(idx_hbm.at[wid], idx_v)

        @pl.loop(0, n_ch)
        def _(c):
            r0 = pl.multiple_of(wid * per_w + c * ch, ch)
            pltpu.sync_copy(x_hbm.at[pl.ds(r0, ch)], rows_v)
            for k in range(TOP_K):
                pltpu.async_copy(rows_v, o_hbm.at[idx_v.at[c * TOP_K + k]], sem).wait()

    return scatter(h2, idx)


def _combine_rows(ys, pos_flat):
    n = pos_flat.shape[0]
    d = ys.shape[1]
    ch = SC_ROWS_PER_COPY
    per_w = n // SC_WORKERS
    n_ch = per_w // ch
    idx = pos_flat.reshape(SC_WORKERS, n_ch, ch)

    @functools.partial(
        pl.kernel, mesh=_sc_mesh(),
        out_type=jax.ShapeDtypeStruct((n, d), ys.dtype),
        scratch_types=[pltpu.VMEM((n_ch, ch), jnp.int32),
                       pltpu.VMEM((ch, d), ys.dtype),
                       pltpu.SemaphoreType.DMA])
    def gather(y_hbm, idx_hbm, o_hbm, idx_v, rows_v, sem):
        wid = _sc_worker_id()
        pltpu.sync_copy(idx_hbm.at[wid], idx_v)

        @pl.loop(0, n_ch)
        def _(c):
            r0 = pl.multiple_of(wid * per_w + c * ch, ch)
            pltpu.async_copy(y_hbm.at[idx_v.at[c]], rows_v, sem).wait()
            pltpu.sync_copy(rows_v, o_hbm.at[pl.ds(r0, ch)])

    return gather(ys, idx)


def _max_slots(n_tokens):
    return N_EXPERTS + (n_tokens * TOP_K) // SLOT_ROWS


def _expert_kernel(se_ref, sr_ref,
                   xs_ref, wgu_hbm, wd_hbm, bgu_ref, bd_ref,
                   ys_ref,
                   xb_ref, act_ref, wbuf_ref, wbf_ref, sem_ref, *, n_a, n_b):
    s = pl.program_id(0)
    rows = sr_ref[s]
    n_blk = (rows + ROW_BLOCK - 1) // ROW_BLOCK
    tw = wbuf_ref.shape[3]
    d_ff = act_ref.shape[1]
    half_d = ys_ref.shape[1]

    def gate_up_panels(e, t, buf):
        c0 = pl.multiple_of(t * tw, tw)
        return (pltpu.make_async_copy(wgu_hbm.at[e, :, pl.ds(c0, tw)], wbuf_ref.at[buf, 0], sem_ref.at[buf, 0]),
                pltpu.make_async_copy(wgu_hbm.at[e, :, pl.ds(d_ff + c0, tw)], wbuf_ref.at[buf, 1],
                                      sem_ref.at[buf, 1]))

    def down_panels(e, u, buf):
        c0 = pl.multiple_of(u * tw, tw)
        return (pltpu.make_async_copy(wd_hbm.at[e, :, pl.ds(c0, tw)], wbuf_ref.at[buf, 0], sem_ref.at[buf, 0]),
                pltpu.make_async_copy(wd_hbm.at[e, :, pl.ds(half_d + c0, tw)], wbuf_ref.at[buf, 1],
                                      sem_ref.at[buf, 1]))

    def start(panels):
        for cp in panels:
            cp.start()

    def wait_and_cast(panels, buf):
        for cp in panels:
            cp.wait()
        for p in range(2):
            wbf_ref[:, p * tw:(p + 1) * tw] = wbuf_ref[buf, p].astype(BF16)

    expert = se_ref[s]

    def for_row_blocks(fn):
        n_quads = n_blk // 4

        def body(rb, carry):
            fn(pl.multiple_of(rb * 4 * ROW_BLOCK, 4 * ROW_BLOCK), 4 * ROW_BLOCK)
            return carry

        lax.fori_loop(0, n_quads, body, 0)

        @pl.when((n_blk // 2) % 2 == 1)
        def _():
            fn(pl.multiple_of(n_quads * 4 * ROW_BLOCK, 4 * ROW_BLOCK), 2 * ROW_BLOCK)

        @pl.when(n_blk % 2 == 1)
        def _():
            fn(pl.multiple_of((n_blk // 2) * 2 * ROW_BLOCK, 2 * ROW_BLOCK), ROW_BLOCK)

    @pl.when(s == 0)
    def _():
        start(gate_up_panels(expert, 0, 0))

    def unpack(rb, carry):
        r0 = pl.multiple_of(rb * ROW_BLOCK, ROW_BLOCK)
        keep = (r0 + lax.broadcasted_iota(jnp.int32, (ROW_BLOCK, 1), 0)) < rows
        lo, hi = _unpack_bf16_pair(xs_ref[pl.ds(r0, ROW_BLOCK), :])
        xb_ref[pl.ds(r0, ROW_BLOCK), 0:half_d] = jnp.where(keep, lo, 0.0).astype(BF16)
        xb_ref[pl.ds(r0, ROW_BLOCK), half_d:2 * half_d] = jnp.where(keep, hi, 0.0).astype(BF16)
        return carry

    lax.fori_loop(0, n_blk, unpack, 0)

    def gate_up_tile(t, carry):
        buf = t % 2
        wait_and_cast(gate_up_panels(expert, t, buf), buf)

        @pl.when(t + 1 < n_a)
        def _():
            start(gate_up_panels(expert, t + 1, 1 - buf))

        @pl.when(t + 1 == n_a)
        def _():
            start(down_panels(expert, 0, 1 - buf))

        col = pl.multiple_of(t * tw, tw)
        b_gate = bgu_ref[0, :, pl.ds(col, tw)]
        b_up = bgu_ref[0, :, pl.ds(d_ff + col, tw)]

        def gate_up(r0, m):
            gu = jnp.dot(xb_ref[pl.ds(r0, m), :], wbf_ref[...], preferred_element_type=F32)
            gate = jnp.minimum(gu[:, 0:tw] + b_gate, SWIGLU_LIMIT)
            up = jnp.clip(gu[:, tw:2 * tw] + b_up, -SWIGLU_LIMIT, SWIGLU_LIMIT)
            act = (up + 1.0) * (gate * jax.nn.sigmoid(SWIGLU_ALPHA * gate))
            act_ref[pl.ds(r0, m), pl.ds(col, tw)] = act.astype(BF16)

        for_row_blocks(gate_up)
        return carry

    lax.fori_loop(0, n_a, gate_up_tile, 0)

    def down_tile(u, carry):
        buf = (n_a + u) % 2
        wait_and_cast(down_panels(expert, u, buf), buf)

        @pl.when(u + 1 < n_b)
        def _():
            start(down_panels(expert, u + 1, 1 - buf))

        @pl.when((u + 1 == n_b) & (s + 1 < pl.num_programs(0)))
        def _():
            start(gate_up_panels(se_ref[s + 1], 0, 1 - buf))

        col = pl.multiple_of(u * tw, tw)
        b_lo = bd_ref[0, :, pl.ds(col, tw)]
        b_hi = bd_ref[0, :, pl.ds(half_d + col, tw)]

        def down(r0, m):
            y = jnp.dot(act_ref[pl.ds(r0, m), :], wbf_ref[...], preferred_element_type=F32)
            ys_ref[pl.ds(r0, m), pl.ds(col, tw)] = _pack_bf16_pair(y[:, 0:tw] + b_lo, y[:, tw:2 * tw] + b_hi)

        for_row_blocks(down)
        return carry

    lax.fori_loop(0, n_b, down_tile, 0)

    def fill(rb, carry):
        r0 = pl.multiple_of(rb * ROW_BLOCK, ROW_BLOCK)
        ys_ref[pl.ds(r0, ROW_BLOCK), :] = jnp.zeros((ROW_BLOCK, half_d), jnp.uint32)
        return carry

    lax.fori_loop(n_blk, SLOT_ROWS // ROW_BLOCK, fill, 0)


def _experts(xs, slot_expert, slot_rows, n_used, w_gate_up, b_gate_up, w_down, b_down):
    n_e, d, two_ff = w_gate_up.shape
    d_ff = two_ff // 2
    assert d == d_ff, "the weight ring holds (rows, W_PANEL) panels of both projections"
    tw = W_PANEL
    n_a, n_b = d_ff // tw, (d // 2) // tw
    assert (n_a + n_b) % 2 == 0, "ring slot parity must repeat from one expert slot to the next"
    total_rows = xs.shape[0]

    grid_spec = pltpu.PrefetchScalarGridSpec(
        num_scalar_prefetch=2,
        grid=(n_used,),
        in_specs=[
            pl.BlockSpec((SLOT_ROWS, d // 2), lambda s, se, sr: (s, 0)),
            pl.BlockSpec(memory_space=pl.ANY),
            pl.BlockSpec(memory_space=pl.ANY),
            pl.BlockSpec((1, 1, two_ff), lambda s, se, sr: (se[s], 0, 0)),
            pl.BlockSpec((1, 1, d), lambda s, se, sr: (se[s], 0, 0)),
        ],
        out_specs=pl.BlockSpec((SLOT_ROWS, d // 2), lambda s, se, sr: (s, 0)),
        scratch_shapes=[pltpu.VMEM((SLOT_ROWS, d), BF16),
                        pltpu.VMEM((SLOT_ROWS, d_ff), BF16),
                        pltpu.VMEM((2, 2, d, tw), F32),
                        pltpu.VMEM((d, 2 * tw), BF16),
                        pltpu.SemaphoreType.DMA((2, 2))],
    )
    return pl.pallas_call(
        functools.partial(_expert_kernel, n_a=n_a, n_b=n_b),
        grid_spec=grid_spec,
        out_shape=jax.ShapeDtypeStruct((total_rows, d // 2), jnp.uint32),
        compiler_params=_tc_params(("arbitrary",)),
        name="experts",
    )(slot_expert, slot_rows, xs, w_gate_up, w_down,
      b_gate_up.reshape(n_e, 1, two_ff), b_down.reshape(n_e, 1, d))


def _slot_tables(counts, n_tokens):
    max_slots = _max_slots(n_tokens)
    n_slots = (counts + SLOT_ROWS - 1) // SLOT_ROWS
    slot_end = jnp.cumsum(n_slots)
    slot_start = slot_end - n_slots
    n_used = slot_end[-1]
    sid = jnp.arange(max_slots, dtype=jnp.int32)
    expert = jnp.minimum(jnp.sum(sid[:, None] >= slot_end[None, :], axis=1), N_EXPERTS - 1).astype(jnp.int32)
    local = sid - slot_start[expert]
    rows = jnp.clip(counts[expert] - local * SLOT_ROWS, 0, SLOT_ROWS)
    rows = jnp.where(sid < n_used, rows, 0).astype(jnp.int32)
    return expert, rows, n_used.reshape(1).astype(jnp.int32)


def _final_kernel(x1_ref, y0_ref, y1_ref, y2_ref, y3_ref, w_ref, mod_ref, g_ref, o_ref):
    half = y0_ref.shape[1]
    d = 2 * half
    moe_lo = moe_hi = None
    for k, y_ref in enumerate((y0_ref, y1_ref, y2_ref, y3_ref)):
        lo, hi = _unpack_bf16_pair(y_ref[...])
        wk = w_ref[:, k:k + 1]
        moe_lo = wk * lo if k == 0 else moe_lo + wk * lo
        moe_hi = wk * hi if k == 0 else moe_hi + wk * hi
    x_lo = x1_ref[:, 0:half] + mod_ref[0, 5:6, 0:half] * moe_lo
    x_hi = x1_ref[:, half:d] + mod_ref[0, 5:6, half:d] * moe_hi
    ss = jnp.sum(x_lo * x_lo, axis=-1, keepdims=True) + jnp.sum(x_hi * x_hi, axis=-1, keepdims=True)
    r = lax.rsqrt(ss / d + EPS)
    o_ref[:, 0:half] = x_lo * r * g_ref[:, 0:half]
    o_ref[:, half:d] = x_hi * r * g_ref[:, half:d]


def _final(x1, y4, w_tok, mod, final_g, seq):
    n, d = x1.shape
    tm = 256
    per_batch = seq // tm
    n_i = n // tm
    y_specs = [pl.BlockSpec((tm, d // 2), functools.partial(lambda i, k: (k * n_i + i, 0), k=k))
               for k in range(TOP_K)]
    return pl.pallas_call(
        _final_kernel,
        grid=(n_i,),
        in_specs=[pl.BlockSpec((tm, d), lambda i: (i, 0))] + y_specs + [
                  pl.BlockSpec((tm, TOP_K), lambda i: (i, 0)),
                  pl.BlockSpec((1, 6, d), lambda i: (i // per_batch, 0, 0)),
                  pl.BlockSpec((1, d), lambda i: (0, 0))],
        out_specs=pl.BlockSpec((tm, d), lambda i: (i, 0)),
        out_shape=jax.ShapeDtypeStruct((n, d), F32),
        compiler_params=_tc_params(("arbitrary",)),
        name="final",
    )(x1, y4, y4, y4, y4, w_tok, mod, final_g.reshape(1, d))


def kernel(x, c, norm1_g, w_mod, b_mod, w_in, conv_w, w_out, norm2_g, w_router, b_router,
           w_gate_up, b_gate_up, w_down, b_down, final_g):
    b, seq, d = x.shape
    n = b * seq
    x2 = x.reshape(n, d)
    mod = _modulation(c, w_mod[0], b_mod[0])
    proj = _in_projection(x2, mod, norm1_g[0], w_in[0].astype(BF16), seq)
    x1, h2, logits_t = _mixer(proj, x2, mod, conv_w[0], w_out[0].astype(BF16), norm2_g[0],
                              w_router[0], b_router[0], seq)
    pos, w_top, counts = _route(logits_t)
    slot_expert, slot_rows, n_used = _slot_tables(counts[:, 0].astype(jnp.int32), n)
    xs = _dispatch_rows(h2, pos)
    ys = _experts(xs, slot_expert, slot_rows, n_used[0], w_gate_up[0], b_gate_up[0], w_down[0], b_down[0])
    y4 = _combine_rows(ys, pos.reshape(TOP_K * n))
    out = _final(x1, y4, w_top.T, mod, final_g, seq)
    return out.reshape(b, seq, d)
```

```python
import functools
import math

import jax
import jax.numpy as jnp
from jax import lax
from jax.experimental import pallas as pl
from jax.experimental.pallas import tpu as pltpu
from jax.experimental.pallas import tpu_sc as plsc

F32 = jnp.float32
BF16 = jnp.bfloat16

CHUNK = 64
N_HEADS = 8
HEAD_DIM = 128
D_RET = N_HEADS * HEAD_DIM
D_CONV = 1024
CONV_WIDTH = 3
N_EXPERTS = 32
TOP_K = 4
SWIGLU_LIMIT = 7.0
SWIGLU_ALPHA = 1.702
ROPE_BASE = 10000.0
EPS = 1e-6

VMEM_LIMIT_BYTES = 58 * 1024 * 1024
SC_WORKERS = 32
SC_ROWS_PER_COPY = 32

MIX_BLOCK = 256
ROW_BLOCK = 128
SLOT_ROWS = 9 * ROW_BLOCK
W_PANEL = 512


def _tc_params(sem):
    return pltpu.CompilerParams(dimension_semantics=sem, vmem_limit_bytes=VMEM_LIMIT_BYTES)


def _bf16_bits(x):
    b = lax.bitcast_convert_type(x, jnp.uint32)
    return (b + jnp.uint32(0x7FFF) + ((b >> 16) & jnp.uint32(1))) & jnp.uint32(0xFFFF0000)


def _pack_bf16_pair(lo, hi):
    return (_bf16_bits(lo) >> 16) | _bf16_bits(hi)


def _unpack_bf16_pair(p):
    lo = lax.bitcast_convert_type(p << 16, F32)
    hi = lax.bitcast_convert_type(p & jnp.uint32(0xFFFF0000), F32)
    return lo, hi


def _mod_kernel(c_ref, w_ref, b_ref, o_ref):
    c = c_ref[...]
    ca = (c * jax.nn.sigmoid(c)).astype(BF16)
    o_ref[...] = jnp.dot(ca, w_ref[...].astype(BF16), preferred_element_type=F32) + b_ref[...]


def _modulation(c, w_mod, b_mod):
    b, d = c.shape
    n = w_mod.shape[1]
    tn = 1024
    c8 = jnp.zeros((8, d), F32).at[:b].set(c)
    out = pl.pallas_call(
        _mod_kernel,
        grid=(n // tn,),
        in_specs=[pl.BlockSpec((8, d), lambda j: (0, 0)),
                  pl.BlockSpec((d, tn), lambda j: (0, j)),
                  pl.BlockSpec((1, tn), lambda j: (0, j))],
        out_specs=pl.BlockSpec((8, tn), lambda j: (0, j)),
        out_shape=jax.ShapeDtypeStruct((8, n), F32),
        compiler_params=_tc_params(("arbitrary",)),
        name="mod",
    )(c8, w_mod, b_mod.reshape(1, n))
    return out[:b].reshape(b, 6, d)


def _norm_mod(x, g, scale, shift):
    y = x * lax.rsqrt(jnp.mean(x * x, axis=-1, keepdims=True) + EPS) * g
    return y * (1.0 + scale) + shift


def _inproj_kernel(x_ref, mod_ref, g_ref, w_ref, o_ref, h_ref, *, rows):
    @pl.when(pl.program_id(1) == 0)
    def _():
        g = g_ref[...]
        scale = mod_ref[0, 1:2, :]
        shift = mod_ref[0, 0:1, :]

        def body(r, carry):
            r0 = pl.multiple_of(r * rows, rows)
            h = _norm_mod(x_ref[pl.ds(r0, rows), :], g, scale, shift)
            h_ref[pl.ds(r0, rows), :] = h.astype(BF16)
            return carry

        lax.fori_loop(0, x_ref.shape[0] // rows, body, 0)

    o_ref[...] = jnp.dot(h_ref[...], w_ref[...], preferred_element_type=F32).astype(o_ref.dtype)


def _in_projection(x2, mod, g1, w_in_bf, seq):
    n, d = x2.shape
    p = w_in_bf.shape[1]
    tm, tn = min(1024, seq), 1024
    per_batch = seq // tm
    return pl.pallas_call(
        functools.partial(_inproj_kernel, rows=128),
        grid=(n // tm, p // tn),
        in_specs=[pl.BlockSpec((tm, d), lambda i, j: (i, 0)),
                  pl.BlockSpec((1, 6, d), lambda i, j: (i // per_batch, 0, 0)),
                  pl.BlockSpec((1, d), lambda i, j: (0, 0)),
                  pl.BlockSpec((d, tn), lambda i, j: (0, j))],
        out_specs=pl.BlockSpec((tm, tn), lambda i, j: (i, j)),
        out_shape=jax.ShapeDtypeStruct((n, p), BF16),
        scratch_shapes=[pltpu.VMEM((tm, d), BF16)],
        compiler_params=_tc_params(("arbitrary", "arbitrary")),
        name="inproj",
    )(x2, mod, g1.reshape(1, d), w_in_bf)


def _mixer_tables(seq, blk):
    half = HEAD_DIM // 2
    freqs = ROPE_BASE ** (-jnp.arange(half, dtype=F32) / half)
    ang = jnp.arange(seq, dtype=F32)[:, None] * freqs[None, :]
    cos = jnp.concatenate([jnp.cos(ang), jnp.cos(ang)], axis=-1)
    sin = jnp.concatenate([-jnp.sin(ang), jnp.sin(ang)], axis=-1)
    log_gamma = jnp.log1p(-jnp.exp2(-5.0 - jnp.arange(N_HEADS, dtype=F32)))
    idx = jnp.arange(blk, dtype=F32)
    dist = jnp.abs(idx[:, None] - idx[None, :])
    ck = jnp.arange(blk) // CHUNK
    visible = (ck[None, :] <= ck[:, None]).astype(F32)
    dmask = jnp.exp(log_gamma[:, None, None] * dist) * visible
    ones = jnp.ones((1, 1, HEAD_DIM), F32)
    dq = jnp.exp(log_gamma[:, None] * (idx + 1.0)[None])[:, :, None] * ones
    dkv = jnp.exp(log_gamma[:, None] * (blk - 1 - idx)[None])[:, :, None] * ones
    dblk = jnp.exp(log_gamma * blk)[:, None, None] * ones
    return cos, sin, dmask, dq, dkv, dblk


def _mixer_kernel(proj_ref, x_ref, mod_ref, cos_ref, sin_ref, dmask_ref, dq_ref, dkv_ref, dblk_ref,
                  convw_ref, wout_ref, g2_ref, wr_ref, br_ref,
                  x1_ref, h2_ref, lg_ref,
                  s_ref, zbuf_ref, y_ref, *, per_batch, n_blocks):
    blk = x_ref.shape[0]
    i = pl.program_id(0)
    cur = i % 2

    @pl.when(jnp.minimum(i, n_blocks - 1) % per_batch == 0)
    def _():
        s_ref[...] = jnp.zeros_like(s_ref)
        zbuf_ref[0:8, :] = jnp.zeros((8, D_CONV), F32)

    @pl.when(i == 0)
    def _():
        y_ref[1] = jnp.zeros(y_ref.shape[1:], BF16)

    cos = cos_ref[...]
    sin = sin_ref[...]
    k_scale = HEAD_DIM ** -0.5
    nt = (((1,), (1,)), ((), ()))
    tn = (((0,), (0,)), ((), ()))
    for h in range(N_HEADS):
        c0 = h * HEAD_DIM
        q = proj_ref[:, c0:c0 + HEAD_DIM].astype(F32)
        k = proj_ref[:, D_RET + c0:D_RET + c0 + HEAD_DIM].astype(F32)
        v = proj_ref[:, 2 * D_RET + c0:2 * D_RET + c0 + HEAD_DIM]
        g = proj_ref[:, 3 * D_RET + c0:3 * D_RET + c0 + HEAD_DIM].astype(F32)
        qr = q * cos + pltpu.roll(q, HEAD_DIM // 2, 1) * sin
        kr = (k * cos + pltpu.roll(k, HEAD_DIM // 2, 1) * sin) * k_scale
        qb = qr.astype(BF16)
        kb = kr.astype(BF16)
        scores = lax.dot_general(qb, kb, nt, preferred_element_type=F32) * dmask_ref[h]
        intra = jnp.dot(scores.astype(BF16), v, preferred_element_type=F32)
        state = s_ref[h]
        cross = jnp.dot(qb, state.astype(BF16), preferred_element_type=F32) * dq_ref[h]
        kd = (kr * dkv_ref[h]).astype(BF16)
        kv = lax.dot_general(kd, v, tn, preferred_element_type=F32)
        s_ref[h] = dblk_ref[h] * state + kv
        o = intra + cross
        mu = jnp.mean(o, axis=-1, keepdims=True)
        dev = o - mu
        var = jnp.mean(dev * dev, axis=-1, keepdims=True)
        on = dev * lax.rsqrt(var + EPS)
        y_ref[cur, :, c0:c0 + HEAD_DIM] = (g * jax.nn.sigmoid(g) * on).astype(BF16)

    cw = 256
    base = 4 * D_RET
    for cb in range(D_CONV // cw):
        lo = cb * cw
        cg = proj_ref[:, base + D_CONV + lo:base + D_CONV + lo + cw].astype(F32)
        u = proj_ref[:, base + 2 * D_CONV + lo:base + 2 * D_CONV + lo + cw].astype(F32)
        zbuf_ref[8:blk + 8, lo:lo + cw] = cg * u
    for cb in range(D_CONV // cw):
        lo = cb * cw
        z0 = zbuf_ref[8:blk + 8, lo:lo + cw]
        z1 = zbuf_ref[7:blk + 7, lo:lo + cw]
        z2 = zbuf_ref[6:blk + 6, lo:lo + cw]
        z = (convw_ref[2:3, lo:lo + cw] * z0 + convw_ref[1:2, lo:lo + cw] * z1
             + convw_ref[0:1, lo:lo + cw] * z2)
        bg = proj_ref[:, base + lo:base + lo + cw].astype(F32)
        y_ref[cur, :, D_RET + lo:D_RET + lo + cw] = (bg * z).astype(BF16)
    zbuf_ref[0:8, :] = zbuf_ref[blk:blk + 8, :]

    mix = jnp.dot(y_ref[1 - cur], wout_ref[...], preferred_element_type=F32)
    x1 = x_ref[...] + mod_ref[0, 2:3, :] * mix
    x1_ref[...] = x1
    h2 = _norm_mod(x1, g2_ref[...], mod_ref[0, 4:5, :], mod_ref[0, 3:4, :])
    half = h2.shape[1] // 2
    h2_ref[...] = _pack_bf16_pair(h2[:, :half], h2[:, half:])
    w2 = wr_ref[...]
    w_hi = w2.astype(BF16)
    w_lo = (w2 - w_hi.astype(F32)).astype(BF16)
    w_split = jnp.where(lax.broadcasted_iota(jnp.int32, w2.shape, 1) < N_EXPERTS, w_hi, w_lo)
    h_hi = h2.astype(BF16)
    h_lo = (h2 - h_hi.astype(F32)).astype(BF16)
    r = (jnp.dot(h_hi, w_split, preferred_element_type=F32)
         + jnp.dot(h_lo, w_split, preferred_element_type=F32))
    lg_ref[...] = r[:, 0:N_EXPERTS] + r[:, N_EXPERTS:2 * N_EXPERTS] + br_ref[...]


def _mixer(proj, x2, mod, conv_w, w_out_bf, g2, w_router, b_router, seq):
    n, d = x2.shape
    blk = MIX_BLOCK
    per_batch = seq // blk
    cos, sin, dmask, dq, dkv, dblk = _mixer_tables(seq, blk)
    const2 = lambda i: (0, 0)
    const3 = lambda i: (0, 0, 0)
    n_blocks = n // blk
    mixed = lambda i: jnp.minimum(i, n_blocks - 1)
    done = lambda i: jnp.maximum(i - 1, 0)
    return pl.pallas_call(
        functools.partial(_mixer_kernel, per_batch=per_batch, n_blocks=n_blocks),
        grid=(n_blocks + 1,),
        in_specs=[pl.BlockSpec((blk, proj.shape[1]), lambda i: (mixed(i), 0)),
                  pl.BlockSpec((blk, d), lambda i: (done(i), 0)),
                  pl.BlockSpec((1, 6, d), lambda i: (done(i) // per_batch, 0, 0)),
                  pl.BlockSpec((blk, HEAD_DIM), lambda i: (mixed(i) % per_batch, 0)),
                  pl.BlockSpec((blk, HEAD_DIM), lambda i: (mixed(i) % per_batch, 0)),
                  pl.BlockSpec((N_HEADS, blk, blk), const3),
                  pl.BlockSpec((N_HEADS, blk, HEAD_DIM), const3),
                  pl.BlockSpec((N_HEADS, blk, HEAD_DIM), const3),
                  pl.BlockSpec((N_HEADS, 1, HEAD_DIM), const3),
                  pl.BlockSpec((CONV_WIDTH, D_CONV), const2),
                  pl.BlockSpec((D_RET + D_CONV, d), const2),
                  pl.BlockSpec((1, d), const2),
                  pl.BlockSpec((d, 2 * N_EXPERTS), const2),
                  pl.BlockSpec((1, N_EXPERTS), const2)],
        out_specs=[pl.BlockSpec((blk, d), lambda i: (done(i), 0)),
                   pl.BlockSpec((blk, d // 2), lambda i: (done(i), 0)),
                   pl.BlockSpec((blk, N_EXPERTS), lambda i: (done(i), 0))],
        out_shape=[jax.ShapeDtypeStruct((n, d), F32),
                   jax.ShapeDtypeStruct((n, d // 2), jnp.uint32),
                   jax.ShapeDtypeStruct((n, N_EXPERTS), F32)],
        scratch_shapes=[pltpu.VMEM((N_HEADS, HEAD_DIM, HEAD_DIM), F32),
                        pltpu.VMEM((blk + 8, D_CONV), F32),
                        pltpu.VMEM((2, blk, D_RET + D_CONV), BF16)],
        compiler_params=_tc_params(("arbitrary",)),
        name="mixer",
    )(proj, x2, mod, cos, sin, dmask, dq, dkv, dblk, conv_w, w_out_bf, g2.reshape(1, d),
      jnp.concatenate([w_router, w_router], axis=1), b_router.reshape(1, N_EXPERTS))


def _top_k_block(lg_ref, tri_ref, run_ref):
    n_e, bw = lg_ref.shape

    @pl.when(pl.program_id(0) == 0)
    def _():
        run_ref[...] = jnp.zeros_like(run_ref)

    l = lg_ref[...]
    eio = lax.broadcasted_iota(jnp.int32, (n_e, bw), 0)
    vals, hots = [], []
    for _ in range(TOP_K):
        m = jnp.max(l, axis=0, keepdims=True)
        idx = jnp.min(jnp.where(l == m, eio, n_e), axis=0, keepdims=True)
        hot = eio == idx
        vals.append(m)
        hots.append(hot)
        l = jnp.where(hot, -jnp.inf, l)
    sel = sum(jnp.where(hot, 1.0, 0.0) for hot in hots)
    incl = jnp.dot(sel.astype(BF16), tri_ref[...], preferred_element_type=F32)
    run = run_ref[:, 0:1]
    rank = incl - sel + run
    run_ref[...] = jnp.broadcast_to(run + incl[:, bw - 1:bw], run_ref.shape)
    return vals, hots, rank


def _count_kernel(lg_ref, tri_ref, cnt_ref, run_ref):
    _top_k_block(lg_ref, tri_ref, run_ref)
    cnt_ref[...] = run_ref[...]


def _route_kernel(lg_ref, tri_ref, low_ref, tot_ref, pos_ref, w_ref, run_ref, *, max_slots):
    vals, hots, rank = _top_k_block(lg_ref, tri_ref, run_ref)
    ex = [jnp.exp(v - vals[0]) for v in vals]
    den = ex[0] + ex[1] + ex[2] + ex[3]
    for k in range(TOP_K):
        w_ref[k:k + 1, :] = ex[k] / den
    tot = tot_ref[...]
    n_slots = sum(jnp.where(tot > float(m * SLOT_ROWS), 1.0, 0.0) for m in range(max_slots))
    start = jnp.dot(low_ref[...], n_slots.astype(BF16), preferred_element_type=F32)
    dest = start[:, 0:1] * float(SLOT_ROWS) + rank
    for k in range(TOP_K):
        pos_ref[k:k + 1, :] = jnp.sum(jnp.where(hots[k], dest, 0.0), axis=0,
                                      keepdims=True).astype(jnp.int32)


def _route(logits_t):
    n_e, n = logits_t.shape
    bw = 512
    max_slots = -(-n // SLOT_ROWS)
    tri = (jnp.arange(bw)[:, None] <= jnp.arange(bw)[None, :]).astype(BF16)
    low = (jnp.arange(n_e)[None, :] < jnp.arange(n_e)[:, None]).astype(BF16)
    lg_spec = pl.BlockSpec((n_e, bw), lambda c: (0, c))
    tri_spec = pl.BlockSpec((bw, bw), lambda c: (0, 0))
    cnt_spec = pl.BlockSpec((n_e, 128), lambda c: (0, 0))
    run = pltpu.VMEM((n_e, 128), F32)
    counts = pl.pallas_call(
        _count_kernel,
        grid=(n // bw,),
        in_specs=[lg_spec, tri_spec],
        out_specs=cnt_spec,
        out_shape=jax.ShapeDtypeStruct((n_e, 128), F32),
        scratch_shapes=[run],
        compiler_params=_tc_params(("arbitrary",)),
        name="route_count",
    )(logits_t, tri)
    pos, w_top = pl.pallas_call(
        functools.partial(_route_kernel, max_slots=max_slots),
        grid=(n // bw,),
        in_specs=[lg_spec, tri_spec, pl.BlockSpec((n_e, n_e), lambda c: (0, 0)), cnt_spec],
        out_specs=[pl.BlockSpec((TOP_K, bw), lambda c: (0, c)),
                   pl.BlockSpec((TOP_K, bw), lambda c: (0, c))],
        out_shape=[jax.ShapeDtypeStruct((TOP_K, n), jnp.int32),
                   jax.ShapeDtypeStruct((TOP_K, n), F32)],
        scratch_shapes=[run],
        compiler_params=_tc_params(("arbitrary",)),
        name="route",
    )(logits_t, tri, low, counts)
    return pos, w_top, counts


def _sc_mesh():
    return plsc.VectorSubcoreMesh(core_axis_name="c", subcore_axis_name="s")


def _sc_worker_id():
    return lax.axis_index("s") * lax.axis_size("c") + lax.axis_index("c")


def _dispatch_rows(h2, pos):
    n, d = h2.shape
    ch = SC_ROWS_PER_COPY
    per_w = n // SC_WORKERS
    n_ch = per_w // ch
    total_rows = _max_slots(n) * SLOT_ROWS
    idx = pos.reshape(TOP_K, SC_WORKERS, n_ch, ch).transpose(1, 2, 0, 3).reshape(
        SC_WORKERS, n_ch * TOP_K, ch)

    @functools.partial(
        pl.kernel, mesh=_sc_mesh(),
        out_type=jax.ShapeDtypeStruct((total_rows, d), h2.dtype),
        scratch_types=[pltpu.VMEM((n_ch * TOP_K, ch), jnp.int32),
                       pltpu.VMEM((ch, d), h2.dtype),
                       pltpu.SemaphoreType.DMA])
    def scatter(x_hbm, idx_hbm, o_hbm, idx_v, rows_v, sem):
        wid = _sc_worker_id()
        pltpu.sync_copy(idx_hbm.at[wid], idx_v)

        @pl.loop(0, n_ch)
        def _(c):
            r0 = pl.multiple_of(wid * per_w + c * ch, ch)
            pltpu.sync_copy(x_hbm.at[pl.ds(r0, ch)], rows_v)
            for k in range(TOP_K):
                pltpu.async_copy(rows_v, o_hbm.at[idx_v.at[c * TOP_K + k]], sem).wait()

    return scatter(h2, idx)


def _combine_rows(ys, pos_flat):
    n = pos_flat.shape[0]
    d = ys.shape[1]
    ch = SC_ROWS_PER_COPY
    per_w = n // SC_WORKERS
    n_ch = per_w // ch
    idx = pos_flat.reshape(SC_WORKERS, n_ch, ch)

    @functools.partial(
        pl.kernel, mesh=_sc_mesh(),
        out_type=jax.ShapeDtypeStruct((n, d), ys.dtype),
        scratch_types=[pltpu.VMEM((n_ch, ch), jnp.int32),
                       pltpu.VMEM((ch, d), ys.dtype),
                       pltpu.SemaphoreType.DMA])
    def gather(y_hbm, idx_hbm, o_hbm, idx_v, rows_v, sem):
        wid = _sc_worker_id()
        pltpu.sync_copy(idx_hbm.at[wid], idx_v)

        @pl.loop(0, n_ch)
        def _(c):
            r0 = pl.multiple_of(wid * per_w + c * ch, ch)
            pltpu.async_copy(y_hbm.at[idx_v.at[c]], rows_v, sem).wait()
            pltpu.sync_copy(rows_v, o_hbm.at[pl.ds(r0, ch)])

    return gather(ys, idx)


def _max_slots(n_tokens):
    return N_EXPERTS + (n_tokens * TOP_K) // SLOT_ROWS


def _expert_kernel(se_ref, sr_ref,
                   xs_ref, wgu_hbm, wd_hbm, bgu_ref, bd_ref,
                   ys_ref,
                   xb_ref, act_ref, wbuf_ref, wbf_ref, sem_ref, *, n_a, n_b):
    s = pl.program_id(0)
    rows = sr_ref[s]
    n_blk = (rows + ROW_BLOCK - 1) // ROW_BLOCK
    tw = wbuf_ref.shape[3]
    d_ff = act_ref.shape[1]
    half_d = ys_ref.shape[1]

    def gate_up_panels(e, t, buf):
        c0 = pl.multiple_of(t * tw, tw)
        return (pltpu.make_async_copy(wgu_hbm.at[e, :, pl.ds(c0, tw)], wbuf_ref.at[buf, 0], sem_ref.at[buf, 0]),
                pltpu.make_async_copy(wgu_hbm.at[e, :, pl.ds(d_ff + c0, tw)], wbuf_ref.at[buf, 1],
                                      sem_ref.at[buf, 1]))

    def down_panels(e, u, buf):
        c0 = pl.multiple_of(u * tw, tw)
        return (pltpu.make_async_copy(wd_hbm.at[e, :, pl.ds(c0, tw)], wbuf_ref.at[buf, 0], sem_ref.at[buf, 0]),
                pltpu.make_async_copy(wd_hbm.at[e, :, pl.ds(half_d + c0, tw)], wbuf_ref.at[buf, 1],
                                      sem_ref.at[buf, 1]))

    def start(panels):
        for cp in panels:
            cp.start()

    def wait(panels):
        for cp in panels:
            cp.wait()

    def cast(buf):
        for p in range(2):
            wbf_ref[:, p * tw:(p + 1) * tw] = wbuf_ref[buf, p].astype(BF16)

    expert = se_ref[s]

    def for_row_blocks(fn):
        n_quads = n_blk // 4

        def body(rb, carry):
            fn(pl.multiple_of(rb * 4 * ROW_BLOCK, 4 * ROW_BLOCK), 4 * ROW_BLOCK)
            return carry

        lax.fori_loop(0, n_quads, body, 0)

        @pl.when((n_blk // 2) % 2 == 1)
        def _():
            fn(pl.multiple_of(n_quads * 4 * ROW_BLOCK, 4 * ROW_BLOCK), 2 * ROW_BLOCK)

        @pl.when(n_blk % 2 == 1)
        def _():
            fn(pl.multiple_of((n_blk // 2) * 2 * ROW_BLOCK, 2 * ROW_BLOCK), ROW_BLOCK)

    @pl.when(s == 0)
    def _():
        start(gate_up_panels(expert, 0, 0))

    def unpack(rb, carry):
        r0 = pl.multiple_of(rb * ROW_BLOCK, ROW_BLOCK)
        keep = (r0 + lax.broadcasted_iota(jnp.int32, (ROW_BLOCK, 1), 0)) < rows
        lo, hi = _unpack_bf16_pair(xs_ref[pl.ds(r0, ROW_BLOCK), :])
        xb_ref[pl.ds(r0, ROW_BLOCK), 0:half_d] = jnp.where(keep, lo, 0.0).astype(BF16)
        xb_ref[pl.ds(r0, ROW_BLOCK), half_d:2 * half_d] = jnp.where(keep, hi, 0.0).astype(BF16)
        return carry

    lax.fori_loop(0, n_blk, unpack, 0)

    def gate_up_tile(t, carry):
        buf = t % 2
        wait(gate_up_panels(expert, t, buf))

        @pl.when(t + 1 < n_a)
        def _():
            start(gate_up_panels(expert, t + 1, 1 - buf))

        @pl.when(t + 1 == n_a)
        def _():
            start(down_panels(expert, 0, 1 - buf))

        cast(buf)

        col = pl.multiple_of(t * tw, tw)
        b_gate = bgu_ref[0, :, pl.ds(col, tw)]
        b_up = bgu_ref[0, :, pl.ds(d_ff + col, tw)]

        def gate_up(r0, m):
            gu = jnp.dot(xb_ref[pl.ds(r0, m), :], wbf_ref[...], preferred_element_type=F32)
            gate = jnp.minimum(gu[:, 0:tw] + b_gate, SWIGLU_LIMIT)
            up = jnp.clip(gu[:, tw:2 * tw] + b_up, -SWIGLU_LIMIT, SWIGLU_LIMIT)
            act = (up + 1.0) * (gate * jax.nn.sigmoid(SWIGLU_ALPHA * gate))
            act_ref[pl.ds(r0, m), pl.ds(col, tw)] = act.astype(BF16)

        for_row_blocks(gate_up)
        return carry

    lax.fori_loop(0, n_a, gate_up_tile, 0)

    def down_tile(u, carry):
        buf = (n_a + u) % 2
        wait(down_panels(expert, u, buf))

        @pl.when(u + 1 < n_b)
        def _():
            start(down_panels(expert, u + 1, 1 - buf))

        @pl.when((u + 1 == n_b) & (s + 1 < pl.num_programs(0)))
        def _():
            start(gate_up_panels(se_ref[s + 1], 0, 1 - buf))

        cast(buf)

        col = pl.multiple_of(u * tw, tw)
        b_lo = bd_ref[0, :, pl.ds(col, tw)]
        b_hi = bd_ref[0, :, pl.ds(half_d + col, tw)]

        def down(r0, m):
            y = jnp.dot(act_ref[pl.ds(r0, m), :], wbf_ref[...], preferred_element_type=F32)
            ys_ref[pl.ds(r0, m), pl.ds(col, tw)] = _pack_bf16_pair(y[:, 0:tw] + b_lo, y[:, tw:2 * tw] + b_hi)

        for_row_blocks(down)
        return carry

    lax.fori_loop(0, n_b, down_tile, 0)

    def fill(rb, carry):
        r0 = pl.multiple_of(rb * ROW_BLOCK, ROW_BLOCK)
        ys_ref[pl.ds(r0, ROW_BLOCK), :] = jnp.zeros((ROW_BLOCK, half_d), jnp.uint32)
        return carry

    lax.fori_loop(n_blk, SLOT_ROWS // ROW_BLOCK, fill, 0)


def _experts(xs, slot_expert, slot_rows, n_used, w_gate_up, b_gate_up, w_down, b_down):
    n_e, d, two_ff = w_gate_up.shape
    d_ff = two_ff // 2
    assert d == d_ff, "the weight ring holds (rows, W_PANEL) panels of both projections"
    tw = W_PANEL
    n_a, n_b = d_ff // tw, (d // 2) // tw
    assert (n_a + n_b) % 2 == 0, "ring slot parity must repeat from one expert slot to the next"
    total_rows = xs.shape[0]

    grid_spec = pltpu.PrefetchScalarGridSpec(
        num_scalar_prefetch=2,
        grid=(n_used,),
        in_specs=[
            pl.BlockSpec((SLOT_ROWS, d // 2), lambda s, se, sr: (s, 0)),
            pl.BlockSpec(memory_space=pl.ANY),
            pl.BlockSpec(memory_space=pl.ANY),
            pl.BlockSpec((1, 1, two_ff), lambda s, se, sr: (se[s], 0, 0)),
            pl.BlockSpec((1, 1, d), lambda s, se, sr: (se[s], 0, 0)),
        ],
        out_specs=pl.BlockSpec((SLOT_ROWS, d // 2), lambda s, se, sr: (s, 0)),
        scratch_shapes=[pltpu.VMEM((SLOT_ROWS, d), BF16),
                        pltpu.VMEM((SLOT_ROWS, d_ff), BF16),
                        pltpu.VMEM((2, 2, d, tw), F32),
                        pltpu.VMEM((d, 2 * tw), BF16),
                        pltpu.SemaphoreType.DMA((2, 2))],
    )
    return pl.pallas_call(
        functools.partial(_expert_kernel, n_a=n_a, n_b=n_b),
        grid_spec=grid_spec,
        out_shape=jax.ShapeDtypeStruct((total_rows, d // 2), jnp.uint32),
        compiler_params=_tc_params(("arbitrary",)),
        name="experts",
    )(slot_expert, slot_rows, xs, w_gate_up, w_down,
      b_gate_up.reshape(n_e, 1, two_ff), b_down.reshape(n_e, 1, d))


def _slot_tables(counts, n_tokens):
    max_slots = _max_slots(n_tokens)
    n_slots = (counts + SLOT_ROWS - 1) // SLOT_ROWS
    slot_end = jnp.cumsum(n_slots)
    slot_start = slot_end - n_slots
    n_used = slot_end[-1]
    sid = jnp.arange(max_slots, dtype=jnp.int32)
    expert = jnp.minimum(jnp.sum(sid[:, None] >= slot_end[None, :], axis=1), N_EXPERTS - 1).astype(jnp.int32)
    local = sid - slot_start[expert]
    rows = jnp.clip(counts[expert] - local * SLOT_ROWS, 0, SLOT_ROWS)
    rows = jnp.where(sid < n_used, rows, 0).astype(jnp.int32)
    return expert, rows, n_used.reshape(1).astype(jnp.int32)


def _final_kernel(x1_ref, y0_ref, y1_ref, y2_ref, y3_ref, w_ref, mod_ref, g_ref, o_ref):
    half = y0_ref.shape[1]
    d = 2 * half
    moe_lo = moe_hi = None
    for k, y_ref in enumerate((y0_ref, y1_ref, y2_ref, y3_ref)):
        lo, hi = _unpack_bf16_pair(y_ref[...])
        wk = w_ref[:, k:k + 1]
        moe_lo = wk * lo if k == 0 else moe_lo + wk * lo
        moe_hi = wk * hi if k == 0 else moe_hi + wk * hi
    x_lo = x1_ref[:, 0:half] + mod_ref[0, 5:6, 0:half] * moe_lo
    x_hi = x1_ref[:, half:d] + mod_ref[0, 5:6, half:d] * moe_hi
    ss = jnp.sum(x_lo * x_lo, axis=-1, keepdims=True) + jnp.sum(x_hi * x_hi, axis=-1, keepdims=True)
    r = lax.rsqrt(ss / d + EPS)
    o_ref[:, 0:half] = x_lo * r * g_ref[:, 0:half]
    o_ref[:, half:d] = x_hi * r * g_ref[:, half:d]


def _final(x1, y4, w_tok, mod, final_g, seq):
    n, d = x1.shape
    tm = 256
    per_batch = seq // tm
    n_i = n // tm
    y_specs = [pl.BlockSpec((tm, d // 2), functools.partial(lambda i, k: (k * n_i + i, 0), k=k))
               for k in range(TOP_K)]
    return pl.pallas_call(
        _final_kernel,
        grid=(n_i,),
        in_specs=[pl.BlockSpec((tm, d), lambda i: (i, 0))] + y_specs + [
                  pl.BlockSpec((tm, TOP_K), lambda i: (i, 0)),
                  pl.BlockSpec((1, 6, d), lambda i: (i // per_batch, 0, 0)),
                  pl.BlockSpec((1, d), lambda i: (0, 0))],
        out_specs=pl.BlockSpec((tm, d), lambda i: (i, 0)),
        out_shape=jax.ShapeDtypeStruct((n, d), F32),
        compiler_params=_tc_params(("arbitrary",)),
        name="final",
    )(x1, y4, y4, y4, y4, w_tok, mod, final_g.reshape(1, d))


def kernel(x, c, norm1_g, w_mod, b_mod, w_in, conv_w, w_out, norm2_g, w_router, b_router,
           w_gate_up, b_gate_up, w_down, b_down, final_g):
    b, seq, d = x.shape
    n = b * seq
    x2 = x.reshape(n, d)
    mod = _modulation(c, w_mod[0], b_mod[0])
    proj = _in_projection(x2, mod, norm1_g[0], w_in[0].astype(BF16), seq)
    x1, h2, logits = _mixer(proj, x2, mod, conv_w[0], w_out[0].astype(BF16), norm2_g[0],
                            w_router[0], b_router[0], seq)
    pos, w_top, counts = _route(logits.T)
    slot_expert, slot_rows, n_used = _slot_tables(counts[:, 0].astype(jnp.int32), n)
    xs = _dispatch_rows(h2, pos)
    ys = _experts(xs, slot_expert, slot_rows, n_used[0], w_gate_up[0], b_gate_up[0], w_down[0], b_down[0])
    y4 = _combine_rows(ys, pos.reshape(TOP_K * n))
    out = _final(x1, y4, w_top.T, mod, final_g, seq)
    return out.reshape(b, seq, d)
```

```python
import functools
import math

import jax
import jax.numpy as jnp
from jax import lax
from jax.experimental import pallas as pl
from jax.experimental.pallas import tpu as pltpu
from jax.experimental.pallas import tpu_sc as plsc

F32 = jnp.float32
BF16 = jnp.bfloat16

CHUNK = 64
N_HEADS = 8
HEAD_DIM = 128
D_RET = N_HEADS * HEAD_DIM
D_CONV = 1024
CONV_WIDTH = 3
N_EXPERTS = 32
TOP_K = 4
SWIGLU_LIMIT = 7.0
SWIGLU_ALPHA = 1.702
ROPE_BASE = 10000.0
EPS = 1e-6

VMEM_LIMIT_BYTES = 58 * 1024 * 1024
SC_WORKERS = 32
SC_ROWS_PER_COPY = 32

MIX_BLOCK = 256
ROW_BLOCK = 128
SLOT_ROWS = 9 * ROW_BLOCK
W_PANEL = 512


def _tc_params(sem):
    return pltpu.CompilerParams(dimension_semantics=sem, vmem_limit_bytes=VMEM_LIMIT_BYTES)


def _bf16_bits(x):
    b = lax.bitcast_convert_type(x, jnp.uint32)
    return (b + jnp.uint32(0x7FFF) + ((b >> 16) & jnp.uint32(1))) & jnp.uint32(0xFFFF0000)


def _pack_bf16_pair(lo, hi):
    return (_bf16_bits(lo) >> 16) | _bf16_bits(hi)


def _unpack_bf16_pair(p):
    lo = lax.bitcast_convert_type(p << 16, F32)
    hi = lax.bitcast_convert_type(p & jnp.uint32(0xFFFF0000), F32)
    return lo, hi


def _mod_kernel(c_ref, w_ref, b_ref, o_ref):
    c = c_ref[...]
    ca = (c * jax.nn.sigmoid(c)).astype(BF16)
    o_ref[...] = jnp.dot(ca, w_ref[...].astype(BF16), preferred_element_type=F32) + b_ref[...]


def _modulation(c, w_mod, b_mod):
    b, d = c.shape
    n = w_mod.shape[1]
    tn = 1024
    c8 = jnp.zeros((8, d), F32).at[:b].set(c)
    out = pl.pallas_call(
        _mod_kernel,
        grid=(n // tn,),
        in_specs=[pl.BlockSpec((8, d), lambda j: (0, 0)),
                  pl.BlockSpec((d, tn), lambda j: (0, j)),
                  pl.BlockSpec((1, tn), lambda j: (0, j))],
        out_specs=pl.BlockSpec((8, tn), lambda j: (0, j)),
        out_shape=jax.ShapeDtypeStruct((8, n), F32),
        compiler_params=_tc_params(("arbitrary",)),
        name="mod",
    )(c8, w_mod, b_mod.reshape(1, n))
    return out[:b].reshape(b, 6, d)


def _norm_mod(x, g, scale, shift):
    y = x * lax.rsqrt(jnp.mean(x * x, axis=-1, keepdims=True) + EPS) * g
    return y * (1.0 + scale) + shift


def _inproj_kernel(x_ref, mod_ref, g_ref, w_ref, o_ref, h_ref, *, rows):
    @pl.when(pl.program_id(1) == 0)
    def _():
        g = g_ref[...]
        scale = mod_ref[0, 1:2, :]
        shift = mod_ref[0, 0:1, :]

        def body(r, carry):
            r0 = pl.multiple_of(r * rows, rows)
            h = _norm_mod(x_ref[pl.ds(r0, rows), :], g, scale, shift)
            h_ref[pl.ds(r0, rows), :] = h.astype(BF16)
            return carry

        lax.fori_loop(0, x_ref.shape[0] // rows, body, 0)

    o_ref[...] = jnp.dot(h_ref[...], w_ref[...], preferred_element_type=F32).astype(o_ref.dtype)


def _in_projection(x2, mod, g1, w_in_bf, seq):
    n, d = x2.shape
    p = w_in_bf.shape[1]
    tm, tn = min(1024, seq), 1024
    per_batch = seq // tm
    return pl.pallas_call(
        functools.partial(_inproj_kernel, rows=128),
        grid=(n // tm, p // tn),
        in_specs=[pl.BlockSpec((tm, d), lambda i, j: (i, 0)),
                  pl.BlockSpec((1, 6, d), lambda i, j: (i // per_batch, 0, 0)),
                  pl.BlockSpec((1, d), lambda i, j: (0, 0)),
                  pl.BlockSpec((d, tn), lambda i, j: (0, j))],
        out_specs=pl.BlockSpec((tm, tn), lambda i, j: (i, j)),
        out_shape=jax.ShapeDtypeStruct((n, p), BF16),
        scratch_shapes=[pltpu.VMEM((tm, d), BF16)],
        compiler_params=_tc_params(("arbitrary", "arbitrary")),
        name="inproj",
    )(x2, mod, g1.reshape(1, d), w_in_bf)


def _mixer_tables(seq, blk):
    half = HEAD_DIM // 2
    freqs = ROPE_BASE ** (-jnp.arange(half, dtype=F32) / half)
    ang = jnp.arange(seq, dtype=F32)[:, None] * freqs[None, :]
    cos = jnp.concatenate([jnp.cos(ang), jnp.cos(ang)], axis=-1)
    sin = jnp.concatenate([-jnp.sin(ang), jnp.sin(ang)], axis=-1)
    log_gamma = jnp.log1p(-jnp.exp2(-5.0 - jnp.arange(N_HEADS, dtype=F32)))
    idx = jnp.arange(blk, dtype=F32)
    dist = jnp.abs(idx[:, None] - idx[None, :])
    ck = jnp.arange(blk) // CHUNK
    visible = (ck[None, :] <= ck[:, None]).astype(F32)
    dmask = jnp.exp(log_gamma[:, None, None] * dist) * visible
    ones = jnp.ones((1, 1, HEAD_DIM), F32)
    dq = jnp.exp(log_gamma[:, None] * (idx + 1.0)[None])[:, :, None] * ones
    dkv = jnp.exp(log_gamma[:, None] * (blk - 1 - idx)[None])[:, :, None] * ones
    dblk = jnp.exp(log_gamma * blk)[:, None, None] * ones
    return cos, sin, dmask, dq, dkv, dblk


def _mixer_kernel(proj_ref, x_ref, mod_ref, cos_ref, sin_ref, dmask_ref, dq_ref, dkv_ref, dblk_ref,
                  convw_ref, wout_ref, g2_ref, wr_ref, br_ref,
                  x1_ref, h2_ref, lg_ref,
                  s_ref, zbuf_ref, y_ref, *, per_batch, n_blocks):
    blk = x_ref.shape[0]
    i = pl.program_id(0)
    cur = i % 2

    @pl.when(jnp.minimum(i, n_blocks - 1) % per_batch == 0)
    def _():
        s_ref[...] = jnp.zeros_like(s_ref)
        zbuf_ref[0:8, :] = jnp.zeros((8, D_CONV), F32)

    @pl.when(i == 0)
    def _():
        y_ref[1] = jnp.zeros(y_ref.shape[1:], BF16)

    cos = cos_ref[...]
    sin = sin_ref[...]
    k_scale = HEAD_DIM ** -0.5
    nt = (((1,), (1,)), ((), ()))
    tn = (((0,), (0,)), ((), ()))
    for h in range(N_HEADS):
        c0 = h * HEAD_DIM
        q = proj_ref[:, c0:c0 + HEAD_DIM].astype(F32)
        k = proj_ref[:, D_RET + c0:D_RET + c0 + HEAD_DIM].astype(F32)
        v = proj_ref[:, 2 * D_RET + c0:2 * D_RET + c0 + HEAD_DIM]
        g = proj_ref[:, 3 * D_RET + c0:3 * D_RET + c0 + HEAD_DIM].astype(F32)
        qr = q * cos + pltpu.roll(q, HEAD_DIM // 2, 1) * sin
        kr = (k * cos + pltpu.roll(k, HEAD_DIM // 2, 1) * sin) * k_scale
        qb = qr.astype(BF16)
        kb = kr.astype(BF16)
        scores = lax.dot_general(qb, kb, nt, preferred_element_type=F32) * dmask_ref[h]
        intra = jnp.dot(scores.astype(BF16), v, preferred_element_type=F32)
        state = s_ref[h]
        cross = jnp.dot(qb, state.astype(BF16), preferred_element_type=F32) * dq_ref[h]
        kd = (kr * dkv_ref[h]).astype(BF16)
        kv = lax.dot_general(kd, v, tn, preferred_element_type=F32)
        s_ref[h] = dblk_ref[h] * state + kv
        o = intra + cross
        mu = jnp.mean(o, axis=-1, keepdims=True)
        dev = o - mu
        var = jnp.mean(dev * dev, axis=-1, keepdims=True)
        on = dev * lax.rsqrt(var + EPS)
        y_ref[cur, :, c0:c0 + HEAD_DIM] = (g * jax.nn.sigmoid(g) * on).astype(BF16)

    cw = 256
    base = 4 * D_RET
    for cb in range(D_CONV // cw):
        lo = cb * cw
        cg = proj_ref[:, base + D_CONV + lo:base + D_CONV + lo + cw].astype(F32)
        u = proj_ref[:, base + 2 * D_CONV + lo:base + 2 * D_CONV + lo + cw].astype(F32)
        zbuf_ref[8:blk + 8, lo:lo + cw] = cg * u
    for cb in range(D_CONV // cw):
        lo = cb * cw
        z0 = zbuf_ref[8:blk + 8, lo:lo + cw]
        z1 = zbuf_ref[7:blk + 7, lo:lo + cw]
        z2 = zbuf_ref[6:blk + 6, lo:lo + cw]
        z = (convw_ref[2:3, lo:lo + cw] * z0 + convw_ref[1:2, lo:lo + cw] * z1
             + convw_ref[0:1, lo:lo + cw] * z2)
        bg = proj_ref[:, base + lo:base + lo + cw].astype(F32)
        y_ref[cur, :, D_RET + lo:D_RET + lo + cw] = (bg * z).astype(BF16)
    zbuf_ref[0:8, :] = zbuf_ref[blk:blk + 8, :]

    mix = jnp.dot(y_ref[1 - cur], wout_ref[...], preferred_element_type=F32)
    x1 = x_ref[...] + mod_ref[0, 2:3, :] * mix
    x1_ref[...] = x1
    h2 = _norm_mod(x1, g2_ref[...], mod_ref[0, 4:5, :], mod_ref[0, 3:4, :])
    half = h2.shape[1] // 2
    h2_ref[...] = _pack_bf16_pair(h2[:, :half], h2[:, half:])
    w2 = wr_ref[...]
    w_hi = w2.astype(BF16)
    w_lo = (w2 - w_hi.astype(F32)).astype(BF16)
    w_split = jnp.where(lax.broadcasted_iota(jnp.int32, w2.shape, 1) < N_EXPERTS, w_hi, w_lo)
    h_hi = h2.astype(BF16)
    h_lo = (h2 - h_hi.astype(F32)).astype(BF16)
    r = (jnp.dot(h_hi, w_split, preferred_element_type=F32)
         + jnp.dot(h_lo, w_split, preferred_element_type=F32))
    lg_ref[...] = r[:, 0:N_EXPERTS] + r[:, N_EXPERTS:2 * N_EXPERTS] + br_ref[...]


def _mixer(proj, x2, mod, conv_w, w_out_bf, g2, w_router, b_router, seq):
    n, d = x2.shape
    blk = MIX_BLOCK
    per_batch = seq // blk
    cos, sin, dmask, dq, dkv, dblk = _mixer_tables(seq, blk)
    const2 = lambda i: (0, 0)
    const3 = lambda i: (0, 0, 0)
    n_blocks = n // blk
    mixed = lambda i: jnp.minimum(i, n_blocks - 1)
    done = lambda i: jnp.maximum(i - 1, 0)
    return pl.pallas_call(
        functools.partial(_mixer_kernel, per_batch=per_batch, n_blocks=n_blocks),
        grid=(n_blocks + 1,),
        in_specs=[pl.BlockSpec((blk, proj.shape[1]), lambda i: (mixed(i), 0)),
                  pl.BlockSpec((blk, d), lambda i: (done(i), 0)),
                  pl.BlockSpec((1, 6, d), lambda i: (done(i) // per_batch, 0, 0)),
                  pl.BlockSpec((blk, HEAD_DIM), lambda i: (mixed(i) % per_batch, 0)),
                  pl.BlockSpec((blk, HEAD_DIM), lambda i: (mixed(i) % per_batch, 0)),
                  pl.BlockSpec((N_HEADS, blk, blk), const3),
                  pl.BlockSpec((N_HEADS, blk, HEAD_DIM), const3),
                  pl.BlockSpec((N_HEADS, blk, HEAD_DIM), const3),
                  pl.BlockSpec((N_HEADS, 1, HEAD_DIM), const3),
                  pl.BlockSpec((CONV_WIDTH, D_CONV), const2),
                  pl.BlockSpec((D_RET + D_CONV, d), const2),
                  pl.BlockSpec((1, d), const2),
                  pl.BlockSpec((d, 2 * N_EXPERTS), const2),
                  pl.BlockSpec((1, N_EXPERTS), const2)],
        out_specs=[pl.BlockSpec((blk, d), lambda i: (done(i), 0)),
                   pl.BlockSpec((blk, d // 2), lambda i: (done(i), 0)),
                   pl.BlockSpec((blk, N_EXPERTS), lambda i: (done(i), 0))],
        out_shape=[jax.ShapeDtypeStruct((n, d), F32),
                   jax.ShapeDtypeStruct((n, d // 2), jnp.uint32),
                   jax.ShapeDtypeStruct((n, N_EXPERTS), F32)],
        scratch_shapes=[pltpu.VMEM((N_HEADS, HEAD_DIM, HEAD_DIM), F32),
                        pltpu.VMEM((blk + 8, D_CONV), F32),
                        pltpu.VMEM((2, blk, D_RET + D_CONV), BF16)],
        compiler_params=_tc_params(("arbitrary",)),
        name="mixer",
    )(proj, x2, mod, cos, sin, dmask, dq, dkv, dblk, conv_w, w_out_bf, g2.reshape(1, d),
      jnp.concatenate([w_router, w_router], axis=1), b_router.reshape(1, N_EXPERTS))


def _top_k_block(lg_ref, tri_ref, run_ref):
    n_e, bw = lg_ref.shape

    @pl.when(pl.program_id(0) == 0)
    def _():
        run_ref[...] = jnp.zeros_like(run_ref)

    l = lg_ref[...]
    eio = lax.broadcasted_iota(jnp.int32, (n_e, bw), 0)
    vals, hots = [], []
    for _ in range(TOP_K):
        m = jnp.max(l, axis=0, keepdims=True)
        idx = jnp.min(jnp.where(l == m, eio, n_e), axis=0, keepdims=True)
        hot = eio == idx
        vals.append(m)
        hots.append(hot)
        l = jnp.where(hot, -jnp.inf, l)
    sel = sum(jnp.where(hot, 1.0, 0.0) for hot in hots)
    incl = jnp.dot(sel.astype(BF16), tri_ref[...], preferred_element_type=F32)
    run = run_ref[:, 0:1]
    rank = incl - sel + run
    run_ref[...] = jnp.broadcast_to(run + incl[:, bw - 1:bw], run_ref.shape)
    return vals, hots, rank


def _count_kernel(lg_ref, tri_ref, cnt_ref, run_ref):
    _top_k_block(lg_ref, tri_ref, run_ref)
    cnt_ref[...] = run_ref[...]


def _route_kernel(lg_ref, tri_ref, low_ref, tot_ref, pos_ref, w_ref, run_ref, *, max_slots):
    vals, hots, rank = _top_k_block(lg_ref, tri_ref, run_ref)
    ex = [jnp.exp(v - vals[0]) for v in vals]
    den = ex[0] + ex[1] + ex[2] + ex[3]
    for k in range(TOP_K):
        w_ref[k:k + 1, :] = ex[k] / den
    tot = tot_ref[...]
    n_slots = sum(jnp.where(tot > float(m * SLOT_ROWS), 1.0, 0.0) for m in range(max_slots))
    start = jnp.dot(low_ref[...], n_slots.astype(BF16), preferred_element_type=F32)
    dest = start[:, 0:1] * float(SLOT_ROWS) + rank
    for k in range(TOP_K):
        pos_ref[k:k + 1, :] = jnp.sum(jnp.where(hots[k], dest, 0.0), axis=0,
                                      keepdims=True).astype(jnp.int32)


def _route(logits_t):
    n_e, n = logits_t.shape
    bw = 512
    max_slots = -(-n // SLOT_ROWS)
    tri = (jnp.arange(bw)[:, None] <= jnp.arange(bw)[None, :]).astype(BF16)
    low = (jnp.arange(n_e)[None, :] < jnp.arange(n_e)[:, None]).astype(BF16)
    lg_spec = pl.BlockSpec((n_e, bw), lambda c: (0, c))
    tri_spec = pl.BlockSpec((bw, bw), lambda c: (0, 0))
    cnt_spec = pl.BlockSpec((n_e, 128), lambda c: (0, 0))
    run = pltpu.VMEM((n_e, 128), F32)
    counts = pl.pallas_call(
        _count_kernel,
        grid=(n // bw,),
        in_specs=[lg_spec, tri_spec],
        out_specs=cnt_spec,
        out_shape=jax.ShapeDtypeStruct((n_e, 128), F32),
        scratch_shapes=[run],
        compiler_params=_tc_params(("arbitrary",)),
        name="route_count",
    )(logits_t, tri)
    pos, w_top = pl.pallas_call(
        functools.partial(_route_kernel, max_slots=max_slots),
        grid=(n // bw,),
        in_specs=[lg_spec, tri_spec, pl.BlockSpec((n_e, n_e), lambda c: (0, 0)), cnt_spec],
        out_specs=[pl.BlockSpec((TOP_K, bw), lambda c: (0, c)),
                   pl.BlockSpec((TOP_K, bw), lambda c: (0, c))],
        out_shape=[jax.ShapeDtypeStruct((TOP_K, n), jnp.int32),
                   jax.ShapeDtypeStruct((TOP_K, n), F32)],
        scratch_shapes=[run],
        compiler_params=_tc_params(("arbitrary",)),
        name="route",
    )(logits_t, tri, low, counts)
    return pos, w_top, counts


def _sc_mesh():
    return plsc.VectorSubcoreMesh(core_axis_name="c", subcore_axis_name="s")


def _sc_worker_id():
    return lax.axis_index("s") * lax.axis_size("c") + lax.axis_index("c")


def _dispatch_rows(h2, pos):
    n, d = h2.shape
    ch = SC_ROWS_PER_COPY
    per_w = n // SC_WORKERS
    n_ch = per_w // ch
    total_rows = _max_slots(n) * SLOT_ROWS
    idx = pos.reshape(TOP_K, SC_WORKERS, n_ch, ch).transpose(1, 2, 0, 3).reshape(
        SC_WORKERS, n_ch * TOP_K, ch)

    @functools.partial(
        pl.kernel, mesh=_sc_mesh(),
        out_type=jax.ShapeDtypeStruct((total_rows, d), h2.dtype),
        scratch_types=[pltpu.VMEM((n_ch * TOP_K, ch), jnp.int32),
                       pltpu.VMEM((ch, d), h2.dtype),
                       pltpu.SemaphoreType.DMA])
    def scatter(x_hbm, idx_hbm, o_hbm, idx_v, rows_v, sem):
        wid = _sc_worker_id()
        pltpu.sync_copy(idx_hbm.at[wid], idx_v)

        @pl.loop(0, n_ch)
        def _(c):
            r0 = pl.multiple_of(wid * per_w + c * ch, ch)
            pltpu.sync_copy(x_hbm.at[pl.ds(r0, ch)], rows_v)
            for k in range(TOP_K):
                pltpu.async_copy(rows_v, o_hbm.at[idx_v.at[c * TOP_K + k]], sem).wait()

    return scatter(h2, idx)


def _combine_rows(ys, pos_flat):
    n = pos_flat.shape[0]
    d = ys.shape[1]
    ch = SC_ROWS_PER_COPY
    per_w = n // SC_WORKERS
    n_ch = per_w // ch
    idx = pos_flat.reshape(SC_WORKERS, n_ch, ch)

    @functools.partial(
        pl.kernel, mesh=_sc_mesh(),
        out_type=jax.ShapeDtypeStruct((n, d), ys.dtype),
        scratch_types=[pltpu.VMEM((n_ch, ch), jnp.int32),
                       pltpu.VMEM((ch, d), ys.dtype),
                       pltpu.SemaphoreType.DMA])
    def gather(y_hbm, idx_hbm, o_hbm, idx_v, rows_v, sem):
        wid = _sc_worker_id()
        pltpu.sync_copy(idx_hbm.at[wid], idx_v)

        @pl.loop(0, n_ch)
        def _(c):
            r0 = pl.multiple_of(wid * per_w + c * ch, ch)
            pltpu.async_copy(y_hbm.at[idx_v.at[c]], rows_v, sem).wait()
            pltpu.sync_copy(rows_v, o_hbm.at[pl.ds(r0, ch)])

    return gather(ys, idx)


def _max_slots(n_tokens):
    return N_EXPERTS + (n_tokens * TOP_K) // SLOT_ROWS


def _expert_kernel(se_ref, sr_ref,
                   xs_ref, wgu_hbm, wd_hbm, bgu_ref, bd_ref,
                   ys_ref,
                   xb_ref, act_ref, wbuf_ref, sem_ref, *, n_a, n_b):
    s = pl.program_id(0)
    rows = sr_ref[s]
    n_blk = (rows + ROW_BLOCK - 1) // ROW_BLOCK
    tw = wbuf_ref.shape[2] // 2
    d_ff = act_ref.shape[1]
    half_d = ys_ref.shape[1]

    def panel_copy(src, buf, p):
        return pltpu.make_async_copy(src, wbuf_ref.at[buf, :, pl.ds(p * tw, tw)], sem_ref.at[buf, p])

    def gate_up_panels(e, t, buf):
        c0 = pl.multiple_of(t * tw, tw)
        return (panel_copy(wgu_hbm.at[e, :, pl.ds(c0, tw)], buf, 0),
                panel_copy(wgu_hbm.at[e, :, pl.ds(d_ff + c0, tw)], buf, 1))

    def down_panels(e, u, buf):
        c0 = pl.multiple_of(u * tw, tw)
        return (panel_copy(wd_hbm.at[e, :, pl.ds(c0, tw)], buf, 0),
                panel_copy(wd_hbm.at[e, :, pl.ds(half_d + c0, tw)], buf, 1))

    def start(panels):
        for cp in panels:
            cp.start()

    def wait(panels):
        for cp in panels:
            cp.wait()

    expert = se_ref[s]

    def for_row_blocks(fn):
        n_quads = n_blk // 4

        def body(rb, carry):
            fn(pl.multiple_of(rb * 4 * ROW_BLOCK, 4 * ROW_BLOCK), 4 * ROW_BLOCK)
            return carry

        lax.fori_loop(0, n_quads, body, 0)

        @pl.when((n_blk // 2) % 2 == 1)
        def _():
            fn(pl.multiple_of(n_quads * 4 * ROW_BLOCK, 4 * ROW_BLOCK), 2 * ROW_BLOCK)

        @pl.when(n_blk % 2 == 1)
        def _():
            fn(pl.multiple_of((n_blk // 2) * 2 * ROW_BLOCK, 2 * ROW_BLOCK), ROW_BLOCK)

    @pl.when(s == 0)
    def _():
        start(gate_up_panels(expert, 0, 0))

    def unpack(rb, carry):
        r0 = pl.multiple_of(rb * ROW_BLOCK, ROW_BLOCK)
        keep = (r0 + lax.broadcasted_iota(jnp.int32, (ROW_BLOCK, 1), 0)) < rows
        lo, hi = _unpack_bf16_pair(xs_ref[pl.ds(r0, ROW_BLOCK), :])
        xb_ref[pl.ds(r0, ROW_BLOCK), 0:half_d] = jnp.where(keep, lo, 0.0).astype(BF16)
        xb_ref[pl.ds(r0, ROW_BLOCK), half_d:2 * half_d] = jnp.where(keep, hi, 0.0).astype(BF16)
        return carry

    lax.fori_loop(0, n_blk, unpack, 0)

    def gate_up_tile(t, carry):
        buf = t % 2
        wait(gate_up_panels(expert, t, buf))

        @pl.when(t + 1 < n_a)
        def _():
            start(gate_up_panels(expert, t + 1, 1 - buf))

        @pl.when(t + 1 == n_a)
        def _():
            start(down_panels(expert, 0, 1 - buf))

        col = pl.multiple_of(t * tw, tw)
        b_gate = bgu_ref[0, :, pl.ds(col, tw)]
        b_up = bgu_ref[0, :, pl.ds(d_ff + col, tw)]

        def gate_up(r0, m):
            gu = jnp.dot(xb_ref[pl.ds(r0, m), :], wbuf_ref[buf].astype(BF16), preferred_element_type=F32)
            gate = jnp.minimum(gu[:, 0:tw] + b_gate, SWIGLU_LIMIT)
            up = jnp.clip(gu[:, tw:2 * tw] + b_up, -SWIGLU_LIMIT, SWIGLU_LIMIT)
            act = (up + 1.0) * (gate * jax.nn.sigmoid(SWIGLU_ALPHA * gate))
            act_ref[pl.ds(r0, m), pl.ds(col, tw)] = act.astype(BF16)

        for_row_blocks(gate_up)
        return carry

    lax.fori_loop(0, n_a, gate_up_tile, 0)

    def down_tile(u, carry):
        buf = (n_a + u) % 2
        wait(down_panels(expert, u, buf))

        @pl.when(u + 1 < n_b)
        def _():
            start(down_panels(expert, u + 1, 1 - buf))

        @pl.when((u + 1 == n_b) & (s + 1 < pl.num_programs(0)))
        def _():
            start(gate_up_panels(se_ref[s + 1], 0, 1 - buf))

        col = pl.multiple_of(u * tw, tw)
        b_lo = bd_ref[0, :, pl.ds(col, tw)]
        b_hi = bd_ref[0, :, pl.ds(half_d + col, tw)]

        def down(r0, m):
            y = jnp.dot(act_ref[pl.ds(r0, m), :], wbuf_ref[buf].astype(BF16), preferred_element_type=F32)
            ys_ref[pl.ds(r0, m), pl.ds(col, tw)] = _pack_bf16_pair(y[:, 0:tw] + b_lo, y[:, tw:2 * tw] + b_hi)

        for_row_blocks(down)
        return carry

    lax.fori_loop(0, n_b, down_tile, 0)

    def fill(rb, carry):
        r0 = pl.multiple_of(rb * ROW_BLOCK, ROW_BLOCK)
        ys_ref[pl.ds(r0, ROW_BLOCK), :] = jnp.zeros((ROW_BLOCK, half_d), jnp.uint32)
        return carry

    lax.fori_loop(n_blk, SLOT_ROWS // ROW_BLOCK, fill, 0)


def _experts(xs, slot_expert, slot_rows, n_used, w_gate_up, b_gate_up, w_down, b_down):
    n_e, d, two_ff = w_gate_up.shape
    d_ff = two_ff // 2
    assert d == d_ff, "the weight ring holds (rows, W_PANEL) panels of both projections"
    tw = W_PANEL
    n_a, n_b = d_ff // tw, (d // 2) // tw
    assert (n_a + n_b) % 2 == 0, "ring slot parity must repeat from one expert slot to the next"
    total_rows = xs.shape[0]

    grid_spec = pltpu.PrefetchScalarGridSpec(
        num_scalar_prefetch=2,
        grid=(n_used,),
        in_specs=[
            pl.BlockSpec((SLOT_ROWS, d // 2), lambda s, se, sr: (s, 0)),
            pl.BlockSpec(memory_space=pl.ANY),
            pl.BlockSpec(memory_space=pl.ANY),
            pl.BlockSpec((1, 1, two_ff), lambda s, se, sr: (se[s], 0, 0)),
            pl.BlockSpec((1, 1, d), lambda s, se, sr: (se[s], 0, 0)),
        ],
        out_specs=pl.BlockSpec((SLOT_ROWS, d // 2), lambda s, se, sr: (s, 0)),
        scratch_shapes=[pltpu.VMEM((SLOT_ROWS, d), BF16),
                        pltpu.VMEM((SLOT_ROWS, d_ff), BF16),
                        pltpu.VMEM((2, d, 2 * tw), F32),
                        pltpu.SemaphoreType.DMA((2, 2))],
    )
    return pl.pallas_call(
        functools.partial(_expert_kernel, n_a=n_a, n_b=n_b),
        grid_spec=grid_spec,
        out_shape=jax.ShapeDtypeStruct((total_rows, d // 2), jnp.uint32),
        compiler_params=_tc_params(("arbitrary",)),
        name="experts",
    )(slot_expert, slot_rows, xs, w_gate_up, w_down,
      b_gate_up.reshape(n_e, 1, two_ff), b_down.reshape(n_e, 1, d))


def _slot_tables(counts, n_tokens):
    max_slots = _max_slots(n_tokens)
    n_slots = (counts + SLOT_ROWS - 1) // SLOT_ROWS
    slot_end = jnp.cumsum(n_slots)
    slot_start = slot_end - n_slots
    n_used = slot_end[-1]
    sid = jnp.arange(max_slots, dtype=jnp.int32)
    expert = jnp.minimum(jnp.sum(sid[:, None] >= slot_end[None, :], axis=1), N_EXPERTS - 1).astype(jnp.int32)
    local = sid - slot_start[expert]
    rows = jnp.clip(counts[expert] - local * SLOT_ROWS, 0, SLOT_ROWS)
    rows = jnp.where(sid < n_used, rows, 0).astype(jnp.int32)
    return expert, rows, n_used.reshape(1).astype(jnp.int32)


def _final_kernel(x1_ref, y0_ref, y1_ref, y2_ref, y3_ref, w_ref, mod_ref, g_ref, o_ref):
    half = y0_ref.shape[1]
    d = 2 * half
    moe_lo = moe_hi = None
    for k, y_ref in enumerate((y0_ref, y1_ref, y2_ref, y3_ref)):
        lo, hi = _unpack_bf16_pair(y_ref[...])
        wk = w_ref[:, k:k + 1]
        moe_lo = wk * lo if k == 0 else moe_lo + wk * lo
        moe_hi = wk * hi if k == 0 else moe_hi + wk * hi
    x_lo = x1_ref[:, 0:half] + mod_ref[0, 5:6, 0:half] * moe_lo
    x_hi = x1_ref[:, half:d] + mod_ref[0, 5:6, half:d] * moe_hi
    ss = jnp.sum(x_lo * x_lo, axis=-1, keepdims=True) + jnp.sum(x_hi * x_hi, axis=-1, keepdims=True)
    r = lax.rsqrt(ss / d + EPS)
    o_ref[:, 0:half] = x_lo * r * g_ref[:, 0:half]
    o_ref[:, half:d] = x_hi * r * g_ref[:, half:d]


def _final(x1, y4, w_tok, mod, final_g, seq):
    n, d = x1.shape
    tm = 256
    per_batch = seq // tm
    n_i = n // tm
    y_specs = [pl.BlockSpec((tm, d // 2), functools.partial(lambda i, k: (k * n_i + i, 0), k=k))
               for k in range(TOP_K)]
    return pl.pallas_call(
        _final_kernel,
        grid=(n_i,),
        in_specs=[pl.BlockSpec((tm, d), lambda i: (i, 0))] + y_specs + [
                  pl.BlockSpec((tm, TOP_K), lambda i: (i, 0)),
                  pl.BlockSpec((1, 6, d), lambda i: (i // per_batch, 0, 0)),
                  pl.BlockSpec((1, d), lambda i: (0, 0))],
        out_specs=pl.BlockSpec((tm, d), lambda i: (i, 0)),
        out_shape=jax.ShapeDtypeStruct((n, d), F32),
        compiler_params=_tc_params(("arbitrary",)),
        name="final",
    )(x1, y4, y4, y4, y4, w_tok, mod, final_g.reshape(1, d))


def kernel(x, c, norm1_g, w_mod, b_mod, w_in, conv_w, w_out, norm2_g, w_router, b_router,
           w_gate_up, b_gate_up, w_down, b_down, final_g):
    b, seq, d = x.shape
    n = b * seq
    x2 = x.reshape(n, d)
    mod = _modulation(c, w_mod[0], b_mod[0])
    proj = _in_projection(x2, mod, norm1_g[0], w_in[0].astype(BF16), seq)
    x1, h2, logits = _mixer(proj, x2, mod, conv_w[0], w_out[0].astype(BF16), norm2_g[0],
                            w_router[0], b_router[0], seq)
    pos, w_top, counts = _route(logits.T)
    slot_expert, slot_rows, n_used = _slot_tables(counts[:, 0].astype(jnp.int32), n)
    xs = _dispatch_rows(h2, pos)
    ys = _experts(xs, slot_expert, slot_rows, n_used[0], w_gate_up[0], b_gate_up[0], w_down[0], b_down[0])
    y4 = _combine_rows(ys, pos.reshape(TOP_K * n))
    out = _final(x1, y4, w_top.T, mod, final_g, seq)
    return out.reshape(b, seq, d)
```

```python
import functools
import math

import jax
import jax.numpy as jnp
from jax import lax
from jax.experimental import pallas as pl
from jax.experimental.pallas import tpu as pltpu
from jax.experimental.pallas import tpu_sc as plsc

F32 = jnp.float32
BF16 = jnp.bfloat16

CHUNK = 64
N_HEADS = 8
HEAD_DIM = 128
D_RET = N_HEADS * HEAD_DIM
D_CONV = 1024
CONV_WIDTH = 3
N_EXPERTS = 32
TOP_K = 4
SWIGLU_LIMIT = 7.0
SWIGLU_ALPHA = 1.702
ROPE_BASE = 10000.0
EPS = 1e-6

VMEM_LIMIT_BYTES = 58 * 1024 * 1024
SC_WORKERS = 32
SC_ROWS_PER_COPY = 32

MIX_BLOCK = 256
ROW_BLOCK = 128
SLOT_ROWS = 9 * ROW_BLOCK
W_PANEL = 512
RING_AHEAD = 2


def _tc_params(sem):
    return pltpu.CompilerParams(dimension_semantics=sem, vmem_limit_bytes=VMEM_LIMIT_BYTES)


def _bf16_bits(x):
    b = lax.bitcast_convert_type(x, jnp.uint32)
    return (b + jnp.uint32(0x7FFF) + ((b >> 16) & jnp.uint32(1))) & jnp.uint32(0xFFFF0000)


def _pack_bf16_pair(lo, hi):
    return (_bf16_bits(lo) >> 16) | _bf16_bits(hi)


def _unpack_bf16_pair(p):
    lo = lax.bitcast_convert_type(p << 16, F32)
    hi = lax.bitcast_convert_type(p & jnp.uint32(0xFFFF0000), F32)
    return lo, hi


def _mod_kernel(c_ref, w_ref, b_ref, o_ref):
    c = c_ref[...]
    ca = (c * jax.nn.sigmoid(c)).astype(BF16)
    o_ref[...] = jnp.dot(ca, w_ref[...].astype(BF16), preferred_element_type=F32) + b_ref[...]


def _modulation(c, w_mod, b_mod):
    b, d = c.shape
    n = w_mod.shape[1]
    tn = 1024
    c8 = jnp.zeros((8, d), F32).at[:b].set(c)
    out = pl.pallas_call(
        _mod_kernel,
        grid=(n // tn,),
        in_specs=[pl.BlockSpec((8, d), lambda j: (0, 0)),
                  pl.BlockSpec((d, tn), lambda j: (0, j)),
                  pl.BlockSpec((1, tn), lambda j: (0, j))],
        out_specs=pl.BlockSpec((8, tn), lambda j: (0, j)),
        out_shape=jax.ShapeDtypeStruct((8, n), F32),
        compiler_params=_tc_params(("arbitrary",)),
        name="mod",
    )(c8, w_mod, b_mod.reshape(1, n))
    return out[:b].reshape(b, 6, d)


def _norm_mod(x, g, scale, shift):
    y = x * lax.rsqrt(jnp.mean(x * x, axis=-1, keepdims=True) + EPS) * g
    return y * (1.0 + scale) + shift


def _inproj_kernel(x_ref, mod_ref, g_ref, w_ref, o_ref, h_ref, *, rows):
    @pl.when(pl.program_id(1) == 0)
    def _():
        g = g_ref[...]
        scale = mod_ref[0, 1:2, :]
        shift = mod_ref[0, 0:1, :]

        def body(r, carry):
            r0 = pl.multiple_of(r * rows, rows)
            h = _norm_mod(x_ref[pl.ds(r0, rows), :], g, scale, shift)
            h_ref[pl.ds(r0, rows), :] = h.astype(BF16)
            return carry

        lax.fori_loop(0, x_ref.shape[0] // rows, body, 0)

    o_ref[...] = jnp.dot(h_ref[...], w_ref[...], preferred_element_type=F32).astype(o_ref.dtype)


def _in_projection(x2, mod, g1, w_in_bf, seq):
    n, d = x2.shape
    p = w_in_bf.shape[1]
    tm, tn = min(1024, seq), 1024
    per_batch = seq // tm
    return pl.pallas_call(
        functools.partial(_inproj_kernel, rows=128),
        grid=(n // tm, p // tn),
        in_specs=[pl.BlockSpec((tm, d), lambda i, j: (i, 0)),
                  pl.BlockSpec((1, 6, d), lambda i, j: (i // per_batch, 0, 0)),
                  pl.BlockSpec((1, d), lambda i, j: (0, 0)),
                  pl.BlockSpec((d, tn), lambda i, j: (0, j))],
        out_specs=pl.BlockSpec((tm, tn), lambda i, j: (i, j)),
        out_shape=jax.ShapeDtypeStruct((n, p), BF16),
        scratch_shapes=[pltpu.VMEM((tm, d), BF16)],
        compiler_params=_tc_params(("arbitrary", "arbitrary")),
        name="inproj",
    )(x2, mod, g1.reshape(1, d), w_in_bf)


def _mixer_tables(seq, blk):
    half = HEAD_DIM // 2
    freqs = ROPE_BASE ** (-jnp.arange(half, dtype=F32) / half)
    ang = jnp.arange(seq, dtype=F32)[:, None] * freqs[None, :]
    cos = jnp.concatenate([jnp.cos(ang), jnp.cos(ang)], axis=-1)
    sin = jnp.concatenate([-jnp.sin(ang), jnp.sin(ang)], axis=-1)
    log_gamma = jnp.log1p(-jnp.exp2(-5.0 - jnp.arange(N_HEADS, dtype=F32)))
    idx = jnp.arange(blk, dtype=F32)
    dist = jnp.abs(idx[:, None] - idx[None, :])
    ck = jnp.arange(blk) // CHUNK
    visible = (ck[None, :] <= ck[:, None]).astype(F32)
    dmask = jnp.exp(log_gamma[:, None, None] * dist) * visible
    ones = jnp.ones((1, 1, HEAD_DIM), F32)
    dq = jnp.exp(log_gamma[:, None] * (idx + 1.0)[None])[:, :, None] * ones
    dkv = jnp.exp(log_gamma[:, None] * (blk - 1 - idx)[None])[:, :, None] * ones
    dblk = jnp.exp(log_gamma * blk)[:, None, None] * ones
    return cos, sin, dmask, dq, dkv, dblk


def _mixer_kernel(proj_ref, x_ref, mod_ref, cos_ref, sin_ref, dmask_ref, dq_ref, dkv_ref, dblk_ref,
                  convw_ref, wout_ref, g2_ref, wr_ref, br_ref,
                  x1_ref, h2_ref, lg_ref,
                  s_ref, zbuf_ref, y_ref, *, per_batch, n_blocks):
    blk = x_ref.shape[0]
    i = pl.program_id(0)
    cur = i % 2

    @pl.when(jnp.minimum(i, n_blocks - 1) % per_batch == 0)
    def _():
        s_ref[...] = jnp.zeros_like(s_ref)
        zbuf_ref[0:8, :] = jnp.zeros((8, D_CONV), F32)

    @pl.when(i == 0)
    def _():
        y_ref[1] = jnp.zeros(y_ref.shape[1:], BF16)

    cos = cos_ref[...]
    sin = sin_ref[...]
    k_scale = HEAD_DIM ** -0.5
    nt = (((1,), (1,)), ((), ()))
    tn = (((0,), (0,)), ((), ()))
    for h in range(N_HEADS):
        c0 = h * HEAD_DIM
        q = proj_ref[:, c0:c0 + HEAD_DIM].astype(F32)
        k = proj_ref[:, D_RET + c0:D_RET + c0 + HEAD_DIM].astype(F32)
        v = proj_ref[:, 2 * D_RET + c0:2 * D_RET + c0 + HEAD_DIM]
        g = proj_ref[:, 3 * D_RET + c0:3 * D_RET + c0 + HEAD_DIM].astype(F32)
        qr = q * cos + pltpu.roll(q, HEAD_DIM // 2, 1) * sin
        kr = (k * cos + pltpu.roll(k, HEAD_DIM // 2, 1) * sin) * k_scale
        qb = qr.astype(BF16)
        kb = kr.astype(BF16)
        scores = lax.dot_general(qb, kb, nt, preferred_element_type=F32) * dmask_ref[h]
        intra = jnp.dot(scores.astype(BF16), v, preferred_element_type=F32)
        state = s_ref[h]
        cross = jnp.dot(qb, state.astype(BF16), preferred_element_type=F32) * dq_ref[h]
        kd = (kr * dkv_ref[h]).astype(BF16)
        kv = lax.dot_general(kd, v, tn, preferred_element_type=F32)
        s_ref[h] = dblk_ref[h] * state + kv
        o = intra + cross
        mu = jnp.mean(o, axis=-1, keepdims=True)
        dev = o - mu
        var = jnp.mean(dev * dev, axis=-1, keepdims=True)
        on = dev * lax.rsqrt(var + EPS)
        y_ref[cur, :, c0:c0 + HEAD_DIM] = (g * jax.nn.sigmoid(g) * on).astype(BF16)

    cw = 256
    base = 4 * D_RET
    for cb in range(D_CONV // cw):
        lo = cb * cw
        cg = proj_ref[:, base + D_CONV + lo:base + D_CONV + lo + cw].astype(F32)
        u = proj_ref[:, base + 2 * D_CONV + lo:base + 2 * D_CONV + lo + cw].astype(F32)
        zbuf_ref[8:blk + 8, lo:lo + cw] = cg * u
    for cb in range(D_CONV // cw):
        lo = cb * cw
        z0 = zbuf_ref[8:blk + 8, lo:lo + cw]
        z1 = zbuf_ref[7:blk + 7, lo:lo + cw]
        z2 = zbuf_ref[6:blk + 6, lo:lo + cw]
        z = (convw_ref[2:3, lo:lo + cw] * z0 + convw_ref[1:2, lo:lo + cw] * z1
             + convw_ref[0:1, lo:lo + cw] * z2)
        bg = proj_ref[:, base + lo:base + lo + cw].astype(F32)
        y_ref[cur, :, D_RET + lo:D_RET + lo + cw] = (bg * z).astype(BF16)
    zbuf_ref[0:8, :] = zbuf_ref[blk:blk + 8, :]

    mix = jnp.dot(y_ref[1 - cur], wout_ref[...], preferred_element_type=F32)
    x1 = x_ref[...] + mod_ref[0, 2:3, :] * mix
    x1_ref[...] = x1
    h2 = _norm_mod(x1, g2_ref[...], mod_ref[0, 4:5, :], mod_ref[0, 3:4, :])
    half = h2.shape[1] // 2
    h2_ref[...] = _pack_bf16_pair(h2[:, :half], h2[:, half:])
    w2 = wr_ref[...]
    w_hi = w2.astype(BF16)
    w_lo = (w2 - w_hi.astype(F32)).astype(BF16)
    w_split = jnp.where(lax.broadcasted_iota(jnp.int32, w2.shape, 1) < N_EXPERTS, w_hi, w_lo)
    h_hi = h2.astype(BF16)
    h_lo = (h2 - h_hi.astype(F32)).astype(BF16)
    r = (jnp.dot(h_hi, w_split, preferred_element_type=F32)
         + jnp.dot(h_lo, w_split, preferred_element_type=F32))
    lg_ref[...] = r[:, 0:N_EXPERTS] + r[:, N_EXPERTS:2 * N_EXPERTS] + br_ref[...]


def _mixer(proj, x2, mod, conv_w, w_out_bf, g2, w_router, b_router, seq):
    n, d = x2.shape
    blk = MIX_BLOCK
    per_batch = seq // blk
    cos, sin, dmask, dq, dkv, dblk = _mixer_tables(seq, blk)
    const2 = lambda i: (0, 0)
    const3 = lambda i: (0, 0, 0)
    n_blocks = n // blk
    mixed = lambda i: jnp.minimum(i, n_blocks - 1)
    done = lambda i: jnp.maximum(i - 1, 0)
    return pl.pallas_call(
        functools.partial(_mixer_kernel, per_batch=per_batch, n_blocks=n_blocks),
        grid=(n_blocks + 1,),
        in_specs=[pl.BlockSpec((blk, proj.shape[1]), lambda i: (mixed(i), 0)),
                  pl.BlockSpec((blk, d), lambda i: (done(i), 0)),
                  pl.BlockSpec((1, 6, d), lambda i: (done(i) // per_batch, 0, 0)),
                  pl.BlockSpec((blk, HEAD_DIM), lambda i: (mixed(i) % per_batch, 0)),
                  pl.BlockSpec((blk, HEAD_DIM), lambda i: (mixed(i) % per_batch, 0)),
                  pl.BlockSpec((N_HEADS, blk, blk), const3),
                  pl.BlockSpec((N_HEADS, blk, HEAD_DIM), const3),
                  pl.BlockSpec((N_HEADS, blk, HEAD_DIM), const3),
                  pl.BlockSpec((N_HEADS, 1, HEAD_DIM), const3),
                  pl.BlockSpec((CONV_WIDTH, D_CONV), const2),
                  pl.BlockSpec((D_RET + D_CONV, d), const2),
                  pl.BlockSpec((1, d), const2),
                  pl.BlockSpec((d, 2 * N_EXPERTS), const2),
                  pl.BlockSpec((1, N_EXPERTS), const2)],
        out_specs=[pl.BlockSpec((blk, d), lambda i: (done(i), 0)),
                   pl.BlockSpec((blk, d // 2), lambda i: (done(i), 0)),
                   pl.BlockSpec((blk, N_EXPERTS), lambda i: (done(i), 0))],
        out_shape=[jax.ShapeDtypeStruct((n, d), F32),
                   jax.ShapeDtypeStruct((n, d // 2), jnp.uint32),
                   jax.ShapeDtypeStruct((n, N_EXPERTS), F32)],
        scratch_shapes=[pltpu.VMEM((N_HEADS, HEAD_DIM, HEAD_DIM), F32),
                        pltpu.VMEM((blk + 8, D_CONV), F32),
                        pltpu.VMEM((2, blk, D_RET + D_CONV), BF16)],
        compiler_params=_tc_params(("arbitrary",)),
        name="mixer",
    )(proj, x2, mod, cos, sin, dmask, dq, dkv, dblk, conv_w, w_out_bf, g2.reshape(1, d),
      jnp.concatenate([w_router, w_router], axis=1), b_router.reshape(1, N_EXPERTS))


def _top_k_block(lg_ref, tri_ref, run_ref):
    n_e, bw = lg_ref.shape

    @pl.when(pl.program_id(0) == 0)
    def _():
        run_ref[...] = jnp.zeros_like(run_ref)

    l = lg_ref[...]
    eio = lax.broadcasted_iota(jnp.int32, (n_e, bw), 0)
    vals, hots = [], []
    for _ in range(TOP_K):
        m = jnp.max(l, axis=0, keepdims=True)
        idx = jnp.min(jnp.where(l == m, eio, n_e), axis=0, keepdims=True)
        hot = eio == idx
        vals.append(m)
        hots.append(hot)
        l = jnp.where(hot, -jnp.inf, l)
    sel = sum(jnp.where(hot, 1.0, 0.0) for hot in hots)
    incl = jnp.dot(sel.astype(BF16), tri_ref[...], preferred_element_type=F32)
    run = run_ref[:, 0:1]
    rank = incl - sel + run
    run_ref[...] = jnp.broadcast_to(run + incl[:, bw - 1:bw], run_ref.shape)
    return vals, hots, rank


def _count_kernel(lg_ref, tri_ref, cnt_ref, run_ref):
    _top_k_block(lg_ref, tri_ref, run_ref)
    cnt_ref[...] = run_ref[...]


def _route_kernel(lg_ref, tri_ref, low_ref, tot_ref, pos_ref, w_ref, run_ref, *, max_slots):
    vals, hots, rank = _top_k_block(lg_ref, tri_ref, run_ref)
    ex = [jnp.exp(v - vals[0]) for v in vals]
    den = ex[0] + ex[1] + ex[2] + ex[3]
    for k in range(TOP_K):
        w_ref[k:k + 1, :] = ex[k] / den
    tot = tot_ref[...]
    n_slots = sum(jnp.where(tot > float(m * SLOT_ROWS), 1.0, 0.0) for m in range(max_slots))
    start = jnp.dot(low_ref[...], n_slots.astype(BF16), preferred_element_type=F32)
    dest = start[:, 0:1] * float(SLOT_ROWS) + rank
    for k in range(TOP_K):
        pos_ref[k:k + 1, :] = jnp.sum(jnp.where(hots[k], dest, 0.0), axis=0,
                                      keepdims=True).astype(jnp.int32)


def _route(logits_t):
    n_e, n = logits_t.shape
    bw = 512
    max_slots = -(-n // SLOT_ROWS)
    tri = (jnp.arange(bw)[:, None] <= jnp.arange(bw)[None, :]).astype(BF16)
    low = (jnp.arange(n_e)[None, :] < jnp.arange(n_e)[:, None]).astype(BF16)
    lg_spec = pl.BlockSpec((n_e, bw), lambda c: (0, c))
    tri_spec = pl.BlockSpec((bw, bw), lambda c: (0, 0))
    cnt_spec = pl.BlockSpec((n_e, 128), lambda c: (0, 0))
    run = pltpu.VMEM((n_e, 128), F32)
    counts = pl.pallas_call(
        _count_kernel,
        grid=(n // bw,),
        in_specs=[lg_spec, tri_spec],
        out_specs=cnt_spec,
        out_shape=jax.ShapeDtypeStruct((n_e, 128), F32),
        scratch_shapes=[run],
        compiler_params=_tc_params(("arbitrary",)),
        name="route_count",
    )(logits_t, tri)
    pos, w_top = pl.pallas_call(
        functools.partial(_route_kernel, max_slots=max_slots),
        grid=(n // bw,),
        in_specs=[lg_spec, tri_spec, pl.BlockSpec((n_e, n_e), lambda c: (0, 0)), cnt_spec],
        out_specs=[pl.BlockSpec((TOP_K, bw), lambda c: (0, c)),
                   pl.BlockSpec((TOP_K, bw), lambda c: (0, c))],
        out_shape=[jax.ShapeDtypeStruct((TOP_K, n), jnp.int32),
                   jax.ShapeDtypeStruct((TOP_K, n), F32)],
        scratch_shapes=[run],
        compiler_params=_tc_params(("arbitrary",)),
        name="route",
    )(logits_t, tri, low, counts)
    return pos, w_top, counts


def _sc_mesh():
    return plsc.VectorSubcoreMesh(core_axis_name="c", subcore_axis_name="s")


def _sc_worker_id():
    return lax.axis_index("s") * lax.axis_size("c") + lax.axis_index("c")


def _dispatch_rows(h2, pos):
    n, d = h2.shape
    ch = SC_ROWS_PER_COPY
    per_w = n // SC_WORKERS
    n_ch = per_w // ch
    total_rows = _max_slots(n) * SLOT_ROWS
    idx = pos.reshape(TOP_K, SC_WORKERS, n_ch, ch).transpose(1, 2, 0, 3).reshape(
        SC_WORKERS, n_ch * TOP_K, ch)

    @functools.partial(
        pl.kernel, mesh=_sc_mesh(),
        out_type=jax.ShapeDtypeStruct((total_rows, d), h2.dtype),
        scratch_types=[pltpu.VMEM((n_ch * TOP_K, ch), jnp.int32),
                       pltpu.VMEM((ch, d), h2.dtype),
                       pltpu.SemaphoreType.DMA])
    def scatter(x_hbm, idx_hbm, o_hbm, idx_v, rows_v, sem):
        wid = _sc_worker_id()
        pltpu.sync_copy(idx_hbm.at[wid], idx_v)

        @pl.loop(0, n_ch)
        def _(c):
            r0 = pl.multiple_of(wid * per_w + c * ch, ch)
            pltpu.sync_copy(x_hbm.at[pl.ds(r0, ch)], rows_v)
            for k in range(TOP_K):
                pltpu.async_copy(rows_v, o_hbm.at[idx_v.at[c * TOP_K + k]], sem).wait()

    return scatter(h2, idx)


def _combine_rows(ys, pos_flat):
    n = pos_flat.shape[0]
    d = ys.shape[1]
    ch = SC_ROWS_PER_COPY
    per_w = n // SC_WORKERS
    n_ch = per_w // ch
    idx = pos_flat.reshape(SC_WORKERS, n_ch, ch)

    @functools.partial(
        pl.kernel, mesh=_sc_mesh(),
        out_type=jax.ShapeDtypeStruct((n, d), ys.dtype),
        scratch_types=[pltpu.VMEM((n_ch, ch), jnp.int32),
                       pltpu.VMEM((ch, d), ys.dtype),
                       pltpu.SemaphoreType.DMA])
    def gather(y_hbm, idx_hbm, o_hbm, idx_v, rows_v, sem):
        wid = _sc_worker_id()
        pltpu.sync_copy(idx_hbm.at[wid], idx_v)

        @pl.loop(0, n_ch)
        def _(c):
            r0 = pl.multiple_of(wid * per_w + c * ch, ch)
            pltpu.async_copy(y_hbm.at[idx_v.at[c]], rows_v, sem).wait()
            pltpu.sync_copy(rows_v, o_hbm.at[pl.ds(r0, ch)])

    return gather(ys, idx)


def _max_slots(n_tokens):
    return N_EXPERTS + (n_tokens * TOP_K) // SLOT_ROWS


def _expert_kernel(se_ref, sr_ref,
                   xs_ref, wgu_hbm, wd_hbm, bgu_ref, bd_ref,
                   ys_ref,
                   xb_ref, act_ref, wbuf_ref, sem_ref, *, n_a, n_b):
    s = pl.program_id(0)
    rows = sr_ref[s]
    n_blk = (rows + ROW_BLOCK - 1) // ROW_BLOCK
    tw = wbuf_ref.shape[2] // 2
    d_ff = act_ref.shape[1]
    half_d = ys_ref.shape[1]

    def panel_copy(src, buf, p):
        return pltpu.make_async_copy(src, wbuf_ref.at[buf, :, pl.ds(p * tw, tw)], sem_ref.at[buf, p])

    def gate_up_panels(e, t, buf):
        c0 = pl.multiple_of(t * tw, tw)
        return (panel_copy(wgu_hbm.at[e, :, pl.ds(c0, tw)], buf, 0),
                panel_copy(wgu_hbm.at[e, :, pl.ds(d_ff + c0, tw)], buf, 1))

    def down_panels(e, u, buf):
        c0 = pl.multiple_of(u * tw, tw)
        return (panel_copy(wd_hbm.at[e, :, pl.ds(c0, tw)], buf, 0),
                panel_copy(wd_hbm.at[e, :, pl.ds(half_d + c0, tw)], buf, 1))

    def start(panels):
        for cp in panels:
            cp.start()

    def wait(panels):
        for cp in panels:
            cp.wait()

    expert = se_ref[s]

    def for_row_blocks(fn):
        n_quads = n_blk // 4

        def body(rb, carry):
            fn(pl.multiple_of(rb * 4 * ROW_BLOCK, 4 * ROW_BLOCK), 4 * ROW_BLOCK)
            return carry

        lax.fori_loop(0, n_quads, body, 0)

        @pl.when((n_blk // 2) % 2 == 1)
        def _():
            fn(pl.multiple_of(n_quads * 4 * ROW_BLOCK, 4 * ROW_BLOCK), 2 * ROW_BLOCK)

        @pl.when(n_blk % 2 == 1)
        def _():
            fn(pl.multiple_of((n_blk // 2) * 2 * ROW_BLOCK, 2 * ROW_BLOCK), ROW_BLOCK)

    @pl.when(s == 0)
    def _():
        for k in range(RING_AHEAD):
            start(gate_up_panels(expert, k, k))

    def unpack(rb, carry):
        r0 = pl.multiple_of(rb * ROW_BLOCK, ROW_BLOCK)
        keep = (r0 + lax.broadcasted_iota(jnp.int32, (ROW_BLOCK, 1), 0)) < rows
        lo, hi = _unpack_bf16_pair(xs_ref[pl.ds(r0, ROW_BLOCK), :])
        xb_ref[pl.ds(r0, ROW_BLOCK), 0:half_d] = jnp.where(keep, lo, 0.0).astype(BF16)
        xb_ref[pl.ds(r0, ROW_BLOCK), half_d:2 * half_d] = jnp.where(keep, hi, 0.0).astype(BF16)
        return carry

    lax.fori_loop(0, n_blk, unpack, 0)

    ring = RING_AHEAD + 1

    def gate_up_tile(t, carry):
        buf = t % ring
        wait(gate_up_panels(expert, t, buf))
        nxt = t + RING_AHEAD

        @pl.when(nxt < n_a)
        def _():
            start(gate_up_panels(expert, nxt, nxt % ring))

        @pl.when(nxt >= n_a)
        def _():
            start(down_panels(expert, nxt - n_a, nxt % ring))

        col = pl.multiple_of(t * tw, tw)
        b_gate = bgu_ref[0, :, pl.ds(col, tw)]
        b_up = bgu_ref[0, :, pl.ds(d_ff + col, tw)]

        def gate_up(r0, m):
            gu = jnp.dot(xb_ref[pl.ds(r0, m), :], wbuf_ref[buf].astype(BF16), preferred_element_type=F32)
            gate = jnp.minimum(gu[:, 0:tw] + b_gate, SWIGLU_LIMIT)
            up = jnp.clip(gu[:, tw:2 * tw] + b_up, -SWIGLU_LIMIT, SWIGLU_LIMIT)
            act = (up + 1.0) * (gate * jax.nn.sigmoid(SWIGLU_ALPHA * gate))
            act_ref[pl.ds(r0, m), pl.ds(col, tw)] = act.astype(BF16)

        for_row_blocks(gate_up)
        return carry

    lax.fori_loop(0, n_a, gate_up_tile, 0)

    def down_tile(u, carry):
        buf = (n_a + u) % ring
        wait(down_panels(expert, u, buf))
        nxt = u + RING_AHEAD

        @pl.when(s + 1 < pl.num_programs(0))
        def _():
            start(gate_up_panels(se_ref[s + 1], nxt - n_b, (nxt - n_b) % ring))

        col = pl.multiple_of(u * tw, tw)
        b_lo = bd_ref[0, :, pl.ds(col, tw)]
        b_hi = bd_ref[0, :, pl.ds(half_d + col, tw)]

        def down(r0, m):
            y = jnp.dot(act_ref[pl.ds(r0, m), :], wbuf_ref[buf].astype(BF16), preferred_element_type=F32)
            ys_ref[pl.ds(r0, m), pl.ds(col, tw)] = _pack_bf16_pair(y[:, 0:tw] + b_lo, y[:, tw:2 * tw] + b_hi)

        for_row_blocks(down)
        return carry

    lax.fori_loop(0, n_b, down_tile, 0)

    def fill(rb, carry):
        r0 = pl.multiple_of(rb * ROW_BLOCK, ROW_BLOCK)
        ys_ref[pl.ds(r0, ROW_BLOCK), :] = jnp.zeros((ROW_BLOCK, half_d), jnp.uint32)
        return carry

    lax.fori_loop(n_blk, SLOT_ROWS // ROW_BLOCK, fill, 0)


def _experts(xs, slot_expert, slot_rows, n_used, w_gate_up, b_gate_up, w_down, b_down):
    n_e, d, two_ff = w_gate_up.shape
    d_ff = two_ff // 2
    assert d == d_ff, "the weight ring holds (rows, W_PANEL) panels of both projections"
    tw = W_PANEL
    n_a, n_b = d_ff // tw, (d // 2) // tw
    ring = RING_AHEAD + 1
    assert (n_a + n_b) % ring == 0, "ring slot of tile j must repeat from one expert slot to the next"
    assert n_b == RING_AHEAD <= n_a, "the look-ahead from a down tile must land in the next slot's gate/up tiles"
    total_rows = xs.shape[0]

    grid_spec = pltpu.PrefetchScalarGridSpec(
        num_scalar_prefetch=2,
        grid=(n_used,),
        in_specs=[
            pl.BlockSpec((SLOT_ROWS, d // 2), lambda s, se, sr: (s, 0)),
            pl.BlockSpec(memory_space=pl.ANY),
            pl.BlockSpec(memory_space=pl.ANY),
            pl.BlockSpec((1, 1, two_ff), lambda s, se, sr: (se[s], 0, 0)),
            pl.BlockSpec((1, 1, d), lambda s, se, sr: (se[s], 0, 0)),
        ],
        out_specs=pl.BlockSpec((SLOT_ROWS, d // 2), lambda s, se, sr: (s, 0)),
        scratch_shapes=[pltpu.VMEM((SLOT_ROWS, d), BF16),
                        pltpu.VMEM((SLOT_ROWS, d_ff), BF16),
                        pltpu.VMEM((ring, d, 2 * tw), F32),
                        pltpu.SemaphoreType.DMA((ring, 2))],
    )
    return pl.pallas_call(
        functools.partial(_expert_kernel, n_a=n_a, n_b=n_b),
        grid_spec=grid_spec,
        out_shape=jax.ShapeDtypeStruct((total_rows, d // 2), jnp.uint32),
        compiler_params=_tc_params(("arbitrary",)),
        name="experts",
    )(slot_expert, slot_rows, xs, w_gate_up, w_down,
      b_gate_up.reshape(n_e, 1, two_ff), b_down.reshape(n_e, 1, d))


def _slot_tables(counts, n_tokens):
    max_slots = _max_slots(n_tokens)
    n_slots = (counts + SLOT_ROWS - 1) // SLOT_ROWS
    slot_end = jnp.cumsum(n_slots)
    slot_start = slot_end - n_slots
    n_used = slot_end[-1]
    sid = jnp.arange(max_slots, dtype=jnp.int32)
    expert = jnp.minimum(jnp.sum(sid[:, None] >= slot_end[None, :], axis=1), N_EXPERTS - 1).astype(jnp.int32)
    local = sid - slot_start[expert]
    rows = jnp.clip(counts[expert] - local * SLOT_ROWS, 0, SLOT_ROWS)
    rows = jnp.where(sid < n_used, rows, 0).astype(jnp.int32)
    return expert, rows, n_used.reshape(1).astype(jnp.int32)


def _final_kernel(x1_ref, y0_ref, y1_ref, y2_ref, y3_ref, w_ref, mod_ref, g_ref, o_ref):
    half = y0_ref.shape[1]
    d = 2 * half
    moe_lo = moe_hi = None
    for k, y_ref in enumerate((y0_ref, y1_ref, y2_ref, y3_ref)):
        lo, hi = _unpack_bf16_pair(y_ref[...])
        wk = w_ref[:, k:k + 1]
        moe_lo = wk * lo if k == 0 else moe_lo + wk * lo
        moe_hi = wk * hi if k == 0 else moe_hi + wk * hi
    x_lo = x1_ref[:, 0:half] + mod_ref[0, 5:6, 0:half] * moe_lo
    x_hi = x1_ref[:, half:d] + mod_ref[0, 5:6, half:d] * moe_hi
    ss = jnp.sum(x_lo * x_lo, axis=-1, keepdims=True) + jnp.sum(x_hi * x_hi, axis=-1, keepdims=True)
    r = lax.rsqrt(ss / d + EPS)
    o_ref[:, 0:half] = x_lo * r * g_ref[:, 0:half]
    o_ref[:, half:d] = x_hi * r * g_ref[:, half:d]


def _final(x1, y4, w_tok, mod, final_g, seq):
    n, d = x1.shape
    tm = 256
    per_batch = seq // tm
    n_i = n // tm
    y_specs = [pl.BlockSpec((tm, d // 2), functools.partial(lambda i, k: (k * n_i + i, 0), k=k))
               for k in range(TOP_K)]
    return pl.pallas_call(
        _final_kernel,
        grid=(n_i,),
        in_specs=[pl.BlockSpec((tm, d), lambda i: (i, 0))] + y_specs + [
                  pl.BlockSpec((tm, TOP_K), lambda i: (i, 0)),
                  pl.BlockSpec((1, 6, d), lambda i: (i // per_batch, 0, 0)),
                  pl.BlockSpec((1, d), lambda i: (0, 0))],
        out_specs=pl.BlockSpec((tm, d), lambda i: (i, 0)),
        out_shape=jax.ShapeDtypeStruct((n, d), F32),
        compiler_params=_tc_params(("arbitrary",)),
        name="final",
    )(x1, y4, y4, y4, y4, w_tok, mod, final_g.reshape(1, d))


def kernel(x, c, norm1_g, w_mod, b_mod, w_in, conv_w, w_out, norm2_g, w_router, b_router,
           w_gate_up, b_gate_up, w_down, b_down, final_g):
    b, seq, d = x.shape
    n = b * seq
    x2 = x.reshape(n, d)
    mod = _modulation(c, w_mod[0], b_mod[0])
    proj = _in_projection(x2, mod, norm1_g[0], w_in[0].astype(BF16), seq)
    x1, h2, logits = _mixer(proj, x2, mod, conv_w[0], w_out[0].astype(BF16), norm2_g[0],
                            w_router[0], b_router[0], seq)
    pos, w_top, counts = _route(logits.T)
    slot_expert, slot_rows, n_used = _slot_tables(counts[:, 0].astype(jnp.int32), n)
    xs = _dispatch_rows(h2, pos)
    ys = _experts(xs, slot_expert, slot_rows, n_used[0], w_gate_up[0], b_gate_up[0], w_down[0], b_down[0])
    y4 = _combine_rows(ys, pos.reshape(TOP_K * n))
    out = _final(x1, y4, w_top.T, mod, final_g, seq)
    return out.reshape(b, seq, d)
```

```python
import functools
import math

import jax
import jax.numpy as jnp
from jax import lax
from jax.experimental import pallas as pl
from jax.experimental.pallas import tpu as pltpu
from jax.experimental.pallas import tpu_sc as plsc

F32 = jnp.float32
BF16 = jnp.bfloat16

CHUNK = 64
N_HEADS = 8
HEAD_DIM = 128
D_RET = N_HEADS * HEAD_DIM
D_CONV = 1024
CONV_WIDTH = 3
N_EXPERTS = 32
TOP_K = 4
SWIGLU_LIMIT = 7.0
SWIGLU_ALPHA = 1.702
ROPE_BASE = 10000.0
EPS = 1e-6

VMEM_LIMIT_BYTES = 58 * 1024 * 1024
SC_WORKERS = 32
SC_ROWS_PER_COPY = 32

MIX_BLOCK = 256
ROW_BLOCK = 128
SLOT_ROWS = 9 * ROW_BLOCK
W_PANEL = 512
RING_AHEAD = 2


def _tc_params(sem):
    return pltpu.CompilerParams(dimension_semantics=sem, vmem_limit_bytes=VMEM_LIMIT_BYTES)


def _bf16_bits(x):
    b = lax.bitcast_convert_type(x, jnp.uint32)
    return (b + jnp.uint32(0x7FFF) + ((b >> 16) & jnp.uint32(1))) & jnp.uint32(0xFFFF0000)


def _pack_bf16_pair(lo, hi):
    return (_bf16_bits(lo) >> 16) | _bf16_bits(hi)


def _unpack_bf16_pair(p):
    lo = lax.bitcast_convert_type(p << 16, F32)
    hi = lax.bitcast_convert_type(p & jnp.uint32(0xFFFF0000), F32)
    return lo, hi


def _mod_kernel(c_ref, w_ref, b_ref, o_ref):
    c = c_ref[...]
    ca = (c * jax.nn.sigmoid(c)).astype(BF16)
    o_ref[...] = jnp.dot(ca, w_ref[...].astype(BF16), preferred_element_type=F32) + b_ref[...]


def _modulation(c, w_mod, b_mod):
    b, d = c.shape
    n = w_mod.shape[1]
    tn = 1024
    c8 = jnp.zeros((8, d), F32).at[:b].set(c)
    out = pl.pallas_call(
        _mod_kernel,
        grid=(n // tn,),
        in_specs=[pl.BlockSpec((8, d), lambda j: (0, 0)),
                  pl.BlockSpec((d, tn), lambda j: (0, j)),
                  pl.BlockSpec((1, tn), lambda j: (0, j))],
        out_specs=pl.BlockSpec((8, tn), lambda j: (0, j)),
        out_shape=jax.ShapeDtypeStruct((8, n), F32),
        compiler_params=_tc_params(("arbitrary",)),
        name="mod",
    )(c8, w_mod, b_mod.reshape(1, n))
    return out[:b].reshape(b, 6, d)


def _norm_mod(x, g, scale, shift):
    y = x * lax.rsqrt(jnp.mean(x * x, axis=-1, keepdims=True) + EPS) * g
    return y * (1.0 + scale) + shift


def _inproj_kernel(x_ref, mod_ref, g_ref, w_ref, o_ref, h_ref, *, rows):
    @pl.when(pl.program_id(1) == 0)
    def _():
        g = g_ref[...]
        scale = mod_ref[0, 1:2, :]
        shift = mod_ref[0, 0:1, :]

        def body(r, carry):
            r0 = pl.multiple_of(r * rows, rows)
            h = _norm_mod(x_ref[pl.ds(r0, rows), :], g, scale, shift)
            h_ref[pl.ds(r0, rows), :] = h.astype(BF16)
            return carry

        lax.fori_loop(0, x_ref.shape[0] // rows, body, 0)

    o_ref[...] = jnp.dot(h_ref[...], w_ref[...], preferred_element_type=F32).astype(o_ref.dtype)


def _in_projection(x2, mod, g1, w_in_bf, seq):
    n, d = x2.shape
    p = w_in_bf.shape[1]
    tm, tn = min(1024, seq), 1024
    per_batch = seq // tm
    return pl.pallas_call(
        functools.partial(_inproj_kernel, rows=128),
        grid=(n // tm, p // tn),
        in_specs=[pl.BlockSpec((tm, d), lambda i, j: (i, 0)),
                  pl.BlockSpec((1, 6, d), lambda i, j: (i // per_batch, 0, 0)),
                  pl.BlockSpec((1, d), lambda i, j: (0, 0)),
                  pl.BlockSpec((d, tn), lambda i, j: (0, j))],
        out_specs=pl.BlockSpec((tm, tn), lambda i, j: (i, j)),
        out_shape=jax.ShapeDtypeStruct((n, p), BF16),
        scratch_shapes=[pltpu.VMEM((tm, d), BF16)],
        compiler_params=_tc_params(("arbitrary", "arbitrary")),
        name="inproj",
    )(x2, mod, g1.reshape(1, d), w_in_bf)


def _mixer_tables(seq, blk):
    half = HEAD_DIM // 2
    freqs = ROPE_BASE ** (-jnp.arange(half, dtype=F32) / half)
    ang = jnp.arange(seq, dtype=F32)[:, None] * freqs[None, :]
    cos = jnp.concatenate([jnp.cos(ang), jnp.cos(ang)], axis=-1)
    sin = jnp.concatenate([-jnp.sin(ang), jnp.sin(ang)], axis=-1)
    log_gamma = jnp.log1p(-jnp.exp2(-5.0 - jnp.arange(N_HEADS, dtype=F32)))
    idx = jnp.arange(blk, dtype=F32)
    dist = jnp.abs(idx[:, None] - idx[None, :])
    ck = jnp.arange(blk) // CHUNK
    visible = (ck[None, :] <= ck[:, None]).astype(F32)
    dmask = jnp.exp(log_gamma[:, None, None] * dist) * visible
    ones = jnp.ones((1, 1, HEAD_DIM), F32)
    dq = jnp.exp(log_gamma[:, None] * (idx + 1.0)[None])[:, :, None] * ones
    dkv = jnp.exp(log_gamma[:, None] * (blk - 1 - idx)[None])[:, :, None] * ones
    dblk = jnp.exp(log_gamma * blk)[:, None, None] * ones
    return cos, sin, dmask, dq, dkv, dblk


def _mixer_kernel(proj_ref, x_ref, mod_ref, cos_ref, sin_ref, dmask_ref, dq_ref, dkv_ref, dblk_ref,
                  convw_ref, wout_ref, g2_ref, wr_ref, br_ref,
                  x1_ref, h2_ref, lg_ref,
                  s_ref, zbuf_ref, y_ref, *, per_batch, n_blocks):
    blk = x_ref.shape[0]
    i = pl.program_id(0)
    cur = i % 2

    @pl.when(jnp.minimum(i, n_blocks - 1) % per_batch == 0)
    def _():
        s_ref[...] = jnp.zeros_like(s_ref)
        zbuf_ref[0:8, :] = jnp.zeros((8, D_CONV), F32)

    @pl.when(i == 0)
    def _():
        y_ref[1] = jnp.zeros(y_ref.shape[1:], BF16)

    cos = cos_ref[...]
    sin = sin_ref[...]
    k_scale = HEAD_DIM ** -0.5
    nt = (((1,), (1,)), ((), ()))
    tn = (((0,), (0,)), ((), ()))
    for h in range(N_HEADS):
        c0 = h * HEAD_DIM
        q = proj_ref[:, c0:c0 + HEAD_DIM].astype(F32)
        k = proj_ref[:, D_RET + c0:D_RET + c0 + HEAD_DIM].astype(F32)
        v = proj_ref[:, 2 * D_RET + c0:2 * D_RET + c0 + HEAD_DIM]
        g = proj_ref[:, 3 * D_RET + c0:3 * D_RET + c0 + HEAD_DIM].astype(F32)
        qr = q * cos + pltpu.roll(q, HEAD_DIM // 2, 1) * sin
        kr = (k * cos + pltpu.roll(k, HEAD_DIM // 2, 1) * sin) * k_scale
        qb = qr.astype(BF16)
        kb = kr.astype(BF16)
        scores = lax.dot_general(qb, kb, nt, preferred_element_type=F32) * dmask_ref[h]
        intra = jnp.dot(scores.astype(BF16), v, preferred_element_type=F32)
        state = s_ref[h]
        cross = jnp.dot(qb, state.astype(BF16), preferred_element_type=F32) * dq_ref[h]
        kd = (kr * dkv_ref[h]).astype(BF16)
        kv = lax.dot_general(kd, v, tn, preferred_element_type=F32)
        s_ref[h] = dblk_ref[h] * state + kv
        o = intra + cross
        mu = jnp.mean(o, axis=-1, keepdims=True)
        dev = o - mu
        var = jnp.mean(dev * dev, axis=-1, keepdims=True)
        on = dev * lax.rsqrt(var + EPS)
        y_ref[cur, :, c0:c0 + HEAD_DIM] = (g * jax.nn.sigmoid(g) * on).astype(BF16)

    cw = 256
    base = 4 * D_RET
    for cb in range(D_CONV // cw):
        lo = cb * cw
        cg = proj_ref[:, base + D_CONV + lo:base + D_CONV + lo + cw].astype(F32)
        u = proj_ref[:, base + 2 * D_CONV + lo:base + 2 * D_CONV + lo + cw].astype(F32)
        zbuf_ref[8:blk + 8, lo:lo + cw] = cg * u
    for cb in range(D_CONV // cw):
        lo = cb * cw
        z0 = zbuf_ref[8:blk + 8, lo:lo + cw]
        z1 = zbuf_ref[7:blk + 7, lo:lo + cw]
        z2 = zbuf_ref[6:blk + 6, lo:lo + cw]
        z = (convw_ref[2:3, lo:lo + cw] * z0 + convw_ref[1:2, lo:lo + cw] * z1
             + convw_ref[0:1, lo:lo + cw] * z2)
        bg = proj_ref[:, base + lo:base + lo + cw].astype(F32)
        y_ref[cur, :, D_RET + lo:D_RET + lo + cw] = (bg * z).astype(BF16)
    zbuf_ref[0:8, :] = zbuf_ref[blk:blk + 8, :]

    mix = jnp.dot(y_ref[1 - cur], wout_ref[...], preferred_element_type=F32)
    x1 = x_ref[...] + mod_ref[0, 2:3, :] * mix
    x1_ref[...] = x1
    h2 = _norm_mod(x1, g2_ref[...], mod_ref[0, 4:5, :], mod_ref[0, 3:4, :])
    half = h2.shape[1] // 2
    h2_ref[...] = _pack_bf16_pair(h2[:, :half], h2[:, half:])
    w2 = wr_ref[...]
    w_hi = w2.astype(BF16)
    w_lo = (w2 - w_hi.astype(F32)).astype(BF16)
    w_split = jnp.where(lax.broadcasted_iota(jnp.int32, w2.shape, 1) < N_EXPERTS, w_hi, w_lo)
    h_hi = h2.astype(BF16)
    h_lo = (h2 - h_hi.astype(F32)).astype(BF16)
    r = (jnp.dot(h_hi, w_split, preferred_element_type=F32)
         + jnp.dot(h_lo, w_split, preferred_element_type=F32))
    lg_ref[...] = r[:, 0:N_EXPERTS] + r[:, N_EXPERTS:2 * N_EXPERTS] + br_ref[...]


def _mixer(proj, x2, mod, conv_w, w_out_bf, g2, w_router, b_router, seq):
    n, d = x2.shape
    blk = MIX_BLOCK
    per_batch = seq // blk
    cos, sin, dmask, dq, dkv, dblk = _mixer_tables(seq, blk)
    const2 = lambda i: (0, 0)
    const3 = lambda i: (0, 0, 0)
    n_blocks = n // blk
    mixed = lambda i: jnp.minimum(i, n_blocks - 1)
    done = lambda i: jnp.maximum(i - 1, 0)
    return pl.pallas_call(
        functools.partial(_mixer_kernel, per_batch=per_batch, n_blocks=n_blocks),
        grid=(n_blocks + 1,),
        in_specs=[pl.BlockSpec((blk, proj.shape[1]), lambda i: (mixed(i), 0)),
                  pl.BlockSpec((blk, d), lambda i: (done(i), 0)),
                  pl.BlockSpec((1, 6, d), lambda i: (done(i) // per_batch, 0, 0)),
                  pl.BlockSpec((blk, HEAD_DIM), lambda i: (mixed(i) % per_batch, 0)),
                  pl.BlockSpec((blk, HEAD_DIM), lambda i: (mixed(i) % per_batch, 0)),
                  pl.BlockSpec((N_HEADS, blk, blk), const3),
                  pl.BlockSpec((N_HEADS, blk, HEAD_DIM), const3),
                  pl.BlockSpec((N_HEADS, blk, HEAD_DIM), const3),
                  pl.BlockSpec((N_HEADS, 1, HEAD_DIM), const3),
                  pl.BlockSpec((CONV_WIDTH, D_CONV), const2),
                  pl.BlockSpec((D_RET + D_CONV, d), const2),
                  pl.BlockSpec((1, d), const2),
                  pl.BlockSpec((d, 2 * N_EXPERTS), const2),
                  pl.BlockSpec((1, N_EXPERTS), const2)],
        out_specs=[pl.BlockSpec((blk, d), lambda i: (done(i), 0)),
                   pl.BlockSpec((blk, d // 2), lambda i: (done(i), 0)),
                   pl.BlockSpec((blk, N_EXPERTS), lambda i: (done(i), 0))],
        out_shape=[jax.ShapeDtypeStruct((n, d), F32),
                   jax.ShapeDtypeStruct((n, d // 2), jnp.uint32),
                   jax.ShapeDtypeStruct((n, N_EXPERTS), F32)],
        scratch_shapes=[pltpu.VMEM((N_HEADS, HEAD_DIM, HEAD_DIM), F32),
                        pltpu.VMEM((blk + 8, D_CONV), F32),
                        pltpu.VMEM((2, blk, D_RET + D_CONV), BF16)],
        compiler_params=_tc_params(("arbitrary",)),
        name="mixer",
    )(proj, x2, mod, cos, sin, dmask, dq, dkv, dblk, conv_w, w_out_bf, g2.reshape(1, d),
      jnp.concatenate([w_router, w_router], axis=1), b_router.reshape(1, N_EXPERTS))


def _top_k_block(lg_ref, tri_ref, run_ref):
    n_e, bw = lg_ref.shape

    @pl.when(pl.program_id(0) == 0)
    def _():
        run_ref[...] = jnp.zeros_like(run_ref)

    l = lg_ref[...]
    eio = lax.broadcasted_iota(jnp.int32, (n_e, bw), 0)
    vals, hots = [], []
    for _ in range(TOP_K):
        m = jnp.max(l, axis=0, keepdims=True)
        idx = jnp.min(jnp.where(l == m, eio, n_e), axis=0, keepdims=True)
        hot = eio == idx
        vals.append(m)
        hots.append(hot)
        l = jnp.where(hot, -jnp.inf, l)
    sel = sum(jnp.where(hot, 1.0, 0.0) for hot in hots)
    incl = jnp.dot(sel.astype(BF16), tri_ref[...], preferred_element_type=F32)
    run = run_ref[:, 0:1]
    rank = incl - sel + run
    run_ref[...] = jnp.broadcast_to(run + incl[:, bw - 1:bw], run_ref.shape)
    return vals, hots, rank


def _count_kernel(lg_ref, tri_ref, cnt_ref, run_ref):
    _top_k_block(lg_ref, tri_ref, run_ref)
    cnt_ref[...] = run_ref[...]


def _route_kernel(lg_ref, tri_ref, low_ref, tot_ref, pos_ref, w_ref, run_ref, *, max_slots):
    vals, hots, rank = _top_k_block(lg_ref, tri_ref, run_ref)
    ex = [jnp.exp(v - vals[0]) for v in vals]
    den = ex[0] + ex[1] + ex[2] + ex[3]
    for k in range(TOP_K):
        w_ref[k:k + 1, :] = ex[k] / den
    tot = tot_ref[...]
    n_slots = sum(jnp.where(tot > float(m * SLOT_ROWS), 1.0, 0.0) for m in range(max_slots))
    start = jnp.dot(low_ref[...], n_slots.astype(BF16), preferred_element_type=F32)
    dest = start[:, 0:1] * float(SLOT_ROWS) + rank
    for k in range(TOP_K):
        pos_ref[k:k + 1, :] = jnp.sum(jnp.where(hots[k], dest, 0.0), axis=0,
                                      keepdims=True).astype(jnp.int32)


def _route(logits_t):
    n_e, n = logits_t.shape
    bw = 512
    max_slots = -(-n // SLOT_ROWS)
    tri = (jnp.arange(bw)[:, None] <= jnp.arange(bw)[None, :]).astype(BF16)
    low = (jnp.arange(n_e)[None, :] < jnp.arange(n_e)[:, None]).astype(BF16)
    lg_spec = pl.BlockSpec((n_e, bw), lambda c: (0, c))
    tri_spec = pl.BlockSpec((bw, bw), lambda c: (0, 0))
    cnt_spec = pl.BlockSpec((n_e, 128), lambda c: (0, 0))
    run = pltpu.VMEM((n_e, 128), F32)
    counts = pl.pallas_call(
        _count_kernel,
        grid=(n // bw,),
        in_specs=[lg_spec, tri_spec],
        out_specs=cnt_spec,
        out_shape=jax.ShapeDtypeStruct((n_e, 128), F32),
        scratch_shapes=[run],
        compiler_params=_tc_params(("arbitrary",)),
        name="route_count",
    )(logits_t, tri)
    pos, w_top = pl.pallas_call(
        functools.partial(_route_kernel, max_slots=max_slots),
        grid=(n // bw,),
        in_specs=[lg_spec, tri_spec, pl.BlockSpec((n_e, n_e), lambda c: (0, 0)), cnt_spec],
        out_specs=[pl.BlockSpec((TOP_K, bw), lambda c: (0, c)),
                   pl.BlockSpec((TOP_K, bw), lambda c: (0, c))],
        out_shape=[jax.ShapeDtypeStruct((TOP_K, n), jnp.int32),
                   jax.ShapeDtypeStruct((TOP_K, n), F32)],
        scratch_shapes=[run],
        compiler_params=_tc_params(("arbitrary",)),
        name="route",
    )(logits_t, tri, low, counts)
    return pos, w_top, counts


def _sc_mesh():
    return plsc.VectorSubcoreMesh(core_axis_name="c", subcore_axis_name="s")


def _sc_worker_id():
    return lax.axis_index("s") * lax.axis_size("c") + lax.axis_index("c")


def _dispatch_rows(h2, pos):
    n, d = h2.shape
    ch = SC_ROWS_PER_COPY
    per_w = n // SC_WORKERS
    n_ch = per_w // ch
    total_rows = _max_slots(n) * SLOT_ROWS
    idx = pos.reshape(TOP_K, SC_WORKERS, n_ch, ch).transpose(1, 2, 0, 3).reshape(
        SC_WORKERS, n_ch * TOP_K, ch)

    @functools.partial(
        pl.kernel, mesh=_sc_mesh(),
        out_type=jax.ShapeDtypeStruct((total_rows, d), h2.dtype),
        scratch_types=[pltpu.VMEM((n_ch * TOP_K, ch), jnp.int32),
                       pltpu.VMEM((ch, d), h2.dtype), pltpu.VMEM((ch, d), h2.dtype),
                       pltpu.SemaphoreType.DMA, pltpu.SemaphoreType.DMA,
                       pltpu.SemaphoreType.DMA, pltpu.SemaphoreType.DMA])
    def scatter(x_hbm, idx_hbm, o_hbm, idx_v, rows0, rows1, rsem0, rsem1, ssem0, ssem1):
        wid = _sc_worker_id()
        base = wid * per_w
        pltpu.sync_copy(idx_hbm.at[wid], idx_v)
        rows, rsem, ssem = (rows0, rows1), (rsem0, rsem1), (ssem0, ssem1)

        def read(c, b):
            r0 = pl.multiple_of(base + c * ch, ch)
            return pltpu.make_async_copy(x_hbm.at[pl.ds(r0, ch)], rows[b], rsem[b])

        def send(c, k, b):
            return pltpu.make_async_copy(rows[b], o_hbm.at[idx_v.at[c * TOP_K + k]], ssem[b])

        read(0, 0).start()

        @pl.loop(0, n_ch, step=2)
        def _(c):
            read(c, 0).wait()

            @pl.when(c > 0)
            def _():
                for k in range(TOP_K):
                    send(c - 1, k, 1).wait()

            read(c + 1, 1).start()
            for k in range(TOP_K):
                send(c, k, 0).start()
            read(c + 1, 1).wait()
            for k in range(TOP_K):
                send(c, k, 0).wait()

            @pl.when(c + 2 < n_ch)
            def _():
                read(c + 2, 0).start()

            for k in range(TOP_K):
                send(c + 1, k, 1).start()

        for k in range(TOP_K):
            send(n_ch - 1, k, 1).wait()

    return scatter(h2, idx)


def _combine_rows(ys, pos_flat):
    n = pos_flat.shape[0]
    d = ys.shape[1]
    ch = SC_ROWS_PER_COPY
    per_w = n // SC_WORKERS
    n_ch = per_w // ch
    idx = pos_flat.reshape(SC_WORKERS, n_ch, ch)

    @functools.partial(
        pl.kernel, mesh=_sc_mesh(),
        out_type=jax.ShapeDtypeStruct((n, d), ys.dtype),
        scratch_types=[pltpu.VMEM((n_ch, ch), jnp.int32),
                       pltpu.VMEM((ch, d), ys.dtype), pltpu.VMEM((ch, d), ys.dtype),
                       pltpu.SemaphoreType.DMA, pltpu.SemaphoreType.DMA,
                       pltpu.SemaphoreType.DMA, pltpu.SemaphoreType.DMA])
    def gather(y_hbm, idx_hbm, o_hbm, idx_v, rows0, rows1, gsem0, gsem1, wsem0, wsem1):
        wid = _sc_worker_id()
        base = wid * per_w
        pltpu.sync_copy(idx_hbm.at[wid], idx_v)
        rows, gsem, wsem = (rows0, rows1), (gsem0, gsem1), (wsem0, wsem1)

        def fetch(c, b):
            return pltpu.make_async_copy(y_hbm.at[idx_v.at[c]], rows[b], gsem[b])

        def write(c, b):
            r0 = pl.multiple_of(base + c * ch, ch)
            return pltpu.make_async_copy(rows[b], o_hbm.at[pl.ds(r0, ch)], wsem[b])

        fetch(0, 0).start()

        @pl.loop(0, n_ch, step=2)
        def _(c):
            fetch(c, 0).wait()

            @pl.when(c > 0)
            def _():
                write(c - 1, 1).wait()

            fetch(c + 1, 1).start()
            write(c, 0).start()
            fetch(c + 1, 1).wait()
            write(c, 0).wait()

            @pl.when(c + 2 < n_ch)
            def _():
                fetch(c + 2, 0).start()

            write(c + 1, 1).start()

        write(n_ch - 1, 1).wait()

    return gather(ys, idx)


def _max_slots(n_tokens):
    return N_EXPERTS + (n_tokens * TOP_K) // SLOT_ROWS


def _expert_kernel(se_ref, sr_ref,
                   xs_ref, wgu_hbm, wd_hbm, bgu_ref, bd_ref,
                   ys_ref,
                   xb_ref, act_ref, wbuf_ref, sem_ref, *, n_a, n_b):
    s = pl.program_id(0)
    rows = sr_ref[s]
    n_blk = (rows + ROW_BLOCK - 1) // ROW_BLOCK
    tw = wbuf_ref.shape[2] // 2
    d_ff = act_ref.shape[1]
    half_d = ys_ref.shape[1]

    def panel_copy(src, buf, p):
        return pltpu.make_async_copy(src, wbuf_ref.at[buf, :, pl.ds(p * tw, tw)], sem_ref.at[buf, p])

    def gate_up_panels(e, t, buf):
        c0 = pl.multiple_of(t * tw, tw)
        return (panel_copy(wgu_hbm.at[e, :, pl.ds(c0, tw)], buf, 0),
                panel_copy(wgu_hbm.at[e, :, pl.ds(d_ff + c0, tw)], buf, 1))

    def down_panels(e, u, buf):
        c0 = pl.multiple_of(u * tw, tw)
        return (panel_copy(wd_hbm.at[e, :, pl.ds(c0, tw)], buf, 0),
                panel_copy(wd_hbm.at[e, :, pl.ds(half_d + c0, tw)], buf, 1))

    def start(panels):
        for cp in panels:
            cp.start()

    def wait(panels):
        for cp in panels:
            cp.wait()

    expert = se_ref[s]

    def for_row_blocks(fn):
        n_quads = n_blk // 4

        def body(rb, carry):
            fn(pl.multiple_of(rb * 4 * ROW_BLOCK, 4 * ROW_BLOCK), 4 * ROW_BLOCK)
            return carry

        lax.fori_loop(0, n_quads, body, 0)

        @pl.when((n_blk // 2) % 2 == 1)
        def _():
            fn(pl.multiple_of(n_quads * 4 * ROW_BLOCK, 4 * ROW_BLOCK), 2 * ROW_BLOCK)

        @pl.when(n_blk % 2 == 1)
        def _():
            fn(pl.multiple_of((n_blk // 2) * 2 * ROW_BLOCK, 2 * ROW_BLOCK), ROW_BLOCK)

    @pl.when(s == 0)
    def _():
        for k in range(RING_AHEAD):
            start(gate_up_panels(expert, k, k))

    def unpack(rb, carry):
        r0 = pl.multiple_of(rb * ROW_BLOCK, ROW_BLOCK)
        keep = (r0 + lax.broadcasted_iota(jnp.int32, (ROW_BLOCK, 1), 0)) < rows
        lo, hi = _unpack_bf16_pair(xs_ref[pl.ds(r0, ROW_BLOCK), :])
        xb_ref[pl.ds(r0, ROW_BLOCK), 0:half_d] = jnp.where(keep, lo, 0.0).astype(BF16)
        xb_ref[pl.ds(r0, ROW_BLOCK), half_d:2 * half_d] = jnp.where(keep, hi, 0.0).astype(BF16)
        return carry

    lax.fori_loop(0, n_blk, unpack, 0)

    ring = RING_AHEAD + 1

    def gate_up_tile(t, carry):
        buf = t % ring
        wait(gate_up_panels(expert, t, buf))
        nxt = t + RING_AHEAD

        @pl.when(nxt < n_a)
        def _():
            start(gate_up_panels(expert, nxt, nxt % ring))

        @pl.when(nxt >= n_a)
        def _():
            start(down_panels(expert, nxt - n_a, nxt % ring))

        col = pl.multiple_of(t * tw, tw)
        b_gate = bgu_ref[0, :, pl.ds(col, tw)]
        b_up = bgu_ref[0, :, pl.ds(d_ff + col, tw)]

        def gate_up(r0, m):
            gu = jnp.dot(xb_ref[pl.ds(r0, m), :], wbuf_ref[buf].astype(BF16), preferred_element_type=F32)
            gate = jnp.minimum(gu[:, 0:tw] + b_gate, SWIGLU_LIMIT)
            up = jnp.clip(gu[:, tw:2 * tw] + b_up, -SWIGLU_LIMIT, SWIGLU_LIMIT)
            act = (up + 1.0) * (gate * jax.nn.sigmoid(SWIGLU_ALPHA * gate))
            act_ref[pl.ds(r0, m), pl.ds(col, tw)] = act.astype(BF16)

        for_row_blocks(gate_up)
        return carry

    lax.fori_loop(0, n_a, gate_up_tile, 0)

    def down_tile(u, carry):
        buf = (n_a + u) % ring
        wait(down_panels(expert, u, buf))
        nxt = u + RING_AHEAD

        @pl.when(s + 1 < pl.num_programs(0))
        def _():
            start(gate_up_panels(se_ref[s + 1], nxt - n_b, (nxt - n_b) % ring))

        col = pl.multiple_of(u * tw, tw)
        b_lo = bd_ref[0, :, pl.ds(col, tw)]
        b_hi = bd_ref[0, :, pl.ds(half_d + col, tw)]

        def down(r0, m):
            y = jnp.dot(act_ref[pl.ds(r0, m), :], wbuf_ref[buf].astype(BF16), preferred_element_type=F32)
            ys_ref[pl.ds(r0, m), pl.ds(col, tw)] = _pack_bf16_pair(y[:, 0:tw] + b_lo, y[:, tw:2 * tw] + b_hi)

        for_row_blocks(down)
        return carry

    lax.fori_loop(0, n_b, down_tile, 0)

    def fill(rb, carry):
        r0 = pl.multiple_of(rb * ROW_BLOCK, ROW_BLOCK)
        ys_ref[pl.ds(r0, ROW_BLOCK), :] = jnp.zeros((ROW_BLOCK, half_d), jnp.uint32)
        return carry

    lax.fori_loop(n_blk, SLOT_ROWS // ROW_BLOCK, fill, 0)


def _experts(xs, slot_expert, slot_rows, n_used, w_gate_up, b_gate_up, w_down, b_down):
    n_e, d, two_ff = w_gate_up.shape
    d_ff = two_ff // 2
    assert d == d_ff, "the weight ring holds (rows, W_PANEL) panels of both projections"
    tw = W_PANEL
    n_a, n_b = d_ff // tw, (d // 2) // tw
    ring = RING_AHEAD + 1
    assert (n_a + n_b) % ring == 0, "ring slot of tile j must repeat from one expert slot to the next"
    assert n_b == RING_AHEAD <= n_a, "the look-ahead from a down tile must land in the next slot's gate/up tiles"
    total_rows = xs.shape[0]

    grid_spec = pltpu.PrefetchScalarGridSpec(
        num_scalar_prefetch=2,
        grid=(n_used,),
        in_specs=[
            pl.BlockSpec((SLOT_ROWS, d // 2), lambda s, se, sr: (s, 0)),
            pl.BlockSpec(memory_space=pl.ANY),
            pl.BlockSpec(memory_space=pl.ANY),
            pl.BlockSpec((1, 1, two_ff), lambda s, se, sr: (se[s], 0, 0)),
            pl.BlockSpec((1, 1, d), lambda s, se, sr: (se[s], 0, 0)),
        ],
        out_specs=pl.BlockSpec((SLOT_ROWS, d // 2), lambda s, se, sr: (s, 0)),
        scratch_shapes=[pltpu.VMEM((SLOT_ROWS, d), BF16),
                        pltpu.VMEM((SLOT_ROWS, d_ff), BF16),
                        pltpu.VMEM((ring, d, 2 * tw), F32),
                        pltpu.SemaphoreType.DMA((ring, 2))],
    )
    return pl.pallas_call(
        functools.partial(_expert_kernel, n_a=n_a, n_b=n_b),
        grid_spec=grid_spec,
        out_shape=jax.ShapeDtypeStruct((total_rows, d // 2), jnp.uint32),
        compiler_params=_tc_params(("arbitrary",)),
        name="experts",
    )(slot_expert, slot_rows, xs, w_gate_up, w_down,
      b_gate_up.reshape(n_e, 1, two_ff), b_down.reshape(n_e, 1, d))


def _slot_tables(counts, n_tokens):
    max_slots = _max_slots(n_tokens)
    n_slots = (counts + SLOT_ROWS - 1) // SLOT_ROWS
    slot_end = jnp.cumsum(n_slots)
    slot_start = slot_end - n_slots
    n_used = slot_end[-1]
    sid = jnp.arange(max_slots, dtype=jnp.int32)
    expert = jnp.minimum(jnp.sum(sid[:, None] >= slot_end[None, :], axis=1), N_EXPERTS - 1).astype(jnp.int32)
    local = sid - slot_start[expert]
    rows = jnp.clip(counts[expert] - local * SLOT_ROWS, 0, SLOT_ROWS)
    rows = jnp.where(sid < n_used, rows, 0).astype(jnp.int32)
    return expert, rows, n_used.reshape(1).astype(jnp.int32)


def _final_kernel(x1_ref, y0_ref, y1_ref, y2_ref, y3_ref, w_ref, mod_ref, g_ref, o_ref):
    half = y0_ref.shape[1]
    d = 2 * half
    moe_lo = moe_hi = None
    for k, y_ref in enumerate((y0_ref, y1_ref, y2_ref, y3_ref)):
        lo, hi = _unpack_bf16_pair(y_ref[...])
        wk = w_ref[:, k:k + 1]
        moe_lo = wk * lo if k == 0 else moe_lo + wk * lo
        moe_hi = wk * hi if k == 0 else moe_hi + wk * hi
    x_lo = x1_ref[:, 0:half] + mod_ref[0, 5:6, 0:half] * moe_lo
    x_hi = x1_ref[:, half:d] + mod_ref[0, 5:6, half:d] * moe_hi
    ss = jnp.sum(x_lo * x_lo, axis=-1, keepdims=True) + jnp.sum(x_hi * x_hi, axis=-1, keepdims=True)
    r = lax.rsqrt(ss / d + EPS)
    o_ref[:, 0:half] = x_lo * r * g_ref[:, 0:half]
    o_ref[:, half:d] = x_hi * r * g_ref[:, half:d]


def _final(x1, y4, w_tok, mod, final_g, seq):
    n, d = x1.shape
    tm = 256
    per_batch = seq // tm
    n_i = n // tm
    y_specs = [pl.BlockSpec((tm, d // 2), functools.partial(lambda i, k: (k * n_i + i, 0), k=k))
               for k in range(TOP_K)]
    return pl.pallas_call(
        _final_kernel,
        grid=(n_i,),
        in_specs=[pl.BlockSpec((tm, d), lambda i: (i, 0))] + y_specs + [
                  pl.BlockSpec((tm, TOP_K), lambda i: (i, 0)),
                  pl.BlockSpec((1, 6, d), lambda i: (i // per_batch, 0, 0)),
                  pl.BlockSpec((1, d), lambda i: (0, 0))],
        out_specs=pl.BlockSpec((tm, d), lambda i: (i, 0)),
        out_shape=jax.ShapeDtypeStruct((n, d), F32),
        compiler_params=_tc_params(("arbitrary",)),
        name="final",
    )(x1, y4, y4, y4, y4, w_tok, mod, final_g.reshape(1, d))


def kernel(x, c, norm1_g, w_mod, b_mod, w_in, conv_w, w_out, norm2_g, w_router, b_router,
           w_gate_up, b_gate_up, w_down, b_down, final_g):
    b, seq, d = x.shape
    n = b * seq
    x2 = x.reshape(n, d)
    mod = _modulation(c, w_mod[0], b_mod[0])
    proj = _in_projection(x2, mod, norm1_g[0], w_in[0].astype(BF16), seq)
    x1, h2, logits = _mixer(proj, x2, mod, conv_w[0], w_out[0].astype(BF16), norm2_g[0],
                            w_router[0], b_router[0], seq)
    pos, w_top, counts = _route(logits.T)
    slot_expert, slot_rows, n_used = _slot_tables(counts[:, 0].astype(jnp.int32), n)
    xs = _dispatch_rows(h2, pos)
    ys = _experts(xs, slot_expert, slot_rows, n_used[0], w_gate_up[0], b_gate_up[0], w_down[0], b_down[0])
    y4 = _combine_rows(ys, pos.reshape(TOP_K * n))
    out = _final(x1, y4, w_top.T, mod, final_g, seq)
    return out.reshape(b, seq, d)
```

```python
import functools
import math

import jax
import jax.numpy as jnp
from jax import lax
from jax.experimental import pallas as pl
from jax.experimental.pallas import tpu as pltpu
from jax.experimental.pallas import tpu_sc as plsc

F32 = jnp.float32
BF16 = jnp.bfloat16

CHUNK = 64
N_HEADS = 8
HEAD_DIM = 128
D_RET = N_HEADS * HEAD_DIM
D_CONV = 1024
CONV_WIDTH = 3
N_EXPERTS = 32
TOP_K = 4
SWIGLU_LIMIT = 7.0
SWIGLU_ALPHA = 1.702
ROPE_BASE = 10000.0
EPS = 1e-6

VMEM_LIMIT_BYTES = 58 * 1024 * 1024
SC_WORKERS = 32
SC_ROWS_PER_COPY = 32

MIX_BLOCK = 256
ROW_BLOCK = 128
SLOT_ROWS = 9 * ROW_BLOCK
W_PANEL = 512
RING_AHEAD = 2


def _tc_params(sem):
    return pltpu.CompilerParams(dimension_semantics=sem, vmem_limit_bytes=VMEM_LIMIT_BYTES)


def _bf16_bits(x):
    return lax.bitcast_convert_type(x.astype(BF16).astype(F32), jnp.uint32)


def _pack_bf16_pair(lo, hi):
    return (_bf16_bits(lo) >> 16) | _bf16_bits(hi)


def _unpack_bf16_pair(p):
    lo = lax.bitcast_convert_type(p << 16, F32)
    hi = lax.bitcast_convert_type(p & jnp.uint32(0xFFFF0000), F32)
    return lo, hi


def _mod_kernel(c_ref, w_ref, b_ref, o_ref):
    c = c_ref[...]
    ca = (c * jax.nn.sigmoid(c)).astype(BF16)
    o_ref[...] = jnp.dot(ca, w_ref[...].astype(BF16), preferred_element_type=F32) + b_ref[...]


def _modulation(c, w_mod, b_mod):
    b, d = c.shape
    n = w_mod.shape[1]
    tn = 1024
    c8 = jnp.zeros((8, d), F32).at[:b].set(c)
    out = pl.pallas_call(
        _mod_kernel,
        grid=(n // tn,),
        in_specs=[pl.BlockSpec((8, d), lambda j: (0, 0)),
                  pl.BlockSpec((d, tn), lambda j: (0, j)),
                  pl.BlockSpec((1, tn), lambda j: (0, j))],
        out_specs=pl.BlockSpec((8, tn), lambda j: (0, j)),
        out_shape=jax.ShapeDtypeStruct((8, n), F32),
        compiler_params=_tc_params(("arbitrary",)),
        name="mod",
    )(c8, w_mod, b_mod.reshape(1, n))
    return out[:b].reshape(b, 6, d)


def _norm_mod(x, g, scale, shift):
    y = x * lax.rsqrt(jnp.mean(x * x, axis=-1, keepdims=True) + EPS) * g
    return y * (1.0 + scale) + shift


def _inproj_kernel(x_ref, mod_ref, g_ref, w_ref, o_ref, h_ref, *, rows):
    @pl.when(pl.program_id(1) == 0)
    def _():
        g = g_ref[...]
        scale = mod_ref[0, 1:2, :]
        shift = mod_ref[0, 0:1, :]

        def body(r, carry):
            r0 = pl.multiple_of(r * rows, rows)
            h = _norm_mod(x_ref[pl.ds(r0, rows), :], g, scale, shift)
            h_ref[pl.ds(r0, rows), :] = h.astype(BF16)
            return carry

        lax.fori_loop(0, x_ref.shape[0] // rows, body, 0)

    o_ref[...] = jnp.dot(h_ref[...], w_ref[...], preferred_element_type=F32).astype(o_ref.dtype)


def _in_projection(x2, mod, g1, w_in_bf, seq):
    n, d = x2.shape
    p = w_in_bf.shape[1]
    tm, tn = min(1024, seq), 1024
    per_batch = seq // tm
    return pl.pallas_call(
        functools.partial(_inproj_kernel, rows=128),
        grid=(n // tm, p // tn),
        in_specs=[pl.BlockSpec((tm, d), lambda i, j: (i, 0)),
                  pl.BlockSpec((1, 6, d), lambda i, j: (i // per_batch, 0, 0)),
                  pl.BlockSpec((1, d), lambda i, j: (0, 0)),
                  pl.BlockSpec((d, tn), lambda i, j: (0, j))],
        out_specs=pl.BlockSpec((tm, tn), lambda i, j: (i, j)),
        out_shape=jax.ShapeDtypeStruct((n, p), BF16),
        scratch_shapes=[pltpu.VMEM((tm, d), BF16)],
        compiler_params=_tc_params(("arbitrary", "arbitrary")),
        name="inproj",
    )(x2, mod, g1.reshape(1, d), w_in_bf)


def _mixer_tables(seq, blk):
    half = HEAD_DIM // 2
    freqs = ROPE_BASE ** (-jnp.arange(half, dtype=F32) / half)
    ang = jnp.arange(seq, dtype=F32)[:, None] * freqs[None, :]
    cos = jnp.concatenate([jnp.cos(ang), jnp.cos(ang)], axis=-1)
    sin = jnp.concatenate([-jnp.sin(ang), jnp.sin(ang)], axis=-1)
    log_gamma = jnp.log1p(-jnp.exp2(-5.0 - jnp.arange(N_HEADS, dtype=F32)))
    idx = jnp.arange(blk, dtype=F32)
    dist = jnp.abs(idx[:, None] - idx[None, :])
    ck = jnp.arange(blk) // CHUNK
    visible = (ck[None, :] <= ck[:, None]).astype(F32)
    dmask = jnp.exp(log_gamma[:, None, None] * dist) * visible
    ones = jnp.ones((1, 1, HEAD_DIM), F32)
    dq = jnp.exp(log_gamma[:, None] * (idx + 1.0)[None])[:, :, None] * ones
    dkv = jnp.exp(log_gamma[:, None] * (blk - 1 - idx)[None])[:, :, None] * ones
    dblk = jnp.exp(log_gamma * blk)[:, None, None] * ones
    return cos, sin, dmask, dq, dkv, dblk


def _mixer_kernel(proj_ref, x_ref, mod_ref, cos_ref, sin_ref, dmask_ref, dq_ref, dkv_ref, dblk_ref,
                  convw_ref, wout_ref, g2_ref, wr_ref, br_ref,
                  x1_ref, h2_ref, lg_ref,
                  s_ref, zbuf_ref, y_ref, *, per_batch, n_blocks):
    blk = x_ref.shape[0]
    i = pl.program_id(0)
    cur = i % 2

    @pl.when(jnp.minimum(i, n_blocks - 1) % per_batch == 0)
    def _():
        s_ref[...] = jnp.zeros_like(s_ref)
        zbuf_ref[0:8, :] = jnp.zeros((8, D_CONV), F32)

    @pl.when(i == 0)
    def _():
        y_ref[1] = jnp.zeros(y_ref.shape[1:], BF16)

    cos = cos_ref[...]
    sin = sin_ref[...]
    k_scale = HEAD_DIM ** -0.5
    nt = (((1,), (1,)), ((), ()))
    tn = (((0,), (0,)), ((), ()))
    for h in range(N_HEADS):
        c0 = h * HEAD_DIM
        q = proj_ref[:, c0:c0 + HEAD_DIM].astype(F32)
        k = proj_ref[:, D_RET + c0:D_RET + c0 + HEAD_DIM].astype(F32)
        v = proj_ref[:, 2 * D_RET + c0:2 * D_RET + c0 + HEAD_DIM]
        g = proj_ref[:, 3 * D_RET + c0:3 * D_RET + c0 + HEAD_DIM].astype(F32)
        qr = q * cos + pltpu.roll(q, HEAD_DIM // 2, 1) * sin
        kr = (k * cos + pltpu.roll(k, HEAD_DIM // 2, 1) * sin) * k_scale
        qb = qr.astype(BF16)
        kb = kr.astype(BF16)
        scores = lax.dot_general(qb, kb, nt, preferred_element_type=F32) * dmask_ref[h]
        intra = jnp.dot(scores.astype(BF16), v, preferred_element_type=F32)
        state = s_ref[h]
        cross = jnp.dot(qb, state.astype(BF16), preferred_element_type=F32) * dq_ref[h]
        kd = (kr * dkv_ref[h]).astype(BF16)
        kv = lax.dot_general(kd, v, tn, preferred_element_type=F32)
        s_ref[h] = dblk_ref[h] * state + kv
        o = intra + cross
        mu = jnp.mean(o, axis=-1, keepdims=True)
        dev = o - mu
        var = jnp.mean(dev * dev, axis=-1, keepdims=True)
        on = dev * lax.rsqrt(var + EPS)
        y_ref[cur, :, c0:c0 + HEAD_DIM] = (g * jax.nn.sigmoid(g) * on).astype(BF16)

    cw = 256
    base = 4 * D_RET
    for cb in range(D_CONV // cw):
        lo = cb * cw
        cg = proj_ref[:, base + D_CONV + lo:base + D_CONV + lo + cw].astype(F32)
        u = proj_ref[:, base + 2 * D_CONV + lo:base + 2 * D_CONV + lo + cw].astype(F32)
        zbuf_ref[8:blk + 8, lo:lo + cw] = cg * u
    for cb in range(D_CONV // cw):
        lo = cb * cw
        z0 = zbuf_ref[8:blk + 8, lo:lo + cw]
        z1 = zbuf_ref[7:blk + 7, lo:lo + cw]
        z2 = zbuf_ref[6:blk + 6, lo:lo + cw]
        z = (convw_ref[2:3, lo:lo + cw] * z0 + convw_ref[1:2, lo:lo + cw] * z1
             + convw_ref[0:1, lo:lo + cw] * z2)
        bg = proj_ref[:, base + lo:base + lo + cw].astype(F32)
        y_ref[cur, :, D_RET + lo:D_RET + lo + cw] = (bg * z).astype(BF16)
    zbuf_ref[0:8, :] = zbuf_ref[blk:blk + 8, :]

    mix = jnp.dot(y_ref[1 - cur], wout_ref[...], preferred_element_type=F32)
    x1 = x_ref[...] + mod_ref[0, 2:3, :] * mix
    x1_ref[...] = x1
    h2 = _norm_mod(x1, g2_ref[...], mod_ref[0, 4:5, :], mod_ref[0, 3:4, :])
    half = h2.shape[1] // 2
    h2_ref[...] = _pack_bf16_pair(h2[:, :half], h2[:, half:])
    w2 = wr_ref[...]
    w_hi = w2.astype(BF16)
    w_lo = (w2 - w_hi.astype(F32)).astype(BF16)
    w_split = jnp.where(lax.broadcasted_iota(jnp.int32, w2.shape, 1) < N_EXPERTS, w_hi, w_lo)
    h_hi = h2.astype(BF16)
    h_lo = (h2 - h_hi.astype(F32)).astype(BF16)
    r = (jnp.dot(h_hi, w_split, preferred_element_type=F32)
         + jnp.dot(h_lo, w_split, preferred_element_type=F32))
    lg_ref[...] = r[:, 0:N_EXPERTS] + r[:, N_EXPERTS:2 * N_EXPERTS] + br_ref[...]


def _mixer(proj, x2, mod, conv_w, w_out_bf, g2, w_router, b_router, seq):
    n, d = x2.shape
    blk = MIX_BLOCK
    per_batch = seq // blk
    cos, sin, dmask, dq, dkv, dblk = _mixer_tables(seq, blk)
    const2 = lambda i: (0, 0)
    const3 = lambda i: (0, 0, 0)
    n_blocks = n // blk
    mixed = lambda i: jnp.minimum(i, n_blocks - 1)
    done = lambda i: jnp.maximum(i - 1, 0)
    return pl.pallas_call(
        functools.partial(_mixer_kernel, per_batch=per_batch, n_blocks=n_blocks),
        grid=(n_blocks + 1,),
        in_specs=[pl.BlockSpec((blk, proj.shape[1]), lambda i: (mixed(i), 0)),
                  pl.BlockSpec((blk, d), lambda i: (done(i), 0)),
                  pl.BlockSpec((1, 6, d), lambda i: (done(i) // per_batch, 0, 0)),
                  pl.BlockSpec((blk, HEAD_DIM), lambda i: (mixed(i) % per_batch, 0)),
                  pl.BlockSpec((blk, HEAD_DIM), lambda i: (mixed(i) % per_batch, 0)),
                  pl.BlockSpec((N_HEADS, blk, blk), const3),
                  pl.BlockSpec((N_HEADS, blk, HEAD_DIM), const3),
                  pl.BlockSpec((N_HEADS, blk, HEAD_DIM), const3),
                  pl.BlockSpec((N_HEADS, 1, HEAD_DIM), const3),
                  pl.BlockSpec((CONV_WIDTH, D_CONV), const2),
                  pl.BlockSpec((D_RET + D_CONV, d), const2),
                  pl.BlockSpec((1, d), const2),
                  pl.BlockSpec((d, 2 * N_EXPERTS), const2),
                  pl.BlockSpec((1, N_EXPERTS), const2)],
        out_specs=[pl.BlockSpec((blk, d), lambda i: (done(i), 0)),
                   pl.BlockSpec((blk, d // 2), lambda i: (done(i), 0)),
                   pl.BlockSpec((blk, N_EXPERTS), lambda i: (done(i), 0))],
        out_shape=[jax.ShapeDtypeStruct((n, d), F32),
                   jax.ShapeDtypeStruct((n, d // 2), jnp.uint32),
                   jax.ShapeDtypeStruct((n, N_EXPERTS), F32)],
        scratch_shapes=[pltpu.VMEM((N_HEADS, HEAD_DIM, HEAD_DIM), F32),
                        pltpu.VMEM((blk + 8, D_CONV), F32),
                        pltpu.VMEM((2, blk, D_RET + D_CONV), BF16)],
        compiler_params=_tc_params(("arbitrary",)),
        name="mixer",
    )(proj, x2, mod, cos, sin, dmask, dq, dkv, dblk, conv_w, w_out_bf, g2.reshape(1, d),
      jnp.concatenate([w_router, w_router], axis=1), b_router.reshape(1, N_EXPERTS))


def _top_k_block(lg_ref, tri_ref, run_ref):
    n_e, bw = lg_ref.shape

    @pl.when(pl.program_id(0) == 0)
    def _():
        run_ref[...] = jnp.zeros_like(run_ref)

    l = lg_ref[...]
    eio = lax.broadcasted_iota(jnp.int32, (n_e, bw), 0)
    vals, hots = [], []
    for _ in range(TOP_K):
        m = jnp.max(l, axis=0, keepdims=True)
        idx = jnp.min(jnp.where(l == m, eio, n_e), axis=0, keepdims=True)
        hot = eio == idx
        vals.append(m)
        hots.append(hot)
        l = jnp.where(hot, -jnp.inf, l)
    sel = sum(jnp.where(hot, 1.0, 0.0) for hot in hots)
    incl = jnp.dot(sel.astype(BF16), tri_ref[...], preferred_element_type=F32)
    run = run_ref[:, 0:1]
    rank = incl - sel + run
    run_ref[...] = jnp.broadcast_to(run + incl[:, bw - 1:bw], run_ref.shape)
    return vals, hots, rank


def _count_kernel(lg_ref, tri_ref, cnt_ref, run_ref):
    _top_k_block(lg_ref, tri_ref, run_ref)
    cnt_ref[...] = run_ref[...]


def _route_kernel(lg_ref, tri_ref, low_ref, tot_ref, pos_ref, w_ref, run_ref, *, max_slots):
    vals, hots, rank = _top_k_block(lg_ref, tri_ref, run_ref)
    ex = [jnp.exp(v - vals[0]) for v in vals]
    den = ex[0] + ex[1] + ex[2] + ex[3]
    for k in range(TOP_K):
        w_ref[k:k + 1, :] = ex[k] / den
    tot = tot_ref[...]
    n_slots = sum(jnp.where(tot > float(m * SLOT_ROWS), 1.0, 0.0) for m in range(max_slots))
    start = jnp.dot(low_ref[...], n_slots.astype(BF16), preferred_element_type=F32)
    dest = start[:, 0:1] * float(SLOT_ROWS) + rank
    for k in range(TOP_K):
        pos_ref[k:k + 1, :] = jnp.sum(jnp.where(hots[k], dest, 0.0), axis=0,
                                      keepdims=True).astype(jnp.int32)


def _route(logits_t):
    n_e, n = logits_t.shape
    bw = 512
    max_slots = -(-n // SLOT_ROWS)
    tri = (jnp.arange(bw)[:, None] <= jnp.arange(bw)[None, :]).astype(BF16)
    low = (jnp.arange(n_e)[None, :] < jnp.arange(n_e)[:, None]).astype(BF16)
    lg_spec = pl.BlockSpec((n_e, bw), lambda c: (0, c))
    tri_spec = pl.BlockSpec((bw, bw), lambda c: (0, 0))
    cnt_spec = pl.BlockSpec((n_e, 128), lambda c: (0, 0))
    run = pltpu.VMEM((n_e, 128), F32)
    counts = pl.pallas_call(
        _count_kernel,
        grid=(n // bw,),
        in_specs=[lg_spec, tri_spec],
        out_specs=cnt_spec,
        out_shape=jax.ShapeDtypeStruct((n_e, 128), F32),
        scratch_shapes=[run],
        compiler_params=_tc_params(("arbitrary",)),
        name="route_count",
    )(logits_t, tri)
    pos, w_top = pl.pallas_call(
        functools.partial(_route_kernel, max_slots=max_slots),
        grid=(n // bw,),
        in_specs=[lg_spec, tri_spec, pl.BlockSpec((n_e, n_e), lambda c: (0, 0)), cnt_spec],
        out_specs=[pl.BlockSpec((TOP_K, bw), lambda c: (0, c)),
                   pl.BlockSpec((TOP_K, bw), lambda c: (0, c))],
        out_shape=[jax.ShapeDtypeStruct((TOP_K, n), jnp.int32),
                   jax.ShapeDtypeStruct((TOP_K, n), F32)],
        scratch_shapes=[run],
        compiler_params=_tc_params(("arbitrary",)),
        name="route",
    )(logits_t, tri, low, counts)
    return pos, w_top, counts


def _sc_mesh():
    return plsc.VectorSubcoreMesh(core_axis_name="c", subcore_axis_name="s")


def _sc_worker_id():
    return lax.axis_index("s") * lax.axis_size("c") + lax.axis_index("c")


def _dispatch_rows(h2, pos):
    n, d = h2.shape
    ch = SC_ROWS_PER_COPY
    per_w = n // SC_WORKERS
    n_ch = per_w // ch
    total_rows = _max_slots(n) * SLOT_ROWS
    idx = pos.reshape(TOP_K, SC_WORKERS, n_ch, ch).transpose(1, 2, 0, 3).reshape(
        SC_WORKERS, n_ch * TOP_K, ch)

    @functools.partial(
        pl.kernel, mesh=_sc_mesh(),
        out_type=jax.ShapeDtypeStruct((total_rows, d), h2.dtype),
        scratch_types=[pltpu.VMEM((n_ch * TOP_K, ch), jnp.int32),
                       pltpu.VMEM((ch, d), h2.dtype), pltpu.VMEM((ch, d), h2.dtype),
                       pltpu.SemaphoreType.DMA, pltpu.SemaphoreType.DMA,
                       pltpu.SemaphoreType.DMA, pltpu.SemaphoreType.DMA])
    def scatter(x_hbm, idx_hbm, o_hbm, idx_v, rows0, rows1, rsem0, rsem1, ssem0, ssem1):
        wid = _sc_worker_id()
        base = wid * per_w
        pltpu.sync_copy(idx_hbm.at[wid], idx_v)
        rows, rsem, ssem = (rows0, rows1), (rsem0, rsem1), (ssem0, ssem1)

        def read(c, b):
            r0 = pl.multiple_of(base + c * ch, ch)
            return pltpu.make_async_copy(x_hbm.at[pl.ds(r0, ch)], rows[b], rsem[b])

        def send(c, k, b):
            return pltpu.make_async_copy(rows[b], o_hbm.at[idx_v.at[c * TOP_K + k]], ssem[b])

        read(0, 0).start()

        @pl.loop(0, n_ch, step=2)
        def _(c):
            read(c, 0).wait()

            @pl.when(c > 0)
            def _():
                for k in range(TOP_K):
                    send(c - 1, k, 1).wait()

            read(c + 1, 1).start()
            for k in range(TOP_K):
                send(c, k, 0).start()
            read(c + 1, 1).wait()
            for k in range(TOP_K):
                send(c, k, 0).wait()

            @pl.when(c + 2 < n_ch)
            def _():
                read(c + 2, 0).start()

            for k in range(TOP_K):
                send(c + 1, k, 1).start()

        for k in range(TOP_K):
            send(n_ch - 1, k, 1).wait()

    return scatter(h2, idx)


def _combine_rows(ys, pos_flat):
    n = pos_flat.shape[0]
    d = ys.shape[1]
    ch = SC_ROWS_PER_COPY
    per_w = n // SC_WORKERS
    n_ch = per_w // ch
    idx = pos_flat.reshape(SC_WORKERS, n_ch, ch)

    @functools.partial(
        pl.kernel, mesh=_sc_mesh(),
        out_type=jax.ShapeDtypeStruct((n, d), ys.dtype),
        scratch_types=[pltpu.VMEM((n_ch, ch), jnp.int32),
                       pltpu.VMEM((ch, d), ys.dtype), pltpu.VMEM((ch, d), ys.dtype),
                       pltpu.SemaphoreType.DMA, pltpu.SemaphoreType.DMA,
                       pltpu.SemaphoreType.DMA, pltpu.SemaphoreType.DMA])
    def gather(y_hbm, idx_hbm, o_hbm, idx_v, rows0, rows1, gsem0, gsem1, wsem0, wsem1):
        wid = _sc_worker_id()
        base = wid * per_w
        pltpu.sync_copy(idx_hbm.at[wid], idx_v)
        rows, gsem, wsem = (rows0, rows1), (gsem0, gsem1), (wsem0, wsem1)

        def fetch(c, b):
            return pltpu.make_async_copy(y_hbm.at[idx_v.at[c]], rows[b], gsem[b])

        def write(c, b):
            r0 = pl.multiple_of(base + c * ch, ch)
            return pltpu.make_async_copy(rows[b], o_hbm.at[pl.ds(r0, ch)], wsem[b])

        fetch(0, 0).start()

        @pl.loop(0, n_ch, step=2)
        def _(c):
            fetch(c, 0).wait()

            @pl.when(c > 0)
            def _():
                write(c - 1, 1).wait()

            fetch(c + 1, 1).start()
            write(c, 0).start()
            fetch(c + 1, 1).wait()
            write(c, 0).wait()

            @pl.when(c + 2 < n_ch)
            def _():
                fetch(c + 2, 0).start()

            write(c + 1, 1).start()

        write(n_ch - 1, 1).wait()

    return gather(ys, idx)


def _max_slots(n_tokens):
    return N_EXPERTS + (n_tokens * TOP_K) // SLOT_ROWS


def _expert_kernel(se_ref, sr_ref,
                   xs_ref, wgu_hbm, wd_hbm, bgu_ref, bd_ref,
                   ys_ref,
                   xb_ref, act_ref, wbuf_ref, sem_ref, *, n_a, n_b):
    s = pl.program_id(0)
    rows = sr_ref[s]
    n_blk = (rows + ROW_BLOCK - 1) // ROW_BLOCK
    tw = wbuf_ref.shape[2] // 2
    d_ff = act_ref.shape[1]
    half_d = ys_ref.shape[1]

    def panel_copy(src, buf, p):
        return pltpu.make_async_copy(src, wbuf_ref.at[buf, :, pl.ds(p * tw, tw)], sem_ref.at[buf, p])

    def gate_up_panels(e, t, buf):
        c0 = pl.multiple_of(t * tw, tw)
        return (panel_copy(wgu_hbm.at[e, :, pl.ds(c0, tw)], buf, 0),
                panel_copy(wgu_hbm.at[e, :, pl.ds(d_ff + c0, tw)], buf, 1))

    def down_panels(e, u, buf):
        c0 = pl.multiple_of(u * tw, tw)
        return (panel_copy(wd_hbm.at[e, :, pl.ds(c0, tw)], buf, 0),
                panel_copy(wd_hbm.at[e, :, pl.ds(half_d + c0, tw)], buf, 1))

    def start(panels):
        for cp in panels:
            cp.start()

    def wait(panels):
        for cp in panels:
            cp.wait()

    expert = se_ref[s]

    def for_row_blocks(fn):
        n_big = n_blk // 8

        def body(rb, carry):
            fn(pl.multiple_of(rb * 8 * ROW_BLOCK, 8 * ROW_BLOCK), 8 * ROW_BLOCK)
            return carry

        lax.fori_loop(0, n_big, body, 0)
        for size in (4, 2, 1):
            @pl.when((n_blk // size) % 2 == 1)
            def _(size=size):
                first = (n_blk // (2 * size)) * 2 * size * ROW_BLOCK
                fn(pl.multiple_of(first, 2 * size * ROW_BLOCK), size * ROW_BLOCK)

    @pl.when(s == 0)
    def _():
        for k in range(RING_AHEAD):
            start(gate_up_panels(expert, k, k))

    def unpack(rb, carry):
        r0 = pl.multiple_of(rb * ROW_BLOCK, ROW_BLOCK)
        keep = (r0 + lax.broadcasted_iota(jnp.int32, (ROW_BLOCK, 1), 0)) < rows
        lo, hi = _unpack_bf16_pair(xs_ref[pl.ds(r0, ROW_BLOCK), :])
        xb_ref[pl.ds(r0, ROW_BLOCK), 0:half_d] = jnp.where(keep, lo, 0.0).astype(BF16)
        xb_ref[pl.ds(r0, ROW_BLOCK), half_d:2 * half_d] = jnp.where(keep, hi, 0.0).astype(BF16)
        return carry

    lax.fori_loop(0, n_blk, unpack, 0)

    ring = RING_AHEAD + 1

    def gate_up_tile(t, carry):
        buf = t % ring
        wait(gate_up_panels(expert, t, buf))
        nxt = t + RING_AHEAD

        @pl.when(nxt < n_a)
        def _():
            start(gate_up_panels(expert, nxt, nxt % ring))

        @pl.when(nxt >= n_a)
        def _():
            start(down_panels(expert, nxt - n_a, nxt % ring))

        col = pl.multiple_of(t * tw, tw)
        b_gate = bgu_ref[0, :, pl.ds(col, tw)]
        b_up = bgu_ref[0, :, pl.ds(d_ff + col, tw)]

        def gate_up(r0, m):
            gu = jnp.dot(xb_ref[pl.ds(r0, m), :], wbuf_ref[buf].astype(BF16), preferred_element_type=F32)
            gate = jnp.minimum(gu[:, 0:tw] + b_gate, SWIGLU_LIMIT)
            up = jnp.clip(gu[:, tw:2 * tw] + b_up, -SWIGLU_LIMIT, SWIGLU_LIMIT)
            act = (up + 1.0) * (gate * jax.nn.sigmoid(SWIGLU_ALPHA * gate))
            act_ref[pl.ds(r0, m), pl.ds(col, tw)] = act.astype(BF16)

        for_row_blocks(gate_up)
        return carry

    lax.fori_loop(0, n_a, gate_up_tile, 0)

    def down_tile(u, carry):
        buf = (n_a + u) % ring
        wait(down_panels(expert, u, buf))
        nxt = u + RING_AHEAD

        @pl.when(s + 1 < pl.num_programs(0))
        def _():
            start(gate_up_panels(se_ref[s + 1], nxt - n_b, (nxt - n_b) % ring))

        col = pl.multiple_of(u * tw, tw)
        b_lo = bd_ref[0, :, pl.ds(col, tw)]
        b_hi = bd_ref[0, :, pl.ds(half_d + col, tw)]

        def down(r0, m):
            y = jnp.dot(act_ref[pl.ds(r0, m), :], wbuf_ref[buf].astype(BF16), preferred_element_type=F32)
            ys_ref[pl.ds(r0, m), pl.ds(col, tw)] = _pack_bf16_pair(y[:, 0:tw] + b_lo, y[:, tw:2 * tw] + b_hi)

        for_row_blocks(down)
        return carry

    lax.fori_loop(0, n_b, down_tile, 0)

    def fill(rb, carry):
        r0 = pl.multiple_of(rb * ROW_BLOCK, ROW_BLOCK)
        ys_ref[pl.ds(r0, ROW_BLOCK), :] = jnp.zeros((ROW_BLOCK, half_d), jnp.uint32)
        return carry

    lax.fori_loop(n_blk, SLOT_ROWS // ROW_BLOCK, fill, 0)


def _experts(xs, slot_expert, slot_rows, n_used, w_gate_up, b_gate_up, w_down, b_down):
    n_e, d, two_ff = w_gate_up.shape
    d_ff = two_ff // 2
    assert d == d_ff, "the weight ring holds (rows, W_PANEL) panels of both projections"
    tw = W_PANEL
    n_a, n_b = d_ff // tw, (d // 2) // tw
    ring = RING_AHEAD + 1
    assert (n_a + n_b) % ring == 0, "ring slot of tile j must repeat from one expert slot to the next"
    assert n_b == RING_AHEAD <= n_a, "the look-ahead from a down tile must land in the next slot's gate/up tiles"
    total_rows = xs.shape[0]

    grid_spec = pltpu.PrefetchScalarGridSpec(
        num_scalar_prefetch=2,
        grid=(n_used,),
        in_specs=[
            pl.BlockSpec((SLOT_ROWS, d // 2), lambda s, se, sr: (s, 0)),
            pl.BlockSpec(memory_space=pl.ANY),
            pl.BlockSpec(memory_space=pl.ANY),
            pl.BlockSpec((1, 1, two_ff), lambda s, se, sr: (se[s], 0, 0)),
            pl.BlockSpec((1, 1, d), lambda s, se, sr: (se[s], 0, 0)),
        ],
        out_specs=pl.BlockSpec((SLOT_ROWS, d // 2), lambda s, se, sr: (s, 0)),
        scratch_shapes=[pltpu.VMEM((SLOT_ROWS, d), BF16),
                        pltpu.VMEM((SLOT_ROWS, d_ff), BF16),
                        pltpu.VMEM((ring, d, 2 * tw), F32),
                        pltpu.SemaphoreType.DMA((ring, 2))],
    )
    return pl.pallas_call(
        functools.partial(_expert_kernel, n_a=n_a, n_b=n_b),
        grid_spec=grid_spec,
        out_shape=jax.ShapeDtypeStruct((total_rows, d // 2), jnp.uint32),
        compiler_params=_tc_params(("arbitrary",)),
        name="experts",
    )(slot_expert, slot_rows, xs, w_gate_up, w_down,
      b_gate_up.reshape(n_e, 1, two_ff), b_down.reshape(n_e, 1, d))


def _slot_tables(counts, n_tokens):
    max_slots = _max_slots(n_tokens)
    n_slots = (counts + SLOT_ROWS - 1) // SLOT_ROWS
    slot_end = jnp.cumsum(n_slots)
    slot_start = slot_end - n_slots
    n_used = slot_end[-1]
    sid = jnp.arange(max_slots, dtype=jnp.int32)
    expert = jnp.minimum(jnp.sum(sid[:, None] >= slot_end[None, :], axis=1), N_EXPERTS - 1).astype(jnp.int32)
    local = sid - slot_start[expert]
    rows = jnp.clip(counts[expert] - local * SLOT_ROWS, 0, SLOT_ROWS)
    rows = jnp.where(sid < n_used, rows, 0).astype(jnp.int32)
    return expert, rows, n_used.reshape(1).astype(jnp.int32)


def _final_kernel(x1_ref, y0_ref, y1_ref, y2_ref, y3_ref, w_ref, mod_ref, g_ref, o_ref):
    half = y0_ref.shape[1]
    d = 2 * half
    moe_lo = moe_hi = None
    for k, y_ref in enumerate((y0_ref, y1_ref, y2_ref, y3_ref)):
        lo, hi = _unpack_bf16_pair(y_ref[...])
        wk = w_ref[:, k:k + 1]
        moe_lo = wk * lo if k == 0 else moe_lo + wk * lo
        moe_hi = wk * hi if k == 0 else moe_hi + wk * hi
    x_lo = x1_ref[:, 0:half] + mod_ref[0, 5:6, 0:half] * moe_lo
    x_hi = x1_ref[:, half:d] + mod_ref[0, 5:6, half:d] * moe_hi
    ss = jnp.sum(x_lo * x_lo, axis=-1, keepdims=True) + jnp.sum(x_hi * x_hi, axis=-1, keepdims=True)
    r = lax.rsqrt(ss / d + EPS)
    o_ref[:, 0:half] = x_lo * r * g_ref[:, 0:half]
    o_ref[:, half:d] = x_hi * r * g_ref[:, half:d]


def _final(x1, y4, w_tok, mod, final_g, seq):
    n, d = x1.shape
    tm = 256
    per_batch = seq // tm
    n_i = n // tm
    y_specs = [pl.BlockSpec((tm, d // 2), functools.partial(lambda i, k: (k * n_i + i, 0), k=k))
               for k in range(TOP_K)]
    return pl.pallas_call(
        _final_kernel,
        grid=(n_i,),
        in_specs=[pl.BlockSpec((tm, d), lambda i: (i, 0))] + y_specs + [
                  pl.BlockSpec((tm, TOP_K), lambda i: (i, 0)),
                  pl.BlockSpec((1, 6, d), lambda i: (i // per_batch, 0, 0)),
                  pl.BlockSpec((1, d), lambda i: (0, 0))],
        out_specs=pl.BlockSpec((tm, d), lambda i: (i, 0)),
        out_shape=jax.ShapeDtypeStruct((n, d), F32),
        compiler_params=_tc_params(("arbitrary",)),
        name="final",
    )(x1, y4, y4, y4, y4, w_tok, mod, final_g.reshape(1, d))


def kernel(x, c, norm1_g, w_mod, b_mod, w_in, conv_w, w_out, norm2_g, w_router, b_router,
           w_gate_up, b_gate_up, w_down, b_down, final_g):
    b, seq, d = x.shape
    n = b * seq
    x2 = x.reshape(n, d)
    mod = _modulation(c, w_mod[0], b_mod[0])
    proj = _in_projection(x2, mod, norm1_g[0], w_in[0].astype(BF16), seq)
    x1, h2, logits = _mixer(proj, x2, mod, conv_w[0], w_out[0].astype(BF16), norm2_g[0],
                            w_router[0], b_router[0], seq)
    pos, w_top, counts = _route(logits.T)
    slot_expert, slot_rows, n_used = _slot_tables(counts[:, 0].astype(jnp.int32), n)
    xs = _dispatch_rows(h2, pos)
    ys = _experts(xs, slot_expert, slot_rows, n_used[0], w_gate_up[0], b_gate_up[0], w_down[0], b_down[0])
    y4 = _combine_rows(ys, pos.reshape(TOP_K * n))
    out = _final(x1, y4, w_top.T, mod, final_g, seq)
    return out.reshape(b, seq, d)
```

```python
import functools
import math

import jax
import jax.numpy as jnp
from jax import lax
from jax.experimental import pallas as pl
from jax.experimental.pallas import tpu as pltpu
from jax.experimental.pallas import tpu_sc as plsc

F32 = jnp.float32
BF16 = jnp.bfloat16

CHUNK = 64
N_HEADS = 8
HEAD_DIM = 128
D_RET = N_HEADS * HEAD_DIM
D_CONV = 1024
CONV_WIDTH = 3
N_EXPERTS = 32
TOP_K = 4
SWIGLU_LIMIT = 7.0
SWIGLU_ALPHA = 1.702
ROPE_BASE = 10000.0
EPS = 1e-6

VMEM_LIMIT_BYTES = 58 * 1024 * 1024
SC_WORKERS = 32
SC_ROWS_PER_COPY = 32

MIX_BLOCK = 256
ROW_BLOCK = 128
SLOT_ROWS = 9 * ROW_BLOCK
W_PANEL = 512
RING_AHEAD = 2
COMBINE_PARTS = 2


def _tc_params(sem):
    return pltpu.CompilerParams(dimension_semantics=sem, vmem_limit_bytes=VMEM_LIMIT_BYTES)


def _bf16_bits(x):
    return lax.bitcast_convert_type(x.astype(BF16).astype(F32), jnp.uint32)


def _pack_bf16_pair(lo, hi):
    return (_bf16_bits(lo) >> 16) | _bf16_bits(hi)


def _unpack_bf16_pair(p):
    lo = lax.bitcast_convert_type(p << 16, F32)
    hi = lax.bitcast_convert_type(p & jnp.uint32(0xFFFF0000), F32)
    return lo, hi


def _mod_kernel(c_ref, w_ref, b_ref, o_ref):
    c = c_ref[...]
    ca = (c * jax.nn.sigmoid(c)).astype(BF16)
    o_ref[...] = jnp.dot(ca, w_ref[...].astype(BF16), preferred_element_type=F32) + b_ref[...]


def _modulation(c, w_mod, b_mod):
    b, d = c.shape
    n = w_mod.shape[1]
    tn = 1024
    c8 = jnp.zeros((8, d), F32).at[:b].set(c)
    out = pl.pallas_call(
        _mod_kernel,
        grid=(n // tn,),
        in_specs=[pl.BlockSpec((8, d), lambda j: (0, 0)),
                  pl.BlockSpec((d, tn), lambda j: (0, j)),
                  pl.BlockSpec((1, tn), lambda j: (0, j))],
        out_specs=pl.BlockSpec((8, tn), lambda j: (0, j)),
        out_shape=jax.ShapeDtypeStruct((8, n), F32),
        compiler_params=_tc_params(("arbitrary",)),
        name="mod",
    )(c8, w_mod, b_mod.reshape(1, n))
    return out[:b].reshape(b, 6, d)


def _norm_mod(x, g, scale, shift):
    y = x * lax.rsqrt(jnp.mean(x * x, axis=-1, keepdims=True) + EPS) * g
    return y * (1.0 + scale) + shift


def _inproj_kernel(x_ref, mod_ref, g_ref, w_ref, o_ref, h_ref, *, rows):
    @pl.when(pl.program_id(1) == 0)
    def _():
        g = g_ref[...]
        scale = mod_ref[0, 1:2, :]
        shift = mod_ref[0, 0:1, :]

        def body(r, carry):
            r0 = pl.multiple_of(r * rows, rows)
            h = _norm_mod(x_ref[pl.ds(r0, rows), :], g, scale, shift)
            h_ref[pl.ds(r0, rows), :] = h.astype(BF16)
            return carry

        lax.fori_loop(0, x_ref.shape[0] // rows, body, 0)

    o_ref[...] = jnp.dot(h_ref[...], w_ref[...], preferred_element_type=F32).astype(o_ref.dtype)


def _in_projection(x2, mod, g1, w_in_bf, seq):
    n, d = x2.shape
    p = w_in_bf.shape[1]
    tm, tn = min(1024, seq), 1792
    per_batch = seq // tm
    return pl.pallas_call(
        functools.partial(_inproj_kernel, rows=128),
        grid=(n // tm, p // tn),
        in_specs=[pl.BlockSpec((tm, d), lambda i, j: (i, 0)),
                  pl.BlockSpec((1, 6, d), lambda i, j: (i // per_batch, 0, 0)),
                  pl.BlockSpec((1, d), lambda i, j: (0, 0)),
                  pl.BlockSpec((d, tn), lambda i, j: (0, j))],
        out_specs=pl.BlockSpec((tm, tn), lambda i, j: (i, j)),
        out_shape=jax.ShapeDtypeStruct((n, p), BF16),
        scratch_shapes=[pltpu.VMEM((tm, d), BF16)],
        compiler_params=_tc_params(("arbitrary", "arbitrary")),
        name="inproj",
    )(x2, mod, g1.reshape(1, d), w_in_bf)


def _mixer_tables(seq, blk):
    half = HEAD_DIM // 2
    freqs = ROPE_BASE ** (-jnp.arange(half, dtype=F32) / half)
    ang = jnp.arange(seq, dtype=F32)[:, None] * freqs[None, :]
    cos = jnp.concatenate([jnp.cos(ang), jnp.cos(ang)], axis=-1)
    sin = jnp.concatenate([-jnp.sin(ang), jnp.sin(ang)], axis=-1)
    log_gamma = jnp.log1p(-jnp.exp2(-5.0 - jnp.arange(N_HEADS, dtype=F32)))
    idx = jnp.arange(blk, dtype=F32)
    dist = jnp.abs(idx[:, None] - idx[None, :])
    ck = jnp.arange(blk) // CHUNK
    visible = (ck[None, :] <= ck[:, None]).astype(F32)
    dmask = jnp.exp(log_gamma[:, None, None] * dist) * visible
    ones = jnp.ones((1, 1, HEAD_DIM), F32)
    dq = jnp.exp(log_gamma[:, None] * (idx + 1.0)[None])[:, :, None] * ones
    dkv = jnp.exp(log_gamma[:, None] * (blk - 1 - idx)[None])[:, :, None] * ones
    dblk = jnp.exp(log_gamma * blk)[:, None, None] * ones
    return cos, sin, dmask, dq, dkv, dblk


def _mixer_kernel(proj_ref, x_ref, mod_ref, cos_ref, sin_ref, dmask_ref, dq_ref, dkv_ref, dblk_ref,
                  convw_ref, wout_ref, g2_ref, wr_ref, br_ref,
                  x1_ref, h2_ref, lg_ref,
                  s_ref, zbuf_ref, y_ref, *, per_batch, n_blocks):
    blk = x_ref.shape[0]
    i = pl.program_id(0)
    cur = i % 2

    @pl.when(jnp.minimum(i, n_blocks - 1) % per_batch == 0)
    def _():
        s_ref[...] = jnp.zeros_like(s_ref)
        zbuf_ref[0:8, :] = jnp.zeros((8, D_CONV), F32)

    @pl.when(i == 0)
    def _():
        y_ref[1] = jnp.zeros(y_ref.shape[1:], BF16)

    cos = cos_ref[...]
    sin = sin_ref[...]
    k_scale = HEAD_DIM ** -0.5
    nt = (((1,), (1,)), ((), ()))
    tn = (((0,), (0,)), ((), ()))
    for h in range(N_HEADS):
        c0 = h * HEAD_DIM
        q = proj_ref[:, c0:c0 + HEAD_DIM].astype(F32)
        k = proj_ref[:, D_RET + c0:D_RET + c0 + HEAD_DIM].astype(F32)
        v = proj_ref[:, 2 * D_RET + c0:2 * D_RET + c0 + HEAD_DIM]
        g = proj_ref[:, 3 * D_RET + c0:3 * D_RET + c0 + HEAD_DIM].astype(F32)
        qr = q * cos + pltpu.roll(q, HEAD_DIM // 2, 1) * sin
        kr = (k * cos + pltpu.roll(k, HEAD_DIM // 2, 1) * sin) * k_scale
        qb = qr.astype(BF16)
        kb = kr.astype(BF16)
        scores = lax.dot_general(qb, kb, nt, preferred_element_type=F32) * dmask_ref[h]
        intra = jnp.dot(scores.astype(BF16), v, preferred_element_type=F32)
        state = s_ref[h]
        cross = jnp.dot(qb, state.astype(BF16), preferred_element_type=F32) * dq_ref[h]
        kd = (kr * dkv_ref[h]).astype(BF16)
        kv = lax.dot_general(kd, v, tn, preferred_element_type=F32)
        s_ref[h] = dblk_ref[h] * state + kv
        o = intra + cross
        mu = jnp.mean(o, axis=-1, keepdims=True)
        dev = o - mu
        var = jnp.mean(dev * dev, axis=-1, keepdims=True)
        on = dev * lax.rsqrt(var + EPS)
        y_ref[cur, :, c0:c0 + HEAD_DIM] = (g * jax.nn.sigmoid(g) * on).astype(BF16)

    cw = 256
    base = 4 * D_RET
    for cb in range(D_CONV // cw):
        lo = cb * cw
        cg = proj_ref[:, base + D_CONV + lo:base + D_CONV + lo + cw].astype(F32)
        u = proj_ref[:, base + 2 * D_CONV + lo:base + 2 * D_CONV + lo + cw].astype(F32)
        zbuf_ref[8:blk + 8, lo:lo + cw] = cg * u
    for cb in range(D_CONV // cw):
        lo = cb * cw
        z0 = zbuf_ref[8:blk + 8, lo:lo + cw]
        z1 = zbuf_ref[7:blk + 7, lo:lo + cw]
        z2 = zbuf_ref[6:blk + 6, lo:lo + cw]
        z = (convw_ref[2:3, lo:lo + cw] * z0 + convw_ref[1:2, lo:lo + cw] * z1
             + convw_ref[0:1, lo:lo + cw] * z2)
        bg = proj_ref[:, base + lo:base + lo + cw].astype(F32)
        y_ref[cur, :, D_RET + lo:D_RET + lo + cw] = (bg * z).astype(BF16)
    zbuf_ref[0:8, :] = zbuf_ref[blk:blk + 8, :]

    mix = jnp.dot(y_ref[1 - cur], wout_ref[...], preferred_element_type=F32)
    x1 = x_ref[...] + mod_ref[0, 2:3, :] * mix
    x1_ref[...] = x1
    h2 = _norm_mod(x1, g2_ref[...], mod_ref[0, 4:5, :], mod_ref[0, 3:4, :])
    half = h2.shape[1] // 2
    h2_ref[...] = _pack_bf16_pair(h2[:, :half], h2[:, half:])
    w2 = wr_ref[...]
    w_hi = w2.astype(BF16)
    w_lo = (w2 - w_hi.astype(F32)).astype(BF16)
    w_split = jnp.where(lax.broadcasted_iota(jnp.int32, w2.shape, 1) < N_EXPERTS, w_hi, w_lo)
    h_hi = h2.astype(BF16)
    h_lo = (h2 - h_hi.astype(F32)).astype(BF16)
    r = (jnp.dot(h_hi, w_split, preferred_element_type=F32)
         + jnp.dot(h_lo, w_split, preferred_element_type=F32))
    lg_ref[...] = r[:, 0:N_EXPERTS] + r[:, N_EXPERTS:2 * N_EXPERTS] + br_ref[...]


def _mixer(proj, x2, mod, conv_w, w_out_bf, g2, w_router, b_router, seq):
    n, d = x2.shape
    blk = MIX_BLOCK
    per_batch = seq // blk
    cos, sin, dmask, dq, dkv, dblk = _mixer_tables(seq, blk)
    const2 = lambda i: (0, 0)
    const3 = lambda i: (0, 0, 0)
    n_blocks = n // blk
    mixed = lambda i: jnp.minimum(i, n_blocks - 1)
    done = lambda i: jnp.maximum(i - 1, 0)
    return pl.pallas_call(
        functools.partial(_mixer_kernel, per_batch=per_batch, n_blocks=n_blocks),
        grid=(n_blocks + 1,),
        in_specs=[pl.BlockSpec((blk, proj.shape[1]), lambda i: (mixed(i), 0)),
                  pl.BlockSpec((blk, d), lambda i: (done(i), 0)),
                  pl.BlockSpec((1, 6, d), lambda i: (done(i) // per_batch, 0, 0)),
                  pl.BlockSpec((blk, HEAD_DIM), lambda i: (mixed(i) % per_batch, 0)),
                  pl.BlockSpec((blk, HEAD_DIM), lambda i: (mixed(i) % per_batch, 0)),
                  pl.BlockSpec((N_HEADS, blk, blk), const3),
                  pl.BlockSpec((N_HEADS, blk, HEAD_DIM), const3),
                  pl.BlockSpec((N_HEADS, blk, HEAD_DIM), const3),
                  pl.BlockSpec((N_HEADS, 1, HEAD_DIM), const3),
                  pl.BlockSpec((CONV_WIDTH, D_CONV), const2),
                  pl.BlockSpec((D_RET + D_CONV, d), const2),
                  pl.BlockSpec((1, d), const2),
                  pl.BlockSpec((d, 2 * N_EXPERTS), const2),
                  pl.BlockSpec((1, N_EXPERTS), const2)],
        out_specs=[pl.BlockSpec((blk, d), lambda i: (done(i), 0)),
                   pl.BlockSpec((blk, d // 2), lambda i: (done(i), 0)),
                   pl.BlockSpec((blk, N_EXPERTS), lambda i: (done(i), 0))],
        out_shape=[jax.ShapeDtypeStruct((n, d), F32),
                   jax.ShapeDtypeStruct((n, d // 2), jnp.uint32),
                   jax.ShapeDtypeStruct((n, N_EXPERTS), F32)],
        scratch_shapes=[pltpu.VMEM((N_HEADS, HEAD_DIM, HEAD_DIM), F32),
                        pltpu.VMEM((blk + 8, D_CONV), F32),
                        pltpu.VMEM((2, blk, D_RET + D_CONV), BF16)],
        compiler_params=_tc_params(("arbitrary",)),
        name="mixer",
    )(proj, x2, mod, cos, sin, dmask, dq, dkv, dblk, conv_w, w_out_bf, g2.reshape(1, d),
      jnp.concatenate([w_router, w_router], axis=1), b_router.reshape(1, N_EXPERTS))


def _top_k_block(lg_ref, tri_ref, run_ref):
    n_e, bw = lg_ref.shape

    @pl.when(pl.program_id(0) == 0)
    def _():
        run_ref[...] = jnp.zeros_like(run_ref)

    l = lg_ref[...]
    eio = lax.broadcasted_iota(jnp.int32, (n_e, bw), 0)
    vals, hots = [], []
    for _ in range(TOP_K):
        m = jnp.max(l, axis=0, keepdims=True)
        idx = jnp.min(jnp.where(l == m, eio, n_e), axis=0, keepdims=True)
        hot = eio == idx
        vals.append(m)
        hots.append(hot)
        l = jnp.where(hot, -jnp.inf, l)
    sel = sum(jnp.where(hot, 1.0, 0.0) for hot in hots)
    incl = jnp.dot(sel.astype(BF16), tri_ref[...], preferred_element_type=F32)
    run = run_ref[:, 0:1]
    rank = incl - sel + run
    run_ref[...] = jnp.broadcast_to(run + incl[:, bw - 1:bw], run_ref.shape)
    return vals, hots, rank


def _count_kernel(lg_ref, tri_ref, cnt_ref, run_ref):
    _top_k_block(lg_ref, tri_ref, run_ref)
    cnt_ref[...] = run_ref[...]


def _route_kernel(lg_ref, tri_ref, low_ref, tot_ref, pos_ref, w_ref, run_ref, *, max_slots):
    vals, hots, rank = _top_k_block(lg_ref, tri_ref, run_ref)
    ex = [jnp.exp(v - vals[0]) for v in vals]
    den = ex[0] + ex[1] + ex[2] + ex[3]
    for k in range(TOP_K):
        w_ref[k:k + 1, :] = ex[k] / den
    tot = tot_ref[...]
    n_slots = sum(jnp.where(tot > float(m * SLOT_ROWS), 1.0, 0.0) for m in range(max_slots))
    start = jnp.dot(low_ref[...], n_slots.astype(BF16), preferred_element_type=F32)
    dest = start[:, 0:1] * float(SLOT_ROWS) + rank
    for k in range(TOP_K):
        pos_ref[k:k + 1, :] = jnp.sum(jnp.where(hots[k], dest, 0.0), axis=0,
                                      keepdims=True).astype(jnp.int32)


def _route(logits_t):
    n_e, n = logits_t.shape
    bw = 512
    max_slots = -(-n // SLOT_ROWS)
    tri = (jnp.arange(bw)[:, None] <= jnp.arange(bw)[None, :]).astype(BF16)
    low = (jnp.arange(n_e)[None, :] < jnp.arange(n_e)[:, None]).astype(BF16)
    lg_spec = pl.BlockSpec((n_e, bw), lambda c: (0, c))
    tri_spec = pl.BlockSpec((bw, bw), lambda c: (0, 0))
    cnt_spec = pl.BlockSpec((n_e, 128), lambda c: (0, 0))
    run = pltpu.VMEM((n_e, 128), F32)
    counts = pl.pallas_call(
        _count_kernel,
        grid=(n // bw,),
        in_specs=[lg_spec, tri_spec],
        out_specs=cnt_spec,
        out_shape=jax.ShapeDtypeStruct((n_e, 128), F32),
        scratch_shapes=[run],
        compiler_params=_tc_params(("arbitrary",)),
        name="route_count",
    )(logits_t, tri)
    pos, w_top = pl.pallas_call(
        functools.partial(_route_kernel, max_slots=max_slots),
        grid=(n // bw,),
        in_specs=[lg_spec, tri_spec, pl.BlockSpec((n_e, n_e), lambda c: (0, 0)), cnt_spec],
        out_specs=[pl.BlockSpec((TOP_K, bw), lambda c: (0, c)),
                   pl.BlockSpec((TOP_K, bw), lambda c: (0, c))],
        out_shape=[jax.ShapeDtypeStruct((TOP_K, n), jnp.int32),
                   jax.ShapeDtypeStruct((TOP_K, n), F32)],
        scratch_shapes=[run],
        compiler_params=_tc_params(("arbitrary",)),
        name="route",
    )(logits_t, tri, low, counts)
    return pos, w_top, counts


def _sc_mesh():
    return plsc.VectorSubcoreMesh(core_axis_name="c", subcore_axis_name="s")


def _sc_worker_id():
    return lax.axis_index("s") * lax.axis_size("c") + lax.axis_index("c")


def _dispatch_rows(h2, pos):
    n, d = h2.shape
    ch = SC_ROWS_PER_COPY
    per_w = n // SC_WORKERS
    n_ch = per_w // ch
    total_rows = _max_slots(n) * SLOT_ROWS
    idx = pos.reshape(TOP_K, SC_WORKERS, n_ch, ch).transpose(1, 2, 0, 3).reshape(
        SC_WORKERS, n_ch * TOP_K, ch)

    @functools.partial(
        pl.kernel, mesh=_sc_mesh(),
        out_type=jax.ShapeDtypeStruct((total_rows, d), h2.dtype),
        scratch_types=[pltpu.VMEM((n_ch * TOP_K, ch), jnp.int32),
                       pltpu.VMEM((ch, d), h2.dtype), pltpu.VMEM((ch, d), h2.dtype),
                       pltpu.SemaphoreType.DMA, pltpu.SemaphoreType.DMA,
                       pltpu.SemaphoreType.DMA, pltpu.SemaphoreType.DMA])
    def scatter(x_hbm, idx_hbm, o_hbm, idx_v, rows0, rows1, rsem0, rsem1, ssem0, ssem1):
        wid = _sc_worker_id()
        base = wid * per_w
        pltpu.sync_copy(idx_hbm.at[wid], idx_v)
        rows, rsem, ssem = (rows0, rows1), (rsem0, rsem1), (ssem0, ssem1)

        def read(c, b):
            r0 = pl.multiple_of(base + c * ch, ch)
            return pltpu.make_async_copy(x_hbm.at[pl.ds(r0, ch)], rows[b], rsem[b])

        def send(c, k, b):
            return pltpu.make_async_copy(rows[b], o_hbm.at[idx_v.at[c * TOP_K + k]], ssem[b])

        read(0, 0).start()

        @pl.loop(0, n_ch, step=2)
        def _(c):
            read(c, 0).wait()

            @pl.when(c > 0)
            def _():
                for k in range(TOP_K):
                    send(c - 1, k, 1).wait()

            read(c + 1, 1).start()
            for k in range(TOP_K):
                send(c, k, 0).start()
            read(c + 1, 1).wait()
            for k in range(TOP_K):
                send(c, k, 0).wait()

            @pl.when(c + 2 < n_ch)
            def _():
                read(c + 2, 0).start()

            for k in range(TOP_K):
                send(c + 1, k, 1).start()

        for k in range(TOP_K):
            send(n_ch - 1, k, 1).wait()

    return scatter(h2, idx)


def _combine_rows(ys, pos_flat):
    n = pos_flat.shape[0]
    d = ys.shape[1]
    ch = SC_ROWS_PER_COPY
    per_w = n // SC_WORKERS
    n_ch = per_w // ch
    idx = pos_flat.reshape(SC_WORKERS, n_ch, ch)

    @functools.partial(
        pl.kernel, mesh=_sc_mesh(),
        out_type=jax.ShapeDtypeStruct((n, d), ys.dtype),
        scratch_types=[pltpu.VMEM((n_ch, ch), jnp.int32),
                       pltpu.VMEM((ch, d), ys.dtype), pltpu.VMEM((ch, d), ys.dtype),
                       pltpu.SemaphoreType.DMA, pltpu.SemaphoreType.DMA,
                       pltpu.SemaphoreType.DMA, pltpu.SemaphoreType.DMA])
    def gather(y_hbm, idx_hbm, o_hbm, idx_v, rows0, rows1, gsem0, gsem1, wsem0, wsem1):
        wid = _sc_worker_id()
        base = wid * per_w
        pltpu.sync_copy(idx_hbm.at[wid], idx_v)
        rows, gsem, wsem = (rows0, rows1), (gsem0, gsem1), (wsem0, wsem1)

        def fetch(c, b):
            return pltpu.make_async_copy(y_hbm.at[idx_v.at[c]], rows[b], gsem[b])

        def write(c, b):
            r0 = pl.multiple_of(base + c * ch, ch)
            return pltpu.make_async_copy(rows[b], o_hbm.at[pl.ds(r0, ch)], wsem[b])

        fetch(0, 0).start()

        @pl.loop(0, n_ch, step=2)
        def _(c):
            fetch(c, 0).wait()

            @pl.when(c > 0)
            def _():
                write(c - 1, 1).wait()

            fetch(c + 1, 1).start()
            write(c, 0).start()
            fetch(c + 1, 1).wait()
            write(c, 0).wait()

            @pl.when(c + 2 < n_ch)
            def _():
                fetch(c + 2, 0).start()

            write(c + 1, 1).start()

        write(n_ch - 1, 1).wait()

    return gather(ys, idx)


def _max_slots(n_tokens):
    return N_EXPERTS + (n_tokens * TOP_K) // SLOT_ROWS


def _expert_kernel(se_ref, sr_ref,
                   xs_ref, wgu_hbm, wd_hbm, bgu_ref, bd_ref,
                   ys_ref,
                   xb_ref, act_ref, wbuf_ref, sem_ref, *, n_a, n_b):
    s = pl.program_id(0)
    rows = sr_ref[s]
    n_blk = (rows + ROW_BLOCK - 1) // ROW_BLOCK
    tw = wbuf_ref.shape[2] // 2
    d_ff = act_ref.shape[1]
    half_d = ys_ref.shape[1]

    def panel_copy(src, buf, p):
        return pltpu.make_async_copy(src, wbuf_ref.at[buf, :, pl.ds(p * tw, tw)], sem_ref.at[buf, p])

    def gate_up_panels(e, t, buf):
        c0 = pl.multiple_of(t * tw, tw)
        return (panel_copy(wgu_hbm.at[e, :, pl.ds(c0, tw)], buf, 0),
                panel_copy(wgu_hbm.at[e, :, pl.ds(d_ff + c0, tw)], buf, 1))

    def down_panels(e, u, buf):
        c0 = pl.multiple_of(u * tw, tw)
        return (panel_copy(wd_hbm.at[e, :, pl.ds(c0, tw)], buf, 0),
                panel_copy(wd_hbm.at[e, :, pl.ds(half_d + c0, tw)], buf, 1))

    def start(panels):
        for cp in panels:
            cp.start()

    def wait(panels):
        for cp in panels:
            cp.wait()

    expert = se_ref[s]

    def for_row_blocks(fn):
        n_big = n_blk // 8

        def body(rb, carry):
            fn(pl.multiple_of(rb * 8 * ROW_BLOCK, 8 * ROW_BLOCK), 8 * ROW_BLOCK)
            return carry

        lax.fori_loop(0, n_big, body, 0)
        for size in (4, 2, 1):
            @pl.when((n_blk // size) % 2 == 1)
            def _(size=size):
                first = (n_blk // (2 * size)) * 2 * size * ROW_BLOCK
                fn(pl.multiple_of(first, 2 * size * ROW_BLOCK), size * ROW_BLOCK)

    @pl.when(s == 0)
    def _():
        for k in range(RING_AHEAD):
            start(gate_up_panels(expert, k, k))

    def unpack(rb, carry):
        r0 = pl.multiple_of(rb * ROW_BLOCK, ROW_BLOCK)
        keep = (r0 + lax.broadcasted_iota(jnp.int32, (ROW_BLOCK, 1), 0)) < rows
        lo, hi = _unpack_bf16_pair(xs_ref[pl.ds(r0, ROW_BLOCK), :])
        xb_ref[pl.ds(r0, ROW_BLOCK), 0:half_d] = jnp.where(keep, lo, 0.0).astype(BF16)
        xb_ref[pl.ds(r0, ROW_BLOCK), half_d:2 * half_d] = jnp.where(keep, hi, 0.0).astype(BF16)
        return carry

    lax.fori_loop(0, n_blk, unpack, 0)

    ring = RING_AHEAD + 1

    def gate_up_tile(t, carry):
        buf = t % ring
        wait(gate_up_panels(expert, t, buf))
        nxt = t + RING_AHEAD

        @pl.when(nxt < n_a)
        def _():
            start(gate_up_panels(expert, nxt, nxt % ring))

        @pl.when(nxt >= n_a)
        def _():
            start(down_panels(expert, nxt - n_a, nxt % ring))

        col = pl.multiple_of(t * tw, tw)
        b_gate = bgu_ref[0, :, pl.ds(col, tw)]
        b_up = bgu_ref[0, :, pl.ds(d_ff + col, tw)]

        def gate_up(r0, m):
            gu = jnp.dot(xb_ref[pl.ds(r0, m), :], wbuf_ref[buf].astype(BF16), preferred_element_type=F32)
            gate = jnp.minimum(gu[:, 0:tw] + b_gate, SWIGLU_LIMIT)
            up = jnp.clip(gu[:, tw:2 * tw] + b_up, -SWIGLU_LIMIT, SWIGLU_LIMIT)
            act = (up + 1.0) * (gate * jax.nn.sigmoid(SWIGLU_ALPHA * gate))
            act_ref[pl.ds(r0, m), pl.ds(col, tw)] = act.astype(BF16)

        for_row_blocks(gate_up)
        return carry

    lax.fori_loop(0, n_a, gate_up_tile, 0)

    def down_tile(u, carry):
        buf = (n_a + u) % ring
        wait(down_panels(expert, u, buf))
        nxt = u + RING_AHEAD

        @pl.when(s + 1 < pl.num_programs(0))
        def _():
            start(gate_up_panels(se_ref[s + 1], nxt - n_b, (nxt - n_b) % ring))

        col = pl.multiple_of(u * tw, tw)
        b_lo = bd_ref[0, :, pl.ds(col, tw)]
        b_hi = bd_ref[0, :, pl.ds(half_d + col, tw)]

        def down(r0, m):
            y = jnp.dot(act_ref[pl.ds(r0, m), :], wbuf_ref[buf].astype(BF16), preferred_element_type=F32)
            ys_ref[pl.ds(r0, m), pl.ds(col, tw)] = _pack_bf16_pair(y[:, 0:tw] + b_lo, y[:, tw:2 * tw] + b_hi)

        for_row_blocks(down)
        return carry

    lax.fori_loop(0, n_b, down_tile, 0)

    def fill(rb, carry):
        r0 = pl.multiple_of(rb * ROW_BLOCK, ROW_BLOCK)
        ys_ref[pl.ds(r0, ROW_BLOCK), :] = jnp.zeros((ROW_BLOCK, half_d), jnp.uint32)
        return carry

    lax.fori_loop(n_blk, SLOT_ROWS // ROW_BLOCK, fill, 0)


def _experts(xs, slot_expert, slot_rows, n_used, w_gate_up, b_gate_up, w_down, b_down):
    n_e, d, two_ff = w_gate_up.shape
    d_ff = two_ff // 2
    assert d == d_ff, "the weight ring holds (rows, W_PANEL) panels of both projections"
    tw = W_PANEL
    n_a, n_b = d_ff // tw, (d // 2) // tw
    ring = RING_AHEAD + 1
    assert (n_a + n_b) % ring == 0, "ring slot of tile j must repeat from one expert slot to the next"
    assert n_b == RING_AHEAD <= n_a, "the look-ahead from a down tile must land in the next slot's gate/up tiles"
    total_rows = xs.shape[0]

    grid_spec = pltpu.PrefetchScalarGridSpec(
        num_scalar_prefetch=2,
        grid=(n_used,),
        in_specs=[
            pl.BlockSpec((SLOT_ROWS, d // 2), lambda s, se, sr: (s, 0)),
            pl.BlockSpec(memory_space=pl.ANY),
            pl.BlockSpec(memory_space=pl.ANY),
            pl.BlockSpec((1, 1, two_ff), lambda s, se, sr: (se[s], 0, 0)),
            pl.BlockSpec((1, 1, d), lambda s, se, sr: (se[s], 0, 0)),
        ],
        out_specs=pl.BlockSpec((SLOT_ROWS, d // 2), lambda s, se, sr: (s, 0)),
        scratch_shapes=[pltpu.VMEM((SLOT_ROWS, d), BF16),
                        pltpu.VMEM((SLOT_ROWS, d_ff), BF16),
                        pltpu.VMEM((ring, d, 2 * tw), F32),
                        pltpu.SemaphoreType.DMA((ring, 2))],
    )
    return pl.pallas_call(
        functools.partial(_expert_kernel, n_a=n_a, n_b=n_b),
        grid_spec=grid_spec,
        out_shape=jax.ShapeDtypeStruct((total_rows, d // 2), jnp.uint32),
        compiler_params=_tc_params(("arbitrary",)),
        name="experts",
    )(slot_expert, slot_rows, xs, w_gate_up, w_down,
      b_gate_up.reshape(n_e, 1, two_ff), b_down.reshape(n_e, 1, d))


def _slot_tables(counts, n_tokens):
    max_slots = _max_slots(n_tokens)
    n_slots = (counts + SLOT_ROWS - 1) // SLOT_ROWS
    slot_end = jnp.cumsum(n_slots)
    slot_start = slot_end - n_slots
    n_used = slot_end[-1]
    sid = jnp.arange(max_slots, dtype=jnp.int32)
    expert = jnp.minimum(jnp.sum(sid[:, None] >= slot_end[None, :], axis=1), N_EXPERTS - 1).astype(jnp.int32)
    local = sid - slot_start[expert]
    rows = jnp.clip(counts[expert] - local * SLOT_ROWS, 0, SLOT_ROWS)
    rows = jnp.where(sid < n_used, rows, 0).astype(jnp.int32)
    return expert, rows, n_used.reshape(1).astype(jnp.int32)


def _final_kernel(x1_ref, y0_ref, y1_ref, y2_ref, y3_ref, w_ref, mod_ref, g_ref, *out_refs):
    o_ref = out_refs[-1]
    half = y0_ref.shape[1]
    d = 2 * half
    moe_lo = moe_hi = None
    for k, y_ref in enumerate((y0_ref, y1_ref, y2_ref, y3_ref)):
        lo, hi = _unpack_bf16_pair(y_ref[...])
        wk = w_ref[:, k:k + 1]
        moe_lo = wk * lo if k == 0 else moe_lo + wk * lo
        moe_hi = wk * hi if k == 0 else moe_hi + wk * hi
    x_lo = x1_ref[:, 0:half] + mod_ref[0, 5:6, 0:half] * moe_lo
    x_hi = x1_ref[:, half:d] + mod_ref[0, 5:6, half:d] * moe_hi
    ss = jnp.sum(x_lo * x_lo, axis=-1, keepdims=True) + jnp.sum(x_hi * x_hi, axis=-1, keepdims=True)
    r = lax.rsqrt(ss / d + EPS)
    o_ref[:, 0:half] = x_lo * r * g_ref[:, 0:half]
    o_ref[:, half:d] = x_hi * r * g_ref[:, half:d]


def _final_part(x1, y4, w_tok, mod, final_g, seq, part, n_parts, out_so_far):
    n, d = x1.shape
    tm = 256
    per_batch = seq // tm
    n_i = n // tm // n_parts
    off = part * n_i
    y_specs = [pl.BlockSpec((tm, d // 2), functools.partial(lambda i, k: (k * n_i + i, 0), k=k))
               for k in range(TOP_K)]
    in_specs = [pl.BlockSpec((tm, d), lambda i: (off + i, 0))] + y_specs + [
        pl.BlockSpec((tm, TOP_K), lambda i: (off + i, 0)),
        pl.BlockSpec((1, 6, d), lambda i: ((off + i) // per_batch, 0, 0)),
        pl.BlockSpec((1, d), lambda i: (0, 0))]
    args = [x1, y4, y4, y4, y4, w_tok, mod, final_g.reshape(1, d)]
    aliases = {}
    if out_so_far is not None:
        in_specs.append(pl.BlockSpec(memory_space=pl.ANY))
        args.append(out_so_far)
        aliases = {len(args) - 1: 0}
    return pl.pallas_call(
        _final_kernel,
        grid=(n_i,),
        in_specs=in_specs,
        out_specs=pl.BlockSpec((tm, d), lambda i: (off + i, 0)),
        out_shape=jax.ShapeDtypeStruct((n, d), F32),
        input_output_aliases=aliases,
        compiler_params=_tc_params(("arbitrary",)),
        name=f"final{part}",
    )(*args)


def kernel(x, c, norm1_g, w_mod, b_mod, w_in, conv_w, w_out, norm2_g, w_router, b_router,
           w_gate_up, b_gate_up, w_down, b_down, final_g):
    b, seq, d = x.shape
    n = b * seq
    x2 = x.reshape(n, d)
    mod = _modulation(c, w_mod[0], b_mod[0])
    proj = _in_projection(x2, mod, norm1_g[0], w_in[0].astype(BF16), seq)
    x1, h2, logits = _mixer(proj, x2, mod, conv_w[0], w_out[0].astype(BF16), norm2_g[0],
                            w_router[0], b_router[0], seq)
    pos, w_top, counts = _route(logits.T)
    slot_expert, slot_rows, n_used = _slot_tables(counts[:, 0].astype(jnp.int32), n)
    xs = _dispatch_rows(h2, pos)
    ys = _experts(xs, slot_expert, slot_rows, n_used[0], w_gate_up[0], b_gate_up[0], w_down[0], b_down[0])
    w_tok = w_top.T
    out = None
    for part in range(COMBINE_PARTS):
        lo = part * (n // COMBINE_PARTS)
        pos_part = pos[:, lo:lo + n // COMBINE_PARTS]
        y4 = _combine_rows(ys, pos_part.reshape(TOP_K * (n // COMBINE_PARTS)))
        out = _final_part(x1, y4, w_tok, mod, final_g, seq, part, COMBINE_PARTS, out)
    return out.reshape(b, seq, d)
```

```python
import functools
import math

import jax
import jax.numpy as jnp
from jax import lax
from jax.experimental import pallas as pl
from jax.experimental.pallas import tpu as pltpu
from jax.experimental.pallas import tpu_sc as plsc

F32 = jnp.float32
BF16 = jnp.bfloat16

CHUNK = 64
N_HEADS = 8
HEAD_DIM = 128
D_RET = N_HEADS * HEAD_DIM
D_CONV = 1024
CONV_WIDTH = 3
N_EXPERTS = 32
TOP_K = 4
SWIGLU_LIMIT = 7.0
SWIGLU_ALPHA = 1.702
ROPE_BASE = 10000.0
EPS = 1e-6

VMEM_LIMIT_BYTES = 58 * 1024 * 1024
SC_WORKERS = 32
SC_ROWS_PER_COPY = 32

MIX_BLOCK = 256
ROW_BLOCK = 128
SLOT_ROWS = 9 * ROW_BLOCK
W_PANEL = 512
RING_AHEAD = 2


def _tc_params(sem):
    return pltpu.CompilerParams(dimension_semantics=sem, vmem_limit_bytes=VMEM_LIMIT_BYTES)


def _bf16_bits(x):
    return lax.bitcast_convert_type(x.astype(BF16).astype(F32), jnp.uint32)


def _pack_bf16_pair(lo, hi):
    return (_bf16_bits(lo) >> 16) | _bf16_bits(hi)


def _unpack_bf16_pair(p):
    lo = lax.bitcast_convert_type(p << 16, F32)
    hi = lax.bitcast_convert_type(p & jnp.uint32(0xFFFF0000), F32)
    return lo, hi


def _mod_kernel(c_ref, w_ref, b_ref, o_ref):
    c = c_ref[...]
    ca = (c * jax.nn.sigmoid(c)).astype(BF16)
    o_ref[...] = jnp.dot(ca, w_ref[...].astype(BF16), preferred_element_type=F32) + b_ref[...]


def _modulation(c, w_mod, b_mod):
    b, d = c.shape
    n = w_mod.shape[1]
    tn = 1024
    c8 = jnp.zeros((8, d), F32).at[:b].set(c)
    out = pl.pallas_call(
        _mod_kernel,
        grid=(n // tn,),
        in_specs=[pl.BlockSpec((8, d), lambda j: (0, 0)),
                  pl.BlockSpec((d, tn), lambda j: (0, j)),
                  pl.BlockSpec((1, tn), lambda j: (0, j))],
        out_specs=pl.BlockSpec((8, tn), lambda j: (0, j)),
        out_shape=jax.ShapeDtypeStruct((8, n), F32),
        compiler_params=_tc_params(("arbitrary",)),
        name="mod",
    )(c8, w_mod, b_mod.reshape(1, n))
    return out[:b].reshape(b, 6, d)


def _norm_mod(x, g, scale, shift):
    y = x * lax.rsqrt(jnp.mean(x * x, axis=-1, keepdims=True) + EPS) * g
    return y * (1.0 + scale) + shift


def _inproj_kernel(x_ref, mod_ref, g_ref, w_ref, o_ref, h_ref, *, rows):
    @pl.when(pl.program_id(1) == 0)
    def _():
        g = g_ref[...]
        scale = mod_ref[0, 1:2, :]
        shift = mod_ref[0, 0:1, :]

        def body(r, carry):
            r0 = pl.multiple_of(r * rows, rows)
            h = _norm_mod(x_ref[pl.ds(r0, rows), :], g, scale, shift)
            h_ref[pl.ds(r0, rows), :] = h.astype(BF16)
            return carry

        lax.fori_loop(0, x_ref.shape[0] // rows, body, 0)

    o_ref[...] = jnp.dot(h_ref[...], w_ref[...], preferred_element_type=F32).astype(o_ref.dtype)


def _in_projection(x2, mod, g1, w_in_bf, seq):
    n, d = x2.shape
    p = w_in_bf.shape[1]
    tm, tn = min(1024, seq), 1792
    per_batch = seq // tm
    return pl.pallas_call(
        functools.partial(_inproj_kernel, rows=128),
        grid=(n // tm, p // tn),
        in_specs=[pl.BlockSpec((tm, d), lambda i, j: (i, 0)),
                  pl.BlockSpec((1, 6, d), lambda i, j: (i // per_batch, 0, 0)),
                  pl.BlockSpec((1, d), lambda i, j: (0, 0)),
                  pl.BlockSpec((d, tn), lambda i, j: (0, j))],
        out_specs=pl.BlockSpec((tm, tn), lambda i, j: (i, j)),
        out_shape=jax.ShapeDtypeStruct((n, p), BF16),
        scratch_shapes=[pltpu.VMEM((tm, d), BF16)],
        compiler_params=_tc_params(("arbitrary", "arbitrary")),
        name="inproj",
    )(x2, mod, g1.reshape(1, d), w_in_bf)


def _mixer_tables(seq, blk):
    half = HEAD_DIM // 2
    freqs = ROPE_BASE ** (-jnp.arange(half, dtype=F32) / half)
    ang = jnp.arange(seq, dtype=F32)[:, None] * freqs[None, :]
    cos = jnp.concatenate([jnp.cos(ang), jnp.cos(ang)], axis=-1)
    sin = jnp.concatenate([-jnp.sin(ang), jnp.sin(ang)], axis=-1)
    log_gamma = jnp.log1p(-jnp.exp2(-5.0 - jnp.arange(N_HEADS, dtype=F32)))
    idx = jnp.arange(blk, dtype=F32)
    dist = jnp.abs(idx[:, None] - idx[None, :])
    ck = jnp.arange(blk) // CHUNK
    visible = (ck[None, :] <= ck[:, None]).astype(F32)
    dmask = jnp.exp(log_gamma[:, None, None] * dist) * visible
    ones = jnp.ones((1, 1, HEAD_DIM), F32)
    dq = jnp.exp(log_gamma[:, None] * (idx + 1.0)[None])[:, :, None] * ones
    dkv = jnp.exp(log_gamma[:, None] * (blk - 1 - idx)[None])[:, :, None] * ones
    dblk = jnp.exp(log_gamma * blk)[:, None, None] * ones
    return cos, sin, dmask, dq, dkv, dblk


def _mixer_kernel(proj_ref, x_ref, mod_ref, cos_ref, sin_ref, dmask_ref, dq_ref, dkv_ref, dblk_ref,
                  convw_ref, wout_ref, g2_ref, wr_ref, br_ref,
                  x1_ref, h2_ref, lg_ref,
                  s_ref, zbuf_ref, y_ref, *, per_batch, n_blocks):
    blk = x_ref.shape[0]
    i = pl.program_id(0)
    cur = i % 2

    @pl.when(jnp.minimum(i, n_blocks - 1) % per_batch == 0)
    def _():
        s_ref[...] = jnp.zeros_like(s_ref)
        zbuf_ref[0:8, :] = jnp.zeros((8, D_CONV), F32)

    @pl.when(i == 0)
    def _():
        y_ref[1] = jnp.zeros(y_ref.shape[1:], BF16)

    cos = cos_ref[...]
    sin = sin_ref[...]
    k_scale = HEAD_DIM ** -0.5
    nt = (((1,), (1,)), ((), ()))
    tn = (((0,), (0,)), ((), ()))
    for h in range(N_HEADS):
        c0 = h * HEAD_DIM
        q = proj_ref[:, c0:c0 + HEAD_DIM].astype(F32)
        k = proj_ref[:, D_RET + c0:D_RET + c0 + HEAD_DIM].astype(F32)
        v = proj_ref[:, 2 * D_RET + c0:2 * D_RET + c0 + HEAD_DIM]
        g = proj_ref[:, 3 * D_RET + c0:3 * D_RET + c0 + HEAD_DIM].astype(F32)
        qr = q * cos + pltpu.roll(q, HEAD_DIM // 2, 1) * sin
        kr = (k * cos + pltpu.roll(k, HEAD_DIM // 2, 1) * sin) * k_scale
        qb = qr.astype(BF16)
        kb = kr.astype(BF16)
        scores = lax.dot_general(qb, kb, nt, preferred_element_type=F32) * dmask_ref[h]
        intra = jnp.dot(scores.astype(BF16), v, preferred_element_type=F32)
        state = s_ref[h]
        cross = jnp.dot(qb, state.astype(BF16), preferred_element_type=F32) * dq_ref[h]
        kd = (kr * dkv_ref[h]).astype(BF16)
        kv = lax.dot_general(kd, v, tn, preferred_element_type=F32)
        s_ref[h] = dblk_ref[h] * state + kv
        o = intra + cross
        mu = jnp.mean(o, axis=-1, keepdims=True)
        dev = o - mu
        var = jnp.mean(dev * dev, axis=-1, keepdims=True)
        on = dev * lax.rsqrt(var + EPS)
        y_ref[cur, :, c0:c0 + HEAD_DIM] = (g * jax.nn.sigmoid(g) * on).astype(BF16)

    cw = 256
    base = 4 * D_RET
    for cb in range(D_CONV // cw):
        lo = cb * cw
        cg = proj_ref[:, base + D_CONV + lo:base + D_CONV + lo + cw].astype(F32)
        u = proj_ref[:, base + 2 * D_CONV + lo:base + 2 * D_CONV + lo + cw].astype(F32)
        zbuf_ref[8:blk + 8, lo:lo + cw] = cg * u
    for cb in range(D_CONV // cw):
        lo = cb * cw
        z0 = zbuf_ref[8:blk + 8, lo:lo + cw]
        z1 = zbuf_ref[7:blk + 7, lo:lo + cw]
        z2 = zbuf_ref[6:blk + 6, lo:lo + cw]
        z = (convw_ref[2:3, lo:lo + cw] * z0 + convw_ref[1:2, lo:lo + cw] * z1
             + convw_ref[0:1, lo:lo + cw] * z2)
        bg = proj_ref[:, base + lo:base + lo + cw].astype(F32)
        y_ref[cur, :, D_RET + lo:D_RET + lo + cw] = (bg * z).astype(BF16)
    zbuf_ref[0:8, :] = zbuf_ref[blk:blk + 8, :]

    mix = jnp.dot(y_ref[1 - cur], wout_ref[...], preferred_element_type=F32)
    x1 = x_ref[...] + mod_ref[0, 2:3, :] * mix
    x1_ref[...] = x1
    h2 = _norm_mod(x1, g2_ref[...], mod_ref[0, 4:5, :], mod_ref[0, 3:4, :])
    half = h2.shape[1] // 2
    h2_ref[...] = _pack_bf16_pair(h2[:, :half], h2[:, half:])
    w2 = wr_ref[...]
    w_hi = w2.astype(BF16)
    w_lo = (w2 - w_hi.astype(F32)).astype(BF16)
    w_split = jnp.where(lax.broadcasted_iota(jnp.int32, w2.shape, 1) < N_EXPERTS, w_hi, w_lo)
    h_hi = h2.astype(BF16)
    h_lo = (h2 - h_hi.astype(F32)).astype(BF16)
    r = (jnp.dot(h_hi, w_split, preferred_element_type=F32)
         + jnp.dot(h_lo, w_split, preferred_element_type=F32))
    lg_ref[...] = r[:, 0:N_EXPERTS] + r[:, N_EXPERTS:2 * N_EXPERTS] + br_ref[...]


def _mixer(proj, x2, mod, conv_w, w_out_bf, g2, w_router, b_router, seq):
    n, d = x2.shape
    blk = MIX_BLOCK
    per_batch = seq // blk
    cos, sin, dmask, dq, dkv, dblk = _mixer_tables(seq, blk)
    const2 = lambda i: (0, 0)
    const3 = lambda i: (0, 0, 0)
    n_blocks = n // blk
    mixed = lambda i: jnp.minimum(i, n_blocks - 1)
    done = lambda i: jnp.maximum(i - 1, 0)
    return pl.pallas_call(
        functools.partial(_mixer_kernel, per_batch=per_batch, n_blocks=n_blocks),
        grid=(n_blocks + 1,),
        in_specs=[pl.BlockSpec((blk, proj.shape[1]), lambda i: (mixed(i), 0)),
                  pl.BlockSpec((blk, d), lambda i: (done(i), 0)),
                  pl.BlockSpec((1, 6, d), lambda i: (done(i) // per_batch, 0, 0)),
                  pl.BlockSpec((blk, HEAD_DIM), lambda i: (mixed(i) % per_batch, 0)),
                  pl.BlockSpec((blk, HEAD_DIM), lambda i: (mixed(i) % per_batch, 0)),
                  pl.BlockSpec((N_HEADS, blk, blk), const3),
                  pl.BlockSpec((N_HEADS, blk, HEAD_DIM), const3),
                  pl.BlockSpec((N_HEADS, blk, HEAD_DIM), const3),
                  pl.BlockSpec((N_HEADS, 1, HEAD_DIM), const3),
                  pl.BlockSpec((CONV_WIDTH, D_CONV), const2),
                  pl.BlockSpec((D_RET + D_CONV, d), const2),
                  pl.BlockSpec((1, d), const2),
                  pl.BlockSpec((d, 2 * N_EXPERTS), const2),
                  pl.BlockSpec((1, N_EXPERTS), const2)],
        out_specs=[pl.BlockSpec((blk, d), lambda i: (done(i), 0)),
                   pl.BlockSpec((blk, d // 2), lambda i: (done(i), 0)),
                   pl.BlockSpec((blk, N_EXPERTS), lambda i: (done(i), 0))],
        out_shape=[jax.ShapeDtypeStruct((n, d), F32),
                   jax.ShapeDtypeStruct((n, d // 2), jnp.uint32),
                   jax.ShapeDtypeStruct((n, N_EXPERTS), F32)],
        scratch_shapes=[pltpu.VMEM((N_HEADS, HEAD_DIM, HEAD_DIM), F32),
                        pltpu.VMEM((blk + 8, D_CONV), F32),
                        pltpu.VMEM((2, blk, D_RET + D_CONV), BF16)],
        compiler_params=_tc_params(("arbitrary",)),
        name="mixer",
    )(proj, x2, mod, cos, sin, dmask, dq, dkv, dblk, conv_w, w_out_bf, g2.reshape(1, d),
      jnp.concatenate([w_router, w_router], axis=1), b_router.reshape(1, N_EXPERTS))


def _top_k_block(lg_ref, tri_ref, run_ref):
    n_e, bw = lg_ref.shape

    @pl.when(pl.program_id(0) == 0)
    def _():
        run_ref[...] = jnp.zeros_like(run_ref)

    l = lg_ref[...]
    eio = lax.broadcasted_iota(jnp.int32, (n_e, bw), 0)
    vals, hots = [], []
    for _ in range(TOP_K):
        m = jnp.max(l, axis=0, keepdims=True)
        idx = jnp.min(jnp.where(l == m, eio, n_e), axis=0, keepdims=True)
        hot = eio == idx
        vals.append(m)
        hots.append(hot)
        l = jnp.where(hot, -jnp.inf, l)
    sel = sum(jnp.where(hot, 1.0, 0.0) for hot in hots)
    incl = jnp.dot(sel.astype(BF16), tri_ref[...], preferred_element_type=F32)
    run = run_ref[:, 0:1]
    rank = incl - sel + run
    run_ref[...] = jnp.broadcast_to(run + incl[:, bw - 1:bw], run_ref.shape)
    return vals, hots, rank


def _count_kernel(lg_ref, tri_ref, cnt_ref, run_ref):
    _top_k_block(lg_ref, tri_ref, run_ref)
    cnt_ref[...] = run_ref[...]


def _route_kernel(lg_ref, tri_ref, low_ref, tot_ref, pos_ref, w_ref, run_ref, *, max_slots):
    vals, hots, rank = _top_k_block(lg_ref, tri_ref, run_ref)
    ex = [jnp.exp(v - vals[0]) for v in vals]
    den = ex[0] + ex[1] + ex[2] + ex[3]
    for k in range(TOP_K):
        w_ref[k:k + 1, :] = ex[k] / den
    tot = tot_ref[...]
    n_slots = sum(jnp.where(tot > float(m * SLOT_ROWS), 1.0, 0.0) for m in range(max_slots))
    start = jnp.dot(low_ref[...], n_slots.astype(BF16), preferred_element_type=F32)
    dest = start[:, 0:1] * float(SLOT_ROWS) + rank
    for k in range(TOP_K):
        pos_ref[k:k + 1, :] = jnp.sum(jnp.where(hots[k], dest, 0.0), axis=0,
                                      keepdims=True).astype(jnp.int32)


def _route(logits_t):
    n_e, n = logits_t.shape
    bw = 512
    max_slots = -(-n // SLOT_ROWS)
    tri = (jnp.arange(bw)[:, None] <= jnp.arange(bw)[None, :]).astype(BF16)
    low = (jnp.arange(n_e)[None, :] < jnp.arange(n_e)[:, None]).astype(BF16)
    lg_spec = pl.BlockSpec((n_e, bw), lambda c: (0, c))
    tri_spec = pl.BlockSpec((bw, bw), lambda c: (0, 0))
    cnt_spec = pl.BlockSpec((n_e, 128), lambda c: (0, 0))
    run = pltpu.VMEM((n_e, 128), F32)
    counts = pl.pallas_call(
        _count_kernel,
        grid=(n // bw,),
        in_specs=[lg_spec, tri_spec],
        out_specs=cnt_spec,
        out_shape=jax.ShapeDtypeStruct((n_e, 128), F32),
        scratch_shapes=[run],
        compiler_params=_tc_params(("arbitrary",)),
        name="route_count",
    )(logits_t, tri)
    pos, w_top = pl.pallas_call(
        functools.partial(_route_kernel, max_slots=max_slots),
        grid=(n // bw,),
        in_specs=[lg_spec, tri_spec, pl.BlockSpec((n_e, n_e), lambda c: (0, 0)), cnt_spec],
        out_specs=[pl.BlockSpec((TOP_K, bw), lambda c: (0, c)),
                   pl.BlockSpec((TOP_K, bw), lambda c: (0, c))],
        out_shape=[jax.ShapeDtypeStruct((TOP_K, n), jnp.int32),
                   jax.ShapeDtypeStruct((TOP_K, n), F32)],
        scratch_shapes=[run],
        compiler_params=_tc_params(("arbitrary",)),
        name="route",
    )(logits_t, tri, low, counts)
    return pos, w_top, counts


def _sc_mesh():
    return plsc.VectorSubcoreMesh(core_axis_name="c", subcore_axis_name="s")


def _sc_worker_id():
    return lax.axis_index("s") * lax.axis_size("c") + lax.axis_index("c")


def _dispatch_rows(h2, pos):
    n, d = h2.shape
    ch = SC_ROWS_PER_COPY
    per_w = n // SC_WORKERS
    n_ch = per_w // ch
    total_rows = _max_slots(n) * SLOT_ROWS
    idx = pos.reshape(TOP_K, SC_WORKERS, n_ch, ch).transpose(1, 2, 0, 3).reshape(
        SC_WORKERS, n_ch * TOP_K, ch)

    @functools.partial(
        pl.kernel, mesh=_sc_mesh(),
        out_type=jax.ShapeDtypeStruct((total_rows, d), h2.dtype),
        scratch_types=[pltpu.VMEM((n_ch * TOP_K, ch), jnp.int32),
                       pltpu.VMEM((ch, d), h2.dtype), pltpu.VMEM((ch, d), h2.dtype),
                       pltpu.SemaphoreType.DMA, pltpu.SemaphoreType.DMA,
                       pltpu.SemaphoreType.DMA, pltpu.SemaphoreType.DMA])
    def scatter(x_hbm, idx_hbm, o_hbm, idx_v, rows0, rows1, rsem0, rsem1, ssem0, ssem1):
        wid = _sc_worker_id()
        base = wid * per_w
        pltpu.sync_copy(idx_hbm.at[wid], idx_v)
        rows, rsem, ssem = (rows0, rows1), (rsem0, rsem1), (ssem0, ssem1)

        def read(c, b):
            r0 = pl.multiple_of(base + c * ch, ch)
            return pltpu.make_async_copy(x_hbm.at[pl.ds(r0, ch)], rows[b], rsem[b])

        def send(c, k, b):
            return pltpu.make_async_copy(rows[b], o_hbm.at[idx_v.at[c * TOP_K + k]], ssem[b])

        read(0, 0).start()

        @pl.loop(0, n_ch, step=2)
        def _(c):
            read(c, 0).wait()

            @pl.when(c > 0)
            def _():
                for k in range(TOP_K):
                    send(c - 1, k, 1).wait()

            read(c + 1, 1).start()
            for k in range(TOP_K):
                send(c, k, 0).start()
            read(c + 1, 1).wait()
            for k in range(TOP_K):
                send(c, k, 0).wait()

            @pl.when(c + 2 < n_ch)
            def _():
                read(c + 2, 0).start()

            for k in range(TOP_K):
                send(c + 1, k, 1).start()

        for k in range(TOP_K):
            send(n_ch - 1, k, 1).wait()

    return scatter(h2, idx)


def _combine_rows(ys, pos_flat):
    n = pos_flat.shape[0]
    d = ys.shape[1]
    ch = SC_ROWS_PER_COPY
    per_w = n // SC_WORKERS
    n_ch = per_w // ch
    idx = pos_flat.reshape(SC_WORKERS, n_ch, ch)

    @functools.partial(
        pl.kernel, mesh=_sc_mesh(),
        out_type=jax.ShapeDtypeStruct((n, d), ys.dtype),
        scratch_types=[pltpu.VMEM((n_ch, ch), jnp.int32),
                       pltpu.VMEM((ch, d), ys.dtype), pltpu.VMEM((ch, d), ys.dtype),
                       pltpu.SemaphoreType.DMA, pltpu.SemaphoreType.DMA,
                       pltpu.SemaphoreType.DMA, pltpu.SemaphoreType.DMA])
    def gather(y_hbm, idx_hbm, o_hbm, idx_v, rows0, rows1, gsem0, gsem1, wsem0, wsem1):
        wid = _sc_worker_id()
        base = wid * per_w
        pltpu.sync_copy(idx_hbm.at[wid], idx_v)
        rows, gsem, wsem = (rows0, rows1), (gsem0, gsem1), (wsem0, wsem1)

        def fetch(c, b):
            return pltpu.make_async_copy(y_hbm.at[idx_v.at[c]], rows[b], gsem[b])

        def write(c, b):
            r0 = pl.multiple_of(base + c * ch, ch)
            return pltpu.make_async_copy(rows[b], o_hbm.at[pl.ds(r0, ch)], wsem[b])

        fetch(0, 0).start()

        @pl.loop(0, n_ch, step=2)
        def _(c):
            fetch(c, 0).wait()

            @pl.when(c > 0)
            def _():
                write(c - 1, 1).wait()

            fetch(c + 1, 1).start()
            write(c, 0).start()
            fetch(c + 1, 1).wait()
            write(c, 0).wait()

            @pl.when(c + 2 < n_ch)
            def _():
                fetch(c + 2, 0).start()

            write(c + 1, 1).start()

        write(n_ch - 1, 1).wait()

    return gather(ys, idx)


def _max_slots(n_tokens):
    return N_EXPERTS + (n_tokens * TOP_K) // SLOT_ROWS


def _expert_kernel(se_ref, sr_ref,
                   xs_ref, wgu_hbm, wd_hbm, bgu_ref, bd_ref,
                   ys_ref,
                   xb_ref, act_ref, wbuf_ref, sem_ref, *, n_a, n_b):
    s = pl.program_id(0)
    rows = sr_ref[s]
    n_blk = (rows + ROW_BLOCK - 1) // ROW_BLOCK
    tw = wbuf_ref.shape[2] // 2
    d_ff = act_ref.shape[1]
    half_d = ys_ref.shape[1]

    def panel_copy(src, buf, p):
        return pltpu.make_async_copy(src, wbuf_ref.at[buf, :, pl.ds(p * tw, tw)], sem_ref.at[buf, p])

    def gate_up_panels(e, t, buf):
        c0 = pl.multiple_of(t * tw, tw)
        return (panel_copy(wgu_hbm.at[e, :, pl.ds(c0, tw)], buf, 0),
                panel_copy(wgu_hbm.at[e, :, pl.ds(d_ff + c0, tw)], buf, 1))

    def down_panels(e, u, buf):
        c0 = pl.multiple_of(u * tw, tw)
        return (panel_copy(wd_hbm.at[e, :, pl.ds(c0, tw)], buf, 0),
                panel_copy(wd_hbm.at[e, :, pl.ds(half_d + c0, tw)], buf, 1))

    def start(panels):
        for cp in panels:
            cp.start()

    def wait(panels):
        for cp in panels:
            cp.wait()

    expert = se_ref[s]

    def for_row_blocks(fn):
        n_big = n_blk // 8

        def body(rb, carry):
            fn(pl.multiple_of(rb * 8 * ROW_BLOCK, 8 * ROW_BLOCK), 8 * ROW_BLOCK)
            return carry

        lax.fori_loop(0, n_big, body, 0)
        for size in (4, 2, 1):
            @pl.when((n_blk // size) % 2 == 1)
            def _(size=size):
                first = (n_blk // (2 * size)) * 2 * size * ROW_BLOCK
                fn(pl.multiple_of(first, 2 * size * ROW_BLOCK), size * ROW_BLOCK)

    @pl.when(s == 0)
    def _():
        for k in range(RING_AHEAD):
            start(gate_up_panels(expert, k, k))

    def unpack(rb, carry):
        r0 = pl.multiple_of(rb * ROW_BLOCK, ROW_BLOCK)
        keep = (r0 + lax.broadcasted_iota(jnp.int32, (ROW_BLOCK, 1), 0)) < rows
        lo, hi = _unpack_bf16_pair(xs_ref[pl.ds(r0, ROW_BLOCK), :])
        xb_ref[pl.ds(r0, ROW_BLOCK), 0:half_d] = jnp.where(keep, lo, 0.0).astype(BF16)
        xb_ref[pl.ds(r0, ROW_BLOCK), half_d:2 * half_d] = jnp.where(keep, hi, 0.0).astype(BF16)
        return carry

    lax.fori_loop(0, n_blk, unpack, 0)

    ring = RING_AHEAD + 1

    def gate_up_tile(t, carry):
        buf = t % ring
        wait(gate_up_panels(expert, t, buf))
        nxt = t + RING_AHEAD

        @pl.when(nxt < n_a)
        def _():
            start(gate_up_panels(expert, nxt, nxt % ring))

        @pl.when(nxt >= n_a)
        def _():
            start(down_panels(expert, nxt - n_a, nxt % ring))

        col = pl.multiple_of(t * tw, tw)
        b_gate = bgu_ref[0, :, pl.ds(col, tw)]
        b_up = bgu_ref[0, :, pl.ds(d_ff + col, tw)]

        def gate_up(r0, m):
            gu = jnp.dot(xb_ref[pl.ds(r0, m), :], wbuf_ref[buf].astype(BF16), preferred_element_type=F32)
            gate = jnp.minimum(gu[:, 0:tw] + b_gate, SWIGLU_LIMIT)
            up = jnp.clip(gu[:, tw:2 * tw] + b_up, -SWIGLU_LIMIT, SWIGLU_LIMIT)
            act = (up + 1.0) * (gate * jax.nn.sigmoid(SWIGLU_ALPHA * gate))
            act_ref[pl.ds(r0, m), pl.ds(col, tw)] = act.astype(BF16)

        for_row_blocks(gate_up)
        return carry

    lax.fori_loop(0, n_a, gate_up_tile, 0)

    def down_tile(u, carry):
        buf = (n_a + u) % ring
        wait(down_panels(expert, u, buf))
        nxt = u + RING_AHEAD

        @pl.when(s + 1 < pl.num_programs(0))
        def _():
            start(gate_up_panels(se_ref[s + 1], nxt - n_b, (nxt - n_b) % ring))

        col = pl.multiple_of(u * tw, tw)
        b_lo = bd_ref[0, :, pl.ds(col, tw)]
        b_hi = bd_ref[0, :, pl.ds(half_d + col, tw)]

        def down(r0, m):
            y = jnp.dot(act_ref[pl.ds(r0, m), :], wbuf_ref[buf].astype(BF16), preferred_element_type=F32)
            ys_ref[pl.ds(r0, m), pl.ds(col, tw)] = _pack_bf16_pair(y[:, 0:tw] + b_lo, y[:, tw:2 * tw] + b_hi)

        for_row_blocks(down)
        return carry

    lax.fori_loop(0, n_b, down_tile, 0)

    def fill(rb, carry):
        r0 = pl.multiple_of(rb * ROW_BLOCK, ROW_BLOCK)
        ys_ref[pl.ds(r0, ROW_BLOCK), :] = jnp.zeros((ROW_BLOCK, half_d), jnp.uint32)
        return carry

    lax.fori_loop(n_blk, SLOT_ROWS // ROW_BLOCK, fill, 0)


def _experts(xs, slot_expert, slot_rows, n_used, w_gate_up, b_gate_up, w_down, b_down):
    n_e, d, two_ff = w_gate_up.shape
    d_ff = two_ff // 2
    assert d == d_ff, "the weight ring holds (rows, W_PANEL) panels of both projections"
    tw = W_PANEL
    n_a, n_b = d_ff // tw, (d // 2) // tw
    ring = RING_AHEAD + 1
    assert (n_a + n_b) % ring == 0, "ring slot of tile j must repeat from one expert slot to the next"
    assert n_b == RING_AHEAD <= n_a, "the look-ahead from a down tile must land in the next slot's gate/up tiles"
    total_rows = xs.shape[0]

    grid_spec = pltpu.PrefetchScalarGridSpec(
        num_scalar_prefetch=2,
        grid=(n_used,),
        in_specs=[
            pl.BlockSpec((SLOT_ROWS, d // 2), lambda s, se, sr: (s, 0)),
            pl.BlockSpec(memory_space=pl.ANY),
            pl.BlockSpec(memory_space=pl.ANY),
            pl.BlockSpec((1, 1, two_ff), lambda s, se, sr: (se[s], 0, 0)),
            pl.BlockSpec((1, 1, d), lambda s, se, sr: (se[s], 0, 0)),
        ],
        out_specs=pl.BlockSpec((SLOT_ROWS, d // 2), lambda s, se, sr: (s, 0)),
        scratch_shapes=[pltpu.VMEM((SLOT_ROWS, d), BF16),
                        pltpu.VMEM((SLOT_ROWS, d_ff), BF16),
                        pltpu.VMEM((ring, d, 2 * tw), F32),
                        pltpu.SemaphoreType.DMA((ring, 2))],
    )
    return pl.pallas_call(
        functools.partial(_expert_kernel, n_a=n_a, n_b=n_b),
        grid_spec=grid_spec,
        out_shape=jax.ShapeDtypeStruct((total_rows, d // 2), jnp.uint32),
        compiler_params=_tc_params(("arbitrary",)),
        name="experts",
    )(slot_expert, slot_rows, xs, w_gate_up, w_down,
      b_gate_up.reshape(n_e, 1, two_ff), b_down.reshape(n_e, 1, d))


def _slot_tables(counts, n_tokens):
    max_slots = _max_slots(n_tokens)
    n_slots = (counts + SLOT_ROWS - 1) // SLOT_ROWS
    slot_end = jnp.cumsum(n_slots)
    slot_start = slot_end - n_slots
    n_used = slot_end[-1]
    sid = jnp.arange(max_slots, dtype=jnp.int32)
    expert = jnp.minimum(jnp.sum(sid[:, None] >= slot_end[None, :], axis=1), N_EXPERTS - 1).astype(jnp.int32)
    local = sid - slot_start[expert]
    rows = jnp.clip(counts[expert] - local * SLOT_ROWS, 0, SLOT_ROWS)
    rows = jnp.where(sid < n_used, rows, 0).astype(jnp.int32)
    return expert, rows, n_used.reshape(1).astype(jnp.int32)


def _final_kernel(x1_ref, y0_ref, y1_ref, y2_ref, y3_ref, w_ref, mod_ref, g_ref, o_ref):
    half = y0_ref.shape[1]
    d = 2 * half
    moe_lo = moe_hi = None
    for k, y_ref in enumerate((y0_ref, y1_ref, y2_ref, y3_ref)):
        lo, hi = _unpack_bf16_pair(y_ref[...])
        wk = w_ref[:, k:k + 1]
        moe_lo = wk * lo if k == 0 else moe_lo + wk * lo
        moe_hi = wk * hi if k == 0 else moe_hi + wk * hi
    x_lo = x1_ref[:, 0:half] + mod_ref[0, 5:6, 0:half] * moe_lo
    x_hi = x1_ref[:, half:d] + mod_ref[0, 5:6, half:d] * moe_hi
    ss = jnp.sum(x_lo * x_lo, axis=-1, keepdims=True) + jnp.sum(x_hi * x_hi, axis=-1, keepdims=True)
    r = lax.rsqrt(ss / d + EPS)
    o_ref[:, 0:half] = x_lo * r * g_ref[:, 0:half]
    o_ref[:, half:d] = x_hi * r * g_ref[:, half:d]


def _final(x1, y4, w_tok, mod, final_g, seq):
    n, d = x1.shape
    tm = 256
    per_batch = seq // tm
    n_i = n // tm
    y_specs = [pl.BlockSpec((tm, d // 2), functools.partial(lambda i, k: (k * n_i + i, 0), k=k))
               for k in range(TOP_K)]
    return pl.pallas_call(
        _final_kernel,
        grid=(n_i,),
        in_specs=[pl.BlockSpec((tm, d), lambda i: (i, 0))] + y_specs + [
                  pl.BlockSpec((tm, TOP_K), lambda i: (i, 0)),
                  pl.BlockSpec((1, 6, d), lambda i: (i // per_batch, 0, 0)),
                  pl.BlockSpec((1, d), lambda i: (0, 0))],
        out_specs=pl.BlockSpec((tm, d), lambda i: (i, 0)),
        out_shape=jax.ShapeDtypeStruct((n, d), F32),
        compiler_params=_tc_params(("arbitrary",)),
        name="final",
    )(x1, y4, y4, y4, y4, w_tok, mod, final_g.reshape(1, d))


def kernel(x, c, norm1_g, w_mod, b_mod, w_in, conv_w, w_out, norm2_g, w_router, b_router,
           w_gate_up, b_gate_up, w_down, b_down, final_g):
    b, seq, d = x.shape
    n = b * seq
    x2 = x.reshape(n, d)
    mod = _modulation(c, w_mod[0], b_mod[0])
    proj = _in_projection(x2, mod, norm1_g[0], w_in[0].astype(BF16), seq)
    x1, h2, logits = _mixer(proj, x2, mod, conv_w[0], w_out[0].astype(BF16), norm2_g[0],
                            w_router[0], b_router[0], seq)
    pos, w_top, counts = _route(logits.T)
    slot_expert, slot_rows, n_used = _slot_tables(counts[:, 0].astype(jnp.int32), n)
    xs = _dispatch_rows(h2, pos)
    ys = _experts(xs, slot_expert, slot_rows, n_used[0], w_gate_up[0], b_gate_up[0], w_down[0], b_down[0])
    y4 = _combine_rows(ys, pos.reshape(TOP_K * n))
    out = _final(x1, y4, w_top.T, mod, final_g, seq)
    return out.reshape(b, seq, d)
```

```python
import functools
import math

import jax
import jax.numpy as jnp
from jax import lax
from jax.experimental import pallas as pl
from jax.experimental.pallas import tpu as pltpu
from jax.experimental.pallas import tpu_sc as plsc

F32 = jnp.float32
BF16 = jnp.bfloat16

CHUNK = 64
N_HEADS = 8
HEAD_DIM = 128
D_RET = N_HEADS * HEAD_DIM
D_CONV = 1024
CONV_WIDTH = 3
N_EXPERTS = 32
TOP_K = 4
SWIGLU_LIMIT = 7.0
SWIGLU_ALPHA = 1.702
ROPE_BASE = 10000.0
EPS = 1e-6

VMEM_LIMIT_BYTES = 62 * 1024 * 1024
SC_WORKERS = 32
SC_ROWS_PER_COPY = 32

MIX_BLOCK = 256
ROW_BLOCK = 128
SLOT_ROWS = 9 * ROW_BLOCK
W_PANEL = 512
RING_AHEAD = 3


def _tc_params(sem):
    return pltpu.CompilerParams(dimension_semantics=sem, vmem_limit_bytes=VMEM_LIMIT_BYTES)


def _bf16_bits(x):
    return lax.bitcast_convert_type(x.astype(BF16).astype(F32), jnp.uint32)


def _pack_bf16_pair(lo, hi):
    return (_bf16_bits(lo) >> 16) | _bf16_bits(hi)


def _unpack_bf16_pair(p):
    lo = lax.bitcast_convert_type(p << 16, F32)
    hi = lax.bitcast_convert_type(p & jnp.uint32(0xFFFF0000), F32)
    return lo, hi


def _mod_kernel(c_ref, w_ref, b_ref, o_ref):
    c = c_ref[...]
    ca = (c * jax.nn.sigmoid(c)).astype(BF16)
    o_ref[...] = jnp.dot(ca, w_ref[...].astype(BF16), preferred_element_type=F32) + b_ref[...]


def _modulation(c, w_mod, b_mod):
    b, d = c.shape
    n = w_mod.shape[1]
    tn = 1024
    c8 = jnp.zeros((8, d), F32).at[:b].set(c)
    out = pl.pallas_call(
        _mod_kernel,
        grid=(n // tn,),
        in_specs=[pl.BlockSpec((8, d), lambda j: (0, 0)),
                  pl.BlockSpec((d, tn), lambda j: (0, j)),
                  pl.BlockSpec((1, tn), lambda j: (0, j))],
        out_specs=pl.BlockSpec((8, tn), lambda j: (0, j)),
        out_shape=jax.ShapeDtypeStruct((8, n), F32),
        compiler_params=_tc_params(("arbitrary",)),
        name="mod",
    )(c8, w_mod, b_mod.reshape(1, n))
    return out[:b].reshape(b, 6, d)


def _norm_mod(x, g, scale, shift):
    y = x * lax.rsqrt(jnp.mean(x * x, axis=-1, keepdims=True) + EPS) * g
    return y * (1.0 + scale) + shift


def _inproj_kernel(x_ref, mod_ref, g_ref, w_ref, o_ref, h_ref, *, rows):
    @pl.when(pl.program_id(1) == 0)
    def _():
        g = g_ref[...]
        scale = mod_ref[0, 1:2, :]
        shift = mod_ref[0, 0:1, :]

        def body(r, carry):
            r0 = pl.multiple_of(r * rows, rows)
            h = _norm_mod(x_ref[pl.ds(r0, rows), :], g, scale, shift)
            h_ref[pl.ds(r0, rows), :] = h.astype(BF16)
            return carry

        lax.fori_loop(0, x_ref.shape[0] // rows, body, 0)

    o_ref[...] = jnp.dot(h_ref[...], w_ref[...], preferred_element_type=F32).astype(o_ref.dtype)


def _in_projection(x2, mod, g1, w_in_bf, seq):
    n, d = x2.shape
    p = w_in_bf.shape[1]
    tm, tn = min(1024, seq), 1792
    per_batch = seq // tm
    return pl.pallas_call(
        functools.partial(_inproj_kernel, rows=128),
        grid=(n // tm, p // tn),
        in_specs=[pl.BlockSpec((tm, d), lambda i, j: (i, 0)),
                  pl.BlockSpec((1, 6, d), lambda i, j: (i // per_batch, 0, 0)),
                  pl.BlockSpec((1, d), lambda i, j: (0, 0)),
                  pl.BlockSpec((d, tn), lambda i, j: (0, j))],
        out_specs=pl.BlockSpec((tm, tn), lambda i, j: (i, j)),
        out_shape=jax.ShapeDtypeStruct((n, p), BF16),
        scratch_shapes=[pltpu.VMEM((tm, d), BF16)],
        compiler_params=_tc_params(("arbitrary", "arbitrary")),
        name="inproj",
    )(x2, mod, g1.reshape(1, d), w_in_bf)


def _mixer_tables(seq, blk):
    half = HEAD_DIM // 2
    freqs = ROPE_BASE ** (-jnp.arange(half, dtype=F32) / half)
    ang = jnp.arange(seq, dtype=F32)[:, None] * freqs[None, :]
    cos = jnp.concatenate([jnp.cos(ang), jnp.cos(ang)], axis=-1)
    sin = jnp.concatenate([-jnp.sin(ang), jnp.sin(ang)], axis=-1)
    log_gamma = jnp.log1p(-jnp.exp2(-5.0 - jnp.arange(N_HEADS, dtype=F32)))
    idx = jnp.arange(blk, dtype=F32)
    dist = jnp.abs(idx[:, None] - idx[None, :])
    ck = jnp.arange(blk) // CHUNK
    visible = (ck[None, :] <= ck[:, None]).astype(F32)
    dmask = jnp.exp(log_gamma[:, None, None] * dist) * visible
    ones = jnp.ones((1, 1, HEAD_DIM), F32)
    dq = jnp.exp(log_gamma[:, None] * (idx + 1.0)[None])[:, :, None] * ones
    dkv = jnp.exp(log_gamma[:, None] * (blk - 1 - idx)[None])[:, :, None] * ones
    dblk = jnp.exp(log_gamma * blk)[:, None, None] * ones
    return cos, sin, dmask, dq, dkv, dblk


def _mixer_kernel(proj_ref, x_ref, mod_ref, cos_ref, sin_ref, dmask_ref, dq_ref, dkv_ref, dblk_ref,
                  convw_ref, wout_ref, g2_ref, wr_ref, br_ref,
                  x1_ref, h2_ref, lg_ref,
                  s_ref, zbuf_ref, y_ref, *, per_batch, n_blocks):
    blk = x_ref.shape[0]
    i = pl.program_id(0)
    cur = i % 2

    @pl.when(jnp.minimum(i, n_blocks - 1) % per_batch == 0)
    def _():
        s_ref[...] = jnp.zeros_like(s_ref)
        zbuf_ref[0:8, :] = jnp.zeros((8, D_CONV), F32)

    @pl.when(i == 0)
    def _():
        y_ref[1] = jnp.zeros(y_ref.shape[1:], BF16)

    cos = cos_ref[...]
    sin = sin_ref[...]
    k_scale = HEAD_DIM ** -0.5
    nt = (((1,), (1,)), ((), ()))
    tn = (((0,), (0,)), ((), ()))
    for h in range(N_HEADS):
        c0 = h * HEAD_DIM
        q = proj_ref[:, c0:c0 + HEAD_DIM].astype(F32)
        k = proj_ref[:, D_RET + c0:D_RET + c0 + HEAD_DIM].astype(F32)
        v = proj_ref[:, 2 * D_RET + c0:2 * D_RET + c0 + HEAD_DIM]
        g = proj_ref[:, 3 * D_RET + c0:3 * D_RET + c0 + HEAD_DIM].astype(F32)
        qr = q * cos + pltpu.roll(q, HEAD_DIM // 2, 1) * sin
        kr = (k * cos + pltpu.roll(k, HEAD_DIM // 2, 1) * sin) * k_scale
        qb = qr.astype(BF16)
        kb = kr.astype(BF16)
        scores = lax.dot_general(qb, kb, nt, preferred_element_type=F32) * dmask_ref[h]
        intra = jnp.dot(scores.astype(BF16), v, preferred_element_type=F32)
        state = s_ref[h]
        cross = jnp.dot(qb, state.astype(BF16), preferred_element_type=F32) * dq_ref[h]
        kd = (kr * dkv_ref[h]).astype(BF16)
        kv = lax.dot_general(kd, v, tn, preferred_element_type=F32)
        s_ref[h] = dblk_ref[h] * state + kv
        o = intra + cross
        mu = jnp.mean(o, axis=-1, keepdims=True)
        dev = o - mu
        var = jnp.mean(dev * dev, axis=-1, keepdims=True)
        on = dev * lax.rsqrt(var + EPS)
        y_ref[cur, :, c0:c0 + HEAD_DIM] = (g * jax.nn.sigmoid(g) * on).astype(BF16)

    cw = 256
    base = 4 * D_RET
    for cb in range(D_CONV // cw):
        lo = cb * cw
        cg = proj_ref[:, base + D_CONV + lo:base + D_CONV + lo + cw].astype(F32)
        u = proj_ref[:, base + 2 * D_CONV + lo:base + 2 * D_CONV + lo + cw].astype(F32)
        zbuf_ref[8:blk + 8, lo:lo + cw] = cg * u
    for cb in range(D_CONV // cw):
        lo = cb * cw
        z0 = zbuf_ref[8:blk + 8, lo:lo + cw]
        z1 = zbuf_ref[7:blk + 7, lo:lo + cw]
        z2 = zbuf_ref[6:blk + 6, lo:lo + cw]
        z = (convw_ref[2:3, lo:lo + cw] * z0 + convw_ref[1:2, lo:lo + cw] * z1
             + convw_ref[0:1, lo:lo + cw] * z2)
        bg = proj_ref[:, base + lo:base + lo + cw].astype(F32)
        y_ref[cur, :, D_RET + lo:D_RET + lo + cw] = (bg * z).astype(BF16)
    zbuf_ref[0:8, :] = zbuf_ref[blk:blk + 8, :]

    mix = jnp.dot(y_ref[1 - cur], wout_ref[...], preferred_element_type=F32)
    x1 = x_ref[...] + mod_ref[0, 2:3, :] * mix
    x1_ref[...] = x1
    h2 = _norm_mod(x1, g2_ref[...], mod_ref[0, 4:5, :], mod_ref[0, 3:4, :])
    half = h2.shape[1] // 2
    h2_ref[...] = _pack_bf16_pair(h2[:, :half], h2[:, half:])
    w2 = wr_ref[...]
    w_hi = w2.astype(BF16)
    w_lo = (w2 - w_hi.astype(F32)).astype(BF16)
    w_split = jnp.where(lax.broadcasted_iota(jnp.int32, w2.shape, 1) < N_EXPERTS, w_hi, w_lo)
    h_hi = h2.astype(BF16)
    h_lo = (h2 - h_hi.astype(F32)).astype(BF16)
    r = (jnp.dot(h_hi, w_split, preferred_element_type=F32)
         + jnp.dot(h_lo, w_split, preferred_element_type=F32))
    lg_ref[...] = r[:, 0:N_EXPERTS] + r[:, N_EXPERTS:2 * N_EXPERTS] + br_ref[...]


def _mixer(proj, x2, mod, conv_w, w_out_bf, g2, w_router, b_router, seq):
    n, d = x2.shape
    blk = MIX_BLOCK
    per_batch = seq // blk
    cos, sin, dmask, dq, dkv, dblk = _mixer_tables(seq, blk)
    const2 = lambda i: (0, 0)
    const3 = lambda i: (0, 0, 0)
    n_blocks = n // blk
    mixed = lambda i: jnp.minimum(i, n_blocks - 1)
    done = lambda i: jnp.maximum(i - 1, 0)
    return pl.pallas_call(
        functools.partial(_mixer_kernel, per_batch=per_batch, n_blocks=n_blocks),
        grid=(n_blocks + 1,),
        in_specs=[pl.BlockSpec((blk, proj.shape[1]), lambda i: (mixed(i), 0)),
                  pl.BlockSpec((blk, d), lambda i: (done(i), 0)),
                  pl.BlockSpec((1, 6, d), lambda i: (done(i) // per_batch, 0, 0)),
                  pl.BlockSpec((blk, HEAD_DIM), lambda i: (mixed(i) % per_batch, 0)),
                  pl.BlockSpec((blk, HEAD_DIM), lambda i: (mixed(i) % per_batch, 0)),
                  pl.BlockSpec((N_HEADS, blk, blk), const3),
                  pl.BlockSpec((N_HEADS, blk, HEAD_DIM), const3),
                  pl.BlockSpec((N_HEADS, blk, HEAD_DIM), const3),
                  pl.BlockSpec((N_HEADS, 1, HEAD_DIM), const3),
                  pl.BlockSpec((CONV_WIDTH, D_CONV), const2),
                  pl.BlockSpec((D_RET + D_CONV, d), const2),
                  pl.BlockSpec((1, d), const2),
                  pl.BlockSpec((d, 2 * N_EXPERTS), const2),
                  pl.BlockSpec((1, N_EXPERTS), const2)],
        out_specs=[pl.BlockSpec((blk, d), lambda i: (done(i), 0)),
                   pl.BlockSpec((blk, d // 2), lambda i: (done(i), 0)),
                   pl.BlockSpec((blk, N_EXPERTS), lambda i: (done(i), 0))],
        out_shape=[jax.ShapeDtypeStruct((n, d), F32),
                   jax.ShapeDtypeStruct((n, d // 2), jnp.uint32),
                   jax.ShapeDtypeStruct((n, N_EXPERTS), F32)],
        scratch_shapes=[pltpu.VMEM((N_HEADS, HEAD_DIM, HEAD_DIM), F32),
                        pltpu.VMEM((blk + 8, D_CONV), F32),
                        pltpu.VMEM((2, blk, D_RET + D_CONV), BF16)],
        compiler_params=_tc_params(("arbitrary",)),
        name="mixer",
    )(proj, x2, mod, cos, sin, dmask, dq, dkv, dblk, conv_w, w_out_bf, g2.reshape(1, d),
      jnp.concatenate([w_router, w_router], axis=1), b_router.reshape(1, N_EXPERTS))


def _top_k_block(lg_ref, tri_ref, run_ref):
    n_e, bw = lg_ref.shape

    @pl.when(pl.program_id(0) == 0)
    def _():
        run_ref[...] = jnp.zeros_like(run_ref)

    l = lg_ref[...]
    eio = lax.broadcasted_iota(jnp.int32, (n_e, bw), 0)
    vals, hots = [], []
    for _ in range(TOP_K):
        m = jnp.max(l, axis=0, keepdims=True)
        idx = jnp.min(jnp.where(l == m, eio, n_e), axis=0, keepdims=True)
        hot = eio == idx
        vals.append(m)
        hots.append(hot)
        l = jnp.where(hot, -jnp.inf, l)
    sel = sum(jnp.where(hot, 1.0, 0.0) for hot in hots)
    incl = jnp.dot(sel.astype(BF16), tri_ref[...], preferred_element_type=F32)
    run = run_ref[:, 0:1]
    rank = incl - sel + run
    run_ref[...] = jnp.broadcast_to(run + incl[:, bw - 1:bw], run_ref.shape)
    return vals, hots, rank


def _count_kernel(lg_ref, tri_ref, cnt_ref, run_ref):
    _top_k_block(lg_ref, tri_ref, run_ref)
    cnt_ref[...] = run_ref[...]


def _route_kernel(lg_ref, tri_ref, low_ref, tot_ref, pos_ref, w_ref, run_ref, *, max_slots):
    vals, hots, rank = _top_k_block(lg_ref, tri_ref, run_ref)
    ex = [jnp.exp(v - vals[0]) for v in vals]
    den = ex[0] + ex[1] + ex[2] + ex[3]
    for k in range(TOP_K):
        w_ref[k:k + 1, :] = ex[k] / den
    tot = tot_ref[...]
    n_slots = sum(jnp.where(tot > float(m * SLOT_ROWS), 1.0, 0.0) for m in range(max_slots))
    start = jnp.dot(low_ref[...], n_slots.astype(BF16), preferred_element_type=F32)
    dest = start[:, 0:1] * float(SLOT_ROWS) + rank
    for k in range(TOP_K):
        pos_ref[k:k + 1, :] = jnp.sum(jnp.where(hots[k], dest, 0.0), axis=0,
                                      keepdims=True).astype(jnp.int32)


def _route(logits_t):
    n_e, n = logits_t.shape
    bw = 512
    max_slots = -(-n // SLOT_ROWS)
    tri = (jnp.arange(bw)[:, None] <= jnp.arange(bw)[None, :]).astype(BF16)
    low = (jnp.arange(n_e)[None, :] < jnp.arange(n_e)[:, None]).astype(BF16)
    lg_spec = pl.BlockSpec((n_e, bw), lambda c: (0, c))
    tri_spec = pl.BlockSpec((bw, bw), lambda c: (0, 0))
    cnt_spec = pl.BlockSpec((n_e, 128), lambda c: (0, 0))
    run = pltpu.VMEM((n_e, 128), F32)
    counts = pl.pallas_call(
        _count_kernel,
        grid=(n // bw,),
        in_specs=[lg_spec, tri_spec],
        out_specs=cnt_spec,
        out_shape=jax.ShapeDtypeStruct((n_e, 128), F32),
        scratch_shapes=[run],
        compiler_params=_tc_params(("arbitrary",)),
        name="route_count",
    )(logits_t, tri)
    pos, w_top = pl.pallas_call(
        functools.partial(_route_kernel, max_slots=max_slots),
        grid=(n // bw,),
        in_specs=[lg_spec, tri_spec, pl.BlockSpec((n_e, n_e), lambda c: (0, 0)), cnt_spec],
        out_specs=[pl.BlockSpec((TOP_K, bw), lambda c: (0, c)),
                   pl.BlockSpec((TOP_K, bw), lambda c: (0, c))],
        out_shape=[jax.ShapeDtypeStruct((TOP_K, n), jnp.int32),
                   jax.ShapeDtypeStruct((TOP_K, n), F32)],
        scratch_shapes=[run],
        compiler_params=_tc_params(("arbitrary",)),
        name="route",
    )(logits_t, tri, low, counts)
    return pos, w_top, counts


def _sc_mesh():
    return plsc.VectorSubcoreMesh(core_axis_name="c", subcore_axis_name="s")


def _sc_worker_id():
    return lax.axis_index("s") * lax.axis_size("c") + lax.axis_index("c")


def _dispatch_rows(h2, pos):
    n, d = h2.shape
    ch = SC_ROWS_PER_COPY
    per_w = n // SC_WORKERS
    n_ch = per_w // ch
    total_rows = _max_slots(n) * SLOT_ROWS
    idx = pos.reshape(TOP_K, SC_WORKERS, n_ch, ch).transpose(1, 2, 0, 3).reshape(
        SC_WORKERS, n_ch * TOP_K, ch)

    @functools.partial(
        pl.kernel, mesh=_sc_mesh(),
        out_type=jax.ShapeDtypeStruct((total_rows, d), h2.dtype),
        scratch_types=[pltpu.VMEM((n_ch * TOP_K, ch), jnp.int32),
                       pltpu.VMEM((ch, d), h2.dtype), pltpu.VMEM((ch, d), h2.dtype),
                       pltpu.SemaphoreType.DMA, pltpu.SemaphoreType.DMA,
                       pltpu.SemaphoreType.DMA, pltpu.SemaphoreType.DMA])
    def scatter(x_hbm, idx_hbm, o_hbm, idx_v, rows0, rows1, rsem0, rsem1, ssem0, ssem1):
        wid = _sc_worker_id()
        base = wid * per_w
        pltpu.sync_copy(idx_hbm.at[wid], idx_v)
        rows, rsem, ssem = (rows0, rows1), (rsem0, rsem1), (ssem0, ssem1)

        def read(c, b):
            r0 = pl.multiple_of(base + c * ch, ch)
            return pltpu.make_async_copy(x_hbm.at[pl.ds(r0, ch)], rows[b], rsem[b])

        def send(c, k, b):
            return pltpu.make_async_copy(rows[b], o_hbm.at[idx_v.at[c * TOP_K + k]], ssem[b])

        read(0, 0).start()

        @pl.loop(0, n_ch, step=2)
        def _(c):
            read(c, 0).wait()

            @pl.when(c > 0)
            def _():
                for k in range(TOP_K):
                    send(c - 1, k, 1).wait()

            read(c + 1, 1).start()
            for k in range(TOP_K):
                send(c, k, 0).start()
            read(c + 1, 1).wait()
            for k in range(TOP_K):
                send(c, k, 0).wait()

            @pl.when(c + 2 < n_ch)
            def _():
                read(c + 2, 0).start()

            for k in range(TOP_K):
                send(c + 1, k, 1).start()

        for k in range(TOP_K):
            send(n_ch - 1, k, 1).wait()

    return scatter(h2, idx)


def _combine_rows(ys, pos_flat):
    n = pos_flat.shape[0]
    d = ys.shape[1]
    ch = SC_ROWS_PER_COPY
    per_w = n // SC_WORKERS
    n_ch = per_w // ch
    idx = pos_flat.reshape(SC_WORKERS, n_ch, ch)

    @functools.partial(
        pl.kernel, mesh=_sc_mesh(),
        out_type=jax.ShapeDtypeStruct((n, d), ys.dtype),
        scratch_types=[pltpu.VMEM((n_ch, ch), jnp.int32),
                       pltpu.VMEM((ch, d), ys.dtype), pltpu.VMEM((ch, d), ys.dtype),
                       pltpu.SemaphoreType.DMA, pltpu.SemaphoreType.DMA,
                       pltpu.SemaphoreType.DMA, pltpu.SemaphoreType.DMA])
    def gather(y_hbm, idx_hbm, o_hbm, idx_v, rows0, rows1, gsem0, gsem1, wsem0, wsem1):
        wid = _sc_worker_id()
        base = wid * per_w
        pltpu.sync_copy(idx_hbm.at[wid], idx_v)
        rows, gsem, wsem = (rows0, rows1), (gsem0, gsem1), (wsem0, wsem1)

        def fetch(c, b):
            return pltpu.make_async_copy(y_hbm.at[idx_v.at[c]], rows[b], gsem[b])

        def write(c, b):
            r0 = pl.multiple_of(base + c * ch, ch)
            return pltpu.make_async_copy(rows[b], o_hbm.at[pl.ds(r0, ch)], wsem[b])

        fetch(0, 0).start()

        @pl.loop(0, n_ch, step=2)
        def _(c):
            fetch(c, 0).wait()

            @pl.when(c > 0)
            def _():
                write(c - 1, 1).wait()

            fetch(c + 1, 1).start()
            write(c, 0).start()
            fetch(c + 1, 1).wait()
            write(c, 0).wait()

            @pl.when(c + 2 < n_ch)
            def _():
                fetch(c + 2, 0).start()

            write(c + 1, 1).start()

        write(n_ch - 1, 1).wait()

    return gather(ys, idx)


def _max_slots(n_tokens):
    return N_EXPERTS + (n_tokens * TOP_K) // SLOT_ROWS


def _expert_kernel(se_ref, sr_ref,
                   xs_hbm, wgu_hbm, wd_hbm, bgu_ref, bd_ref,
                   ys_ref,
                   xs_ref, xb_ref, act_ref, wbuf_ref, xsem_ref, sem_ref, *, n_a, n_b):
    s = pl.program_id(0)
    n_slots = pl.num_programs(0)
    rows = sr_ref[s]
    n_blk = (rows + ROW_BLOCK - 1) // ROW_BLOCK
    tw = wbuf_ref.shape[2] // 2
    d_ff = act_ref.shape[1]
    half_d = ys_ref.shape[1]

    def panel_copy(src, buf, p):
        return pltpu.make_async_copy(src, wbuf_ref.at[buf, :, pl.ds(p * tw, tw)], sem_ref.at[buf, p])

    def gate_up_panels(e, t, buf):
        c0 = pl.multiple_of(t * tw, tw)
        return (panel_copy(wgu_hbm.at[e, :, pl.ds(c0, tw)], buf, 0),
                panel_copy(wgu_hbm.at[e, :, pl.ds(d_ff + c0, tw)], buf, 1))

    def down_panels(e, u, buf):
        c0 = pl.multiple_of(u * tw, tw)
        return (panel_copy(wd_hbm.at[e, :, pl.ds(c0, tw)], buf, 0),
                panel_copy(wd_hbm.at[e, :, pl.ds(half_d + c0, tw)], buf, 1))

    def start(panels):
        for cp in panels:
            cp.start()

    def wait(panels):
        for cp in panels:
            cp.wait()

    expert = se_ref[s]

    def for_row_blocks(fn):
        n_big = n_blk // 8

        def body(rb, carry):
            fn(pl.multiple_of(rb * 8 * ROW_BLOCK, 8 * ROW_BLOCK), 8 * ROW_BLOCK)
            return carry

        lax.fori_loop(0, n_big, body, 0)
        for size in (4, 2, 1):
            @pl.when((n_blk // size) % 2 == 1)
            def _(size=size):
                first = (n_blk // (2 * size)) * 2 * size * ROW_BLOCK
                fn(pl.multiple_of(first, 2 * size * ROW_BLOCK), size * ROW_BLOCK)

    tiles = n_a + n_b
    ring = RING_AHEAD + 1
    first = s * tiles

    def rows_copy(slot):
        r0 = pl.multiple_of(slot * SLOT_ROWS, SLOT_ROWS)
        return pltpu.make_async_copy(xs_hbm.at[pl.ds(r0, SLOT_ROWS)], xs_ref, xsem_ref.at[0])

    def start_ahead(k):
        nxt = k + RING_AHEAD
        buf = (first + nxt) % ring

        @pl.when(nxt < n_a)
        def _():
            start(gate_up_panels(expert, nxt, buf))

        @pl.when((nxt >= n_a) & (nxt < tiles))
        def _():
            start(down_panels(expert, nxt - n_a, buf))

        @pl.when((nxt >= tiles) & (s + 1 < n_slots))
        def _():
            start(gate_up_panels(se_ref[s + 1], nxt - tiles, buf))

    @pl.when(s == 0)
    def _():
        rows_copy(0).start()
        for k in range(RING_AHEAD):
            start(gate_up_panels(expert, k, k))

    rows_copy(s).wait()

    def unpack(rb, carry):
        r0 = pl.multiple_of(rb * ROW_BLOCK, ROW_BLOCK)
        keep = (r0 + lax.broadcasted_iota(jnp.int32, (ROW_BLOCK, 1), 0)) < rows
        lo, hi = _unpack_bf16_pair(xs_ref[pl.ds(r0, ROW_BLOCK), :])
        xb_ref[pl.ds(r0, ROW_BLOCK), 0:half_d] = jnp.where(keep, lo, 0.0).astype(BF16)
        xb_ref[pl.ds(r0, ROW_BLOCK), half_d:2 * half_d] = jnp.where(keep, hi, 0.0).astype(BF16)
        return carry

    lax.fori_loop(0, n_blk, unpack, 0)

    @pl.when(s + 1 < n_slots)
    def _():
        rows_copy(s + 1).start()

    def gate_up_tile(t, carry):
        buf = (first + t) % ring
        wait(gate_up_panels(expert, t, buf))
        start_ahead(t)

        col = pl.multiple_of(t * tw, tw)
        b_gate = bgu_ref[0, :, pl.ds(col, tw)]
        b_up = bgu_ref[0, :, pl.ds(d_ff + col, tw)]

        def gate_up(r0, m):
            gu = jnp.dot(xb_ref[pl.ds(r0, m), :], wbuf_ref[buf].astype(BF16), preferred_element_type=F32)
            gate = jnp.minimum(gu[:, 0:tw] + b_gate, SWIGLU_LIMIT)
            up = jnp.clip(gu[:, tw:2 * tw] + b_up, -SWIGLU_LIMIT, SWIGLU_LIMIT)
            act = (up + 1.0) * (gate * jax.nn.sigmoid(SWIGLU_ALPHA * gate))
            act_ref[pl.ds(r0, m), pl.ds(col, tw)] = act.astype(BF16)

        for_row_blocks(gate_up)
        return carry

    lax.fori_loop(0, n_a, gate_up_tile, 0)

    def down_tile(u, carry):
        buf = (first + n_a + u) % ring
        wait(down_panels(expert, u, buf))
        start_ahead(n_a + u)

        col = pl.multiple_of(u * tw, tw)
        b_lo = bd_ref[0, :, pl.ds(col, tw)]
        b_hi = bd_ref[0, :, pl.ds(half_d + col, tw)]

        def down(r0, m):
            y = jnp.dot(act_ref[pl.ds(r0, m), :], wbuf_ref[buf].astype(BF16), preferred_element_type=F32)
            ys_ref[pl.ds(r0, m), pl.ds(col, tw)] = _pack_bf16_pair(y[:, 0:tw] + b_lo, y[:, tw:2 * tw] + b_hi)

        for_row_blocks(down)
        return carry

    lax.fori_loop(0, n_b, down_tile, 0)

    def fill(rb, carry):
        r0 = pl.multiple_of(rb * ROW_BLOCK, ROW_BLOCK)
        ys_ref[pl.ds(r0, ROW_BLOCK), :] = jnp.zeros((ROW_BLOCK, half_d), jnp.uint32)
        return carry

    lax.fori_loop(n_blk, SLOT_ROWS // ROW_BLOCK, fill, 0)


def _experts(xs, slot_expert, slot_rows, n_used, w_gate_up, b_gate_up, w_down, b_down):
    n_e, d, two_ff = w_gate_up.shape
    d_ff = two_ff // 2
    assert d == d_ff, "the weight ring holds (rows, W_PANEL) panels of both projections"
    tw = W_PANEL
    n_a, n_b = d_ff // tw, (d // 2) // tw
    ring = RING_AHEAD + 1
    assert RING_AHEAD <= n_a, "a look-ahead past the slot's last tile must land in the next slot's gate/up tiles"
    total_rows = xs.shape[0]

    grid_spec = pltpu.PrefetchScalarGridSpec(
        num_scalar_prefetch=2,
        grid=(n_used,),
        in_specs=[
            pl.BlockSpec(memory_space=pl.ANY),
            pl.BlockSpec(memory_space=pl.ANY),
            pl.BlockSpec(memory_space=pl.ANY),
            pl.BlockSpec((1, 1, two_ff), lambda s, se, sr: (se[s], 0, 0)),
            pl.BlockSpec((1, 1, d), lambda s, se, sr: (se[s], 0, 0)),
        ],
        out_specs=pl.BlockSpec((SLOT_ROWS, d // 2), lambda s, se, sr: (s, 0)),
        scratch_shapes=[pltpu.VMEM((SLOT_ROWS, d // 2), jnp.uint32),
                        pltpu.VMEM((SLOT_ROWS, d), BF16),
                        pltpu.VMEM((SLOT_ROWS, d_ff), BF16),
                        pltpu.VMEM((ring, d, 2 * tw), F32),
                        pltpu.SemaphoreType.DMA((1,)),
                        pltpu.SemaphoreType.DMA((ring, 2))],
    )
    return pl.pallas_call(
        functools.partial(_expert_kernel, n_a=n_a, n_b=n_b),
        grid_spec=grid_spec,
        out_shape=jax.ShapeDtypeStruct((total_rows, d // 2), jnp.uint32),
        compiler_params=_tc_params(("arbitrary",)),
        name="experts",
    )(slot_expert, slot_rows, xs, w_gate_up, w_down,
      b_gate_up.reshape(n_e, 1, two_ff), b_down.reshape(n_e, 1, d))


def _slot_tables(counts, n_tokens):
    max_slots = _max_slots(n_tokens)
    n_slots = (counts + SLOT_ROWS - 1) // SLOT_ROWS
    slot_end = jnp.cumsum(n_slots)
    slot_start = slot_end - n_slots
    n_used = slot_end[-1]
    sid = jnp.arange(max_slots, dtype=jnp.int32)
    expert = jnp.minimum(jnp.sum(sid[:, None] >= slot_end[None, :], axis=1), N_EXPERTS - 1).astype(jnp.int32)
    local = sid - slot_start[expert]
    rows = jnp.clip(counts[expert] - local * SLOT_ROWS, 0, SLOT_ROWS)
    rows = jnp.where(sid < n_used, rows, 0).astype(jnp.int32)
    return expert, rows, n_used.reshape(1).astype(jnp.int32)


def _final_kernel(x1_ref, y0_ref, y1_ref, y2_ref, y3_ref, w_ref, mod_ref, g_ref, o_ref):
    half = y0_ref.shape[1]
    d = 2 * half
    moe_lo = moe_hi = None
    for k, y_ref in enumerate((y0_ref, y1_ref, y2_ref, y3_ref)):
        lo, hi = _unpack_bf16_pair(y_ref[...])
        wk = w_ref[:, k:k + 1]
        moe_lo = wk * lo if k == 0 else moe_lo + wk * lo
        moe_hi = wk * hi if k == 0 else moe_hi + wk * hi
    x_lo = x1_ref[:, 0:half] + mod_ref[0, 5:6, 0:half] * moe_lo
    x_hi = x1_ref[:, half:d] + mod_ref[0, 5:6, half:d] * moe_hi
    ss = jnp.sum(x_lo * x_lo, axis=-1, keepdims=True) + jnp.sum(x_hi * x_hi, axis=-1, keepdims=True)
    r = lax.rsqrt(ss / d + EPS)
    o_ref[:, 0:half] = x_lo * r * g_ref[:, 0:half]
    o_ref[:, half:d] = x_hi * r * g_ref[:, half:d]


def _final(x1, y4, w_tok, mod, final_g, seq):
    n, d = x1.shape
    tm = 256
    per_batch = seq // tm
    n_i = n // tm
    y_specs = [pl.BlockSpec((tm, d // 2), functools.partial(lambda i, k: (k * n_i + i, 0), k=k))
               for k in range(TOP_K)]
    return pl.pallas_call(
        _final_kernel,
        grid=(n_i,),
        in_specs=[pl.BlockSpec((tm, d), lambda i: (i, 0))] + y_specs + [
                  pl.BlockSpec((tm, TOP_K), lambda i: (i, 0)),
                  pl.BlockSpec((1, 6, d), lambda i: (i // per_batch, 0, 0)),
                  pl.BlockSpec((1, d), lambda i: (0, 0))],
        out_specs=pl.BlockSpec((tm, d), lambda i: (i, 0)),
        out_shape=jax.ShapeDtypeStruct((n, d), F32),
        compiler_params=_tc_params(("arbitrary",)),
        name="final",
    )(x1, y4, y4, y4, y4, w_tok, mod, final_g.reshape(1, d))


def kernel(x, c, norm1_g, w_mod, b_mod, w_in, conv_w, w_out, norm2_g, w_router, b_router,
           w_gate_up, b_gate_up, w_down, b_down, final_g):
    b, seq, d = x.shape
    n = b * seq
    x2 = x.reshape(n, d)
    mod = _modulation(c, w_mod[0], b_mod[0])
    proj = _in_projection(x2, mod, norm1_g[0], w_in[0].astype(BF16), seq)
    x1, h2, logits = _mixer(proj, x2, mod, conv_w[0], w_out[0].astype(BF16), norm2_g[0],
                            w_router[0], b_router[0], seq)
    pos, w_top, counts = _route(logits.T)
    slot_expert, slot_rows, n_used = _slot_tables(counts[:, 0].astype(jnp.int32), n)
    xs = _dispatch_rows(h2, pos)
    ys = _experts(xs, slot_expert, slot_rows, n_used[0], w_gate_up[0], b_gate_up[0], w_down[0], b_down[0])
    y4 = _combine_rows(ys, pos.reshape(TOP_K * n))
    out = _final(x1, y4, w_top.T, mod, final_g, seq)
    return out.reshape(b, seq, d)
```

```python
import functools

import jax
import jax.numpy as jnp
from jax import lax
from jax.experimental import pallas as pl
from jax.experimental.pallas import tpu as pltpu
from jax.experimental.pallas import tpu_sc as plsc

F32 = jnp.float32
BF16 = jnp.bfloat16

CHUNK = 64
N_HEADS = 8
HEAD_DIM = 128
D_RET = N_HEADS * HEAD_DIM
D_CONV = 1024
CONV_WIDTH = 3
N_EXPERTS = 32
TOP_K = 4
SWIGLU_LIMIT = 7.0
SWIGLU_ALPHA = 1.702
ROPE_BASE = 10000.0
EPS = 1e-6

VMEM_LIMIT_BYTES = 58 * 1024 * 1024
SC_WORKERS = 32
SC_ROWS_PER_COPY = 32

MIX_BLOCK = 256
ROW_BLOCK = 128
SLOT_ROWS = 9 * ROW_BLOCK
W_PANEL = 512
RING_AHEAD = 2


def _tc_params(sem):
    return pltpu.CompilerParams(dimension_semantics=sem, vmem_limit_bytes=VMEM_LIMIT_BYTES)


def _bf16_bits(x):
    return lax.bitcast_convert_type(x.astype(BF16).astype(F32), jnp.uint32)


def _pack_bf16_pair(lo, hi):
    return (_bf16_bits(lo) >> 16) | _bf16_bits(hi)


def _unpack_bf16_pair(p):
    lo = lax.bitcast_convert_type(p << 16, F32)
    hi = lax.bitcast_convert_type(p & jnp.uint32(0xFFFF0000), F32)
    return lo, hi


def _mod_kernel(c_ref, w_ref, b_ref, o_ref):
    c = c_ref[...]
    ca = (c * jax.nn.sigmoid(c)).astype(BF16)
    o_ref[...] = jnp.dot(ca, w_ref[...].astype(BF16), preferred_element_type=F32) + b_ref[...]


def _modulation(c, w_mod, b_mod):
    b, d = c.shape
    n = w_mod.shape[1]
    tn = 1024
    c8 = jnp.zeros((8, d), F32).at[:b].set(c)
    out = pl.pallas_call(
        _mod_kernel,
        grid=(n // tn,),
        in_specs=[pl.BlockSpec((8, d), lambda j: (0, 0)),
                  pl.BlockSpec((d, tn), lambda j: (0, j)),
                  pl.BlockSpec((1, tn), lambda j: (0, j))],
        out_specs=pl.BlockSpec((8, tn), lambda j: (0, j)),
        out_shape=jax.ShapeDtypeStruct((8, n), F32),
        compiler_params=_tc_params(("arbitrary",)),
        name="mod",
    )(c8, w_mod, b_mod.reshape(1, n))
    return out[:b].reshape(b, 6, d)


def _norm_mod(x, g, scale, shift):
    y = x * lax.rsqrt(jnp.mean(x * x, axis=-1, keepdims=True) + EPS) * g
    return y * (1.0 + scale) + shift


def _inproj_kernel(x_ref, mod_ref, g_ref, w_ref, o_ref, h_ref, *, rows):
    @pl.when(pl.program_id(1) == 0)
    def _():
        g = g_ref[...]
        scale = mod_ref[0, 1:2, :]
        shift = mod_ref[0, 0:1, :]

        def body(r, carry):
            r0 = pl.multiple_of(r * rows, rows)
            h = _norm_mod(x_ref[pl.ds(r0, rows), :], g, scale, shift)
            h_ref[pl.ds(r0, rows), :] = h.astype(BF16)
            return carry

        lax.fori_loop(0, x_ref.shape[0] // rows, body, 0)

    o_ref[...] = jnp.dot(h_ref[...], w_ref[...], preferred_element_type=F32).astype(o_ref.dtype)


def _in_projection(x2, mod, g1, w_in_bf, seq):
    n, d = x2.shape
    p = w_in_bf.shape[1]
    tm, tn = min(1024, seq), 1792
    per_batch = seq // tm
    return pl.pallas_call(
        functools.partial(_inproj_kernel, rows=128),
        grid=(n // tm, p // tn),
        in_specs=[pl.BlockSpec((tm, d), lambda i, j: (i, 0)),
                  pl.BlockSpec((1, 6, d), lambda i, j: (i // per_batch, 0, 0)),
                  pl.BlockSpec((1, d), lambda i, j: (0, 0)),
                  pl.BlockSpec((d, tn), lambda i, j: (0, j))],
        out_specs=pl.BlockSpec((tm, tn), lambda i, j: (i, j)),
        out_shape=jax.ShapeDtypeStruct((n, p), BF16),
        scratch_shapes=[pltpu.VMEM((tm, d), BF16)],
        compiler_params=_tc_params(("arbitrary", "arbitrary")),
        name="inproj",
    )(x2, mod, g1.reshape(1, d), w_in_bf)


def _mixer_tables(seq, blk):
    half = HEAD_DIM // 2
    freqs = ROPE_BASE ** (-jnp.arange(half, dtype=F32) / half)
    ang = jnp.arange(seq, dtype=F32)[:, None] * freqs[None, :]
    cos = jnp.concatenate([jnp.cos(ang), jnp.cos(ang)], axis=-1)
    sin = jnp.concatenate([-jnp.sin(ang), jnp.sin(ang)], axis=-1)
    log_gamma = jnp.log1p(-jnp.exp2(-5.0 - jnp.arange(N_HEADS, dtype=F32)))
    idx = jnp.arange(blk, dtype=F32)
    dist = jnp.abs(idx[:, None] - idx[None, :])
    ck = jnp.arange(blk) // CHUNK
    visible = (ck[None, :] <= ck[:, None]).astype(F32)
    dmask = jnp.exp(log_gamma[:, None, None] * dist) * visible
    ones = jnp.ones((1, 1, HEAD_DIM), F32)
    dq = jnp.exp(log_gamma[:, None] * (idx + 1.0)[None])[:, :, None] * ones
    dkv = jnp.exp(log_gamma[:, None] * (blk - 1 - idx)[None])[:, :, None] * ones
    dblk = jnp.exp(log_gamma * blk)[:, None, None] * ones
    return cos, sin, dmask, dq, dkv, dblk


def _mixer_kernel(proj_ref, x_ref, mod_ref, cos_ref, sin_ref, dmask_ref, dq_ref, dkv_ref, dblk_ref,
                  convw_ref, wout_ref, g2_ref, wr_ref, br_ref,
                  x1_ref, h2_ref, lg_ref,
                  s_ref, zbuf_ref, y_ref, *, per_batch, n_blocks):
    blk = x_ref.shape[0]
    i = pl.program_id(0)
    cur = i % 2

    @pl.when(jnp.minimum(i, n_blocks - 1) % per_batch == 0)
    def _():
        s_ref[...] = jnp.zeros_like(s_ref)
        zbuf_ref[0:8, :] = jnp.zeros((8, D_CONV), F32)

    @pl.when(i == 0)
    def _():
        y_ref[1] = jnp.zeros(y_ref.shape[1:], BF16)

    cos = cos_ref[...]
    sin = sin_ref[...]
    k_scale = HEAD_DIM ** -0.5
    nt = (((1,), (1,)), ((), ()))
    tn = (((0,), (0,)), ((), ()))
    for h in range(N_HEADS):
        c0 = h * HEAD_DIM
        q = proj_ref[:, c0:c0 + HEAD_DIM].astype(F32)
        k = proj_ref[:, D_RET + c0:D_RET + c0 + HEAD_DIM].astype(F32)
        v = proj_ref[:, 2 * D_RET + c0:2 * D_RET + c0 + HEAD_DIM]
        g = proj_ref[:, 3 * D_RET + c0:3 * D_RET + c0 + HEAD_DIM].astype(F32)
        qr = q * cos + pltpu.roll(q, HEAD_DIM // 2, 1) * sin
        kr = (k * cos + pltpu.roll(k, HEAD_DIM // 2, 1) * sin) * k_scale
        qb = qr.astype(BF16)
        kb = kr.astype(BF16)
        scores = lax.dot_general(qb, kb, nt, preferred_element_type=F32) * dmask_ref[h]
        intra = jnp.dot(scores.astype(BF16), v, preferred_element_type=F32)
        state = s_ref[h]
        cross = jnp.dot(qb, state.astype(BF16), preferred_element_type=F32) * dq_ref[h]
        kd = (kr * dkv_ref[h]).astype(BF16)
        kv = lax.dot_general(kd, v, tn, preferred_element_type=F32)
        s_ref[h] = dblk_ref[h] * state + kv
        o = intra + cross
        mu = jnp.mean(o, axis=-1, keepdims=True)
        dev = o - mu
        var = jnp.mean(dev * dev, axis=-1, keepdims=True)
        on = dev * lax.rsqrt(var + EPS)
        y_ref[cur, :, c0:c0 + HEAD_DIM] = (g * jax.nn.sigmoid(g) * on).astype(BF16)

    cw = 256
    base = 4 * D_RET
    for cb in range(D_CONV // cw):
        lo = cb * cw
        cg = proj_ref[:, base + D_CONV + lo:base + D_CONV + lo + cw].astype(F32)
        u = proj_ref[:, base + 2 * D_CONV + lo:base + 2 * D_CONV + lo + cw].astype(F32)
        zbuf_ref[8:blk + 8, lo:lo + cw] = cg * u
    for cb in range(D_CONV // cw):
        lo = cb * cw
        z0 = zbuf_ref[8:blk + 8, lo:lo + cw]
        z1 = zbuf_ref[7:blk + 7, lo:lo + cw]
        z2 = zbuf_ref[6:blk + 6, lo:lo + cw]
        z = (convw_ref[2:3, lo:lo + cw] * z0 + convw_ref[1:2, lo:lo + cw] * z1
             + convw_ref[0:1, lo:lo + cw] * z2)
        bg = proj_ref[:, base + lo:base + lo + cw].astype(F32)
        y_ref[cur, :, D_RET + lo:D_RET + lo + cw] = (bg * z).astype(BF16)
    zbuf_ref[0:8, :] = zbuf_ref[blk:blk + 8, :]

    mix = jnp.dot(y_ref[1 - cur], wout_ref[...], preferred_element_type=F32)
    x1 = x_ref[...] + mod_ref[0, 2:3, :] * mix
    x1_ref[...] = x1
    h2 = _norm_mod(x1, g2_ref[...], mod_ref[0, 4:5, :], mod_ref[0, 3:4, :])
    half = h2.shape[1] // 2
    h2_ref[...] = _pack_bf16_pair(h2[:, :half], h2[:, half:])
    w2 = wr_ref[...]
    w_hi = w2.astype(BF16)
    w_lo = (w2 - w_hi.astype(F32)).astype(BF16)
    w_split = jnp.where(lax.broadcasted_iota(jnp.int32, w2.shape, 1) < N_EXPERTS, w_hi, w_lo)
    h_hi = h2.astype(BF16)
    h_lo = (h2 - h_hi.astype(F32)).astype(BF16)
    r = (jnp.dot(h_hi, w_split, preferred_element_type=F32)
         + jnp.dot(h_lo, w_split, preferred_element_type=F32))
    lg_ref[...] = r[:, 0:N_EXPERTS] + r[:, N_EXPERTS:2 * N_EXPERTS] + br_ref[...]


def _mixer(proj, x2, mod, conv_w, w_out_bf, g2, w_router, b_router, seq):
    n, d = x2.shape
    blk = MIX_BLOCK
    per_batch = seq // blk
    cos, sin, dmask, dq, dkv, dblk = _mixer_tables(seq, blk)
    const2 = lambda i: (0, 0)
    const3 = lambda i: (0, 0, 0)
    n_blocks = n // blk
    mixed = lambda i: jnp.minimum(i, n_blocks - 1)
    done = lambda i: jnp.maximum(i - 1, 0)
    return pl.pallas_call(
        functools.partial(_mixer_kernel, per_batch=per_batch, n_blocks=n_blocks),
        grid=(n_blocks + 1,),
        in_specs=[pl.BlockSpec((blk, proj.shape[1]), lambda i: (mixed(i), 0)),
                  pl.BlockSpec((blk, d), lambda i: (done(i), 0)),
                  pl.BlockSpec((1, 6, d), lambda i: (done(i) // per_batch, 0, 0)),
                  pl.BlockSpec((blk, HEAD_DIM), lambda i: (mixed(i) % per_batch, 0)),
                  pl.BlockSpec((blk, HEAD_DIM), lambda i: (mixed(i) % per_batch, 0)),
                  pl.BlockSpec((N_HEADS, blk, blk), const3),
                  pl.BlockSpec((N_HEADS, blk, HEAD_DIM), const3),
                  pl.BlockSpec((N_HEADS, blk, HEAD_DIM), const3),
                  pl.BlockSpec((N_HEADS, 1, HEAD_DIM), const3),
                  pl.BlockSpec((CONV_WIDTH, D_CONV), const2),
                  pl.BlockSpec((D_RET + D_CONV, d), const2),
                  pl.BlockSpec((1, d), const2),
                  pl.BlockSpec((d, 2 * N_EXPERTS), const2),
                  pl.BlockSpec((1, N_EXPERTS), const2)],
        out_specs=[pl.BlockSpec((blk, d), lambda i: (done(i), 0)),
                   pl.BlockSpec((blk, d // 2), lambda i: (done(i), 0)),
                   pl.BlockSpec((blk, N_EXPERTS), lambda i: (done(i), 0))],
        out_shape=[jax.ShapeDtypeStruct((n, d), F32),
                   jax.ShapeDtypeStruct((n, d // 2), jnp.uint32),
                   jax.ShapeDtypeStruct((n, N_EXPERTS), F32)],
        scratch_shapes=[pltpu.VMEM((N_HEADS, HEAD_DIM, HEAD_DIM), F32),
                        pltpu.VMEM((blk + 8, D_CONV), F32),
                        pltpu.VMEM((2, blk, D_RET + D_CONV), BF16)],
        compiler_params=_tc_params(("arbitrary",)),
        name="mixer",
    )(proj, x2, mod, cos, sin, dmask, dq, dkv, dblk, conv_w, w_out_bf, g2.reshape(1, d),
      jnp.concatenate([w_router, w_router], axis=1), b_router.reshape(1, N_EXPERTS))


def _top_k_block(lg_ref, tri_ref, run_ref):
    n_e, bw = lg_ref.shape

    @pl.when(pl.program_id(0) == 0)
    def _():
        run_ref[...] = jnp.zeros_like(run_ref)

    l = lg_ref[...]
    eio = lax.broadcasted_iota(jnp.int32, (n_e, bw), 0)
    vals, hots = [], []
    for _ in range(TOP_K):
        m = jnp.max(l, axis=0, keepdims=True)
        idx = jnp.min(jnp.where(l == m, eio, n_e), axis=0, keepdims=True)
        hot = eio == idx
        vals.append(m)
        hots.append(hot)
        l = jnp.where(hot, -jnp.inf, l)
    sel = sum(jnp.where(hot, 1.0, 0.0) for hot in hots)
    incl = jnp.dot(sel.astype(BF16), tri_ref[...], preferred_element_type=F32)
    run = run_ref[:, 0:1]
    rank = incl - sel + run
    run_ref[...] = jnp.broadcast_to(run + incl[:, bw - 1:bw], run_ref.shape)
    return vals, hots, rank


def _count_kernel(lg_ref, tri_ref, cnt_ref, run_ref):
    _top_k_block(lg_ref, tri_ref, run_ref)
    cnt_ref[...] = run_ref[...]


def _route_kernel(lg_ref, tri_ref, low_ref, tot_ref, pos_ref, w_ref, run_ref, *, max_slots):
    vals, hots, rank = _top_k_block(lg_ref, tri_ref, run_ref)
    ex = [jnp.exp(v - vals[0]) for v in vals]
    den = ex[0] + ex[1] + ex[2] + ex[3]
    for k in range(TOP_K):
        w_ref[k:k + 1, :] = ex[k] / den
    tot = tot_ref[...]
    n_slots = sum(jnp.where(tot > float(m * SLOT_ROWS), 1.0, 0.0) for m in range(max_slots))
    start = jnp.dot(low_ref[...], n_slots.astype(BF16), preferred_element_type=F32)
    dest = start[:, 0:1] * float(SLOT_ROWS) + rank
    for k in range(TOP_K):
        pos_ref[k:k + 1, :] = jnp.sum(jnp.where(hots[k], dest, 0.0), axis=0,
                                      keepdims=True).astype(jnp.int32)


def _route(logits_t):
    n_e, n = logits_t.shape
    bw = 512
    max_slots = -(-n // SLOT_ROWS)
    tri = (jnp.arange(bw)[:, None] <= jnp.arange(bw)[None, :]).astype(BF16)
    low = (jnp.arange(n_e)[None, :] < jnp.arange(n_e)[:, None]).astype(BF16)
    lg_spec = pl.BlockSpec((n_e, bw), lambda c: (0, c))
    tri_spec = pl.BlockSpec((bw, bw), lambda c: (0, 0))
    cnt_spec = pl.BlockSpec((n_e, 128), lambda c: (0, 0))
    run = pltpu.VMEM((n_e, 128), F32)
    counts = pl.pallas_call(
        _count_kernel,
        grid=(n // bw,),
        in_specs=[lg_spec, tri_spec],
        out_specs=cnt_spec,
        out_shape=jax.ShapeDtypeStruct((n_e, 128), F32),
        scratch_shapes=[run],
        compiler_params=_tc_params(("arbitrary",)),
        name="route_count",
    )(logits_t, tri)
    pos, w_top = pl.pallas_call(
        functools.partial(_route_kernel, max_slots=max_slots),
        grid=(n // bw,),
        in_specs=[lg_spec, tri_spec, pl.BlockSpec((n_e, n_e), lambda c: (0, 0)), cnt_spec],
        out_specs=[pl.BlockSpec((TOP_K, bw), lambda c: (0, c)),
                   pl.BlockSpec((TOP_K, bw), lambda c: (0, c))],
        out_shape=[jax.ShapeDtypeStruct((TOP_K, n), jnp.int32),
                   jax.ShapeDtypeStruct((TOP_K, n), F32)],
        scratch_shapes=[run],
        compiler_params=_tc_params(("arbitrary",)),
        name="route",
    )(logits_t, tri, low, counts)
    return pos, w_top, counts


def _sc_mesh():
    return plsc.VectorSubcoreMesh(core_axis_name="c", subcore_axis_name="s")


def _sc_worker_id():
    return lax.axis_index("s") * lax.axis_size("c") + lax.axis_index("c")


def _dispatch_rows(h2, pos):
    n, d = h2.shape
    ch = SC_ROWS_PER_COPY
    per_w = n // SC_WORKERS
    n_ch = per_w // ch
    total_rows = _max_slots(n) * SLOT_ROWS
    idx = pos.reshape(TOP_K, SC_WORKERS, n_ch, ch).transpose(1, 2, 0, 3).reshape(
        SC_WORKERS, n_ch * TOP_K, ch)

    @functools.partial(
        pl.kernel, mesh=_sc_mesh(),
        out_type=jax.ShapeDtypeStruct((total_rows, d), h2.dtype),
        scratch_types=[pltpu.VMEM((n_ch * TOP_K, ch), jnp.int32),
                       pltpu.VMEM((ch, d), h2.dtype), pltpu.VMEM((ch, d), h2.dtype),
                       pltpu.SemaphoreType.DMA, pltpu.SemaphoreType.DMA,
                       pltpu.SemaphoreType.DMA, pltpu.SemaphoreType.DMA])
    def scatter(x_hbm, idx_hbm, o_hbm, idx_v, rows0, rows1, rsem0, rsem1, ssem0, ssem1):
        wid = _sc_worker_id()
        base = wid * per_w
        pltpu.sync_copy(idx_hbm.at[wid], idx_v)
        rows, rsem, ssem = (rows0, rows1), (rsem0, rsem1), (ssem0, ssem1)

        def read(c, b):
            r0 = pl.multiple_of(base + c * ch, ch)
            return pltpu.make_async_copy(x_hbm.at[pl.ds(r0, ch)], rows[b], rsem[b])

        def send(c, k, b):
            return pltpu.make_async_copy(rows[b], o_hbm.at[idx_v.at[c * TOP_K + k]], ssem[b])

        read(0, 0).start()

        @pl.loop(0, n_ch, step=2)
        def _(c):
            read(c, 0).wait()

            @pl.when(c > 0)
            def _():
                for k in range(TOP_K):
                    send(c - 1, k, 1).wait()

            read(c + 1, 1).start()
            for k in range(TOP_K):
                send(c, k, 0).start()
            read(c + 1, 1).wait()
            for k in range(TOP_K):
                send(c, k, 0).wait()

            @pl.when(c + 2 < n_ch)
            def _():
                read(c + 2, 0).start()

            for k in range(TOP_K):
                send(c + 1, k, 1).start()

        for k in range(TOP_K):
            send(n_ch - 1, k, 1).wait()

    return scatter(h2, idx)


def _combine_rows(ys, pos_flat):
    n = pos_flat.shape[0]
    d = ys.shape[1]
    ch = SC_ROWS_PER_COPY
    per_w = n // SC_WORKERS
    n_ch = per_w // ch
    idx = pos_flat.reshape(SC_WORKERS, n_ch, ch)

    @functools.partial(
        pl.kernel, mesh=_sc_mesh(),
        out_type=jax.ShapeDtypeStruct((n, d), ys.dtype),
        scratch_types=[pltpu.VMEM((n_ch, ch), jnp.int32),
                       pltpu.VMEM((ch, d), ys.dtype), pltpu.VMEM((ch, d), ys.dtype),
                       pltpu.SemaphoreType.DMA, pltpu.SemaphoreType.DMA,
                       pltpu.SemaphoreType.DMA, pltpu.SemaphoreType.DMA])
    def gather(y_hbm, idx_hbm, o_hbm, idx_v, rows0, rows1, gsem0, gsem1, wsem0, wsem1):
        wid = _sc_worker_id()
        base = wid * per_w
        pltpu.sync_copy(idx_hbm.at[wid], idx_v)
        rows, gsem, wsem = (rows0, rows1), (gsem0, gsem1), (wsem0, wsem1)

        def fetch(c, b):
            return pltpu.make_async_copy(y_hbm.at[idx_v.at[c]], rows[b], gsem[b])

        def write(c, b):
            r0 = pl.multiple_of(base + c * ch, ch)
            return pltpu.make_async_copy(rows[b], o_hbm.at[pl.ds(r0, ch)], wsem[b])

        fetch(0, 0).start()

        @pl.loop(0, n_ch, step=2)
        def _(c):
            fetch(c, 0).wait()

            @pl.when(c > 0)
            def _():
                write(c - 1, 1).wait()

            fetch(c + 1, 1).start()
            write(c, 0).start()
            fetch(c + 1, 1).wait()
            write(c, 0).wait()

            @pl.when(c + 2 < n_ch)
            def _():
                fetch(c + 2, 0).start()

            write(c + 1, 1).start()

        write(n_ch - 1, 1).wait()

    return gather(ys, idx)


def _max_slots(n_tokens):
    return N_EXPERTS + (n_tokens * TOP_K) // SLOT_ROWS


def _expert_kernel(se_ref, sr_ref,
                   xs_ref, wgu_hbm, wd_hbm, bgu_ref, bd_ref,
                   ys_ref,
                   xb_ref, act_ref, wbuf_ref, sem_ref, *, n_a, n_b):
    s = pl.program_id(0)
    rows = sr_ref[s]
    n_blk = (rows + ROW_BLOCK - 1) // ROW_BLOCK
    tw = wbuf_ref.shape[2] // 2
    d_ff = act_ref.shape[1]
    half_d = ys_ref.shape[1]

    def panel_copy(src, buf, p):
        return pltpu.make_async_copy(src, wbuf_ref.at[buf, :, pl.ds(p * tw, tw)], sem_ref.at[buf, p])

    def gate_up_panels(e, t, buf):
        c0 = pl.multiple_of(t * tw, tw)
        return (panel_copy(wgu_hbm.at[e, :, pl.ds(c0, tw)], buf, 0),
                panel_copy(wgu_hbm.at[e, :, pl.ds(d_ff + c0, tw)], buf, 1))

    def down_panels(e, u, buf):
        c0 = pl.multiple_of(u * tw, tw)
        return (panel_copy(wd_hbm.at[e, :, pl.ds(c0, tw)], buf, 0),
                panel_copy(wd_hbm.at[e, :, pl.ds(half_d + c0, tw)], buf, 1))

    def start(panels):
        for cp in panels:
            cp.start()

    def wait(panels):
        for cp in panels:
            cp.wait()

    expert = se_ref[s]

    def for_row_blocks(fn):
        n_big = n_blk // 8

        def body(rb, carry):
            fn(pl.multiple_of(rb * 8 * ROW_BLOCK, 8 * ROW_BLOCK), 8 * ROW_BLOCK)
            return carry

        lax.fori_loop(0, n_big, body, 0)
        for size in (4, 2, 1):
            @pl.when((n_blk // size) % 2 == 1)
            def _(size=size):
                first = (n_blk // (2 * size)) * 2 * size * ROW_BLOCK
                fn(pl.multiple_of(first, 2 * size * ROW_BLOCK), size * ROW_BLOCK)

    @pl.when(s == 0)
    def _():
        for k in range(RING_AHEAD):
            start(gate_up_panels(expert, k, k))

    def unpack(rb, carry):
        r0 = pl.multiple_of(rb * ROW_BLOCK, ROW_BLOCK)
        keep = (r0 + lax.broadcasted_iota(jnp.int32, (ROW_BLOCK, 1), 0)) < rows
        lo, hi = _unpack_bf16_pair(xs_ref[pl.ds(r0, ROW_BLOCK), :])
        xb_ref[pl.ds(r0, ROW_BLOCK), 0:half_d] = jnp.where(keep, lo, 0.0).astype(BF16)
        xb_ref[pl.ds(r0, ROW_BLOCK), half_d:2 * half_d] = jnp.where(keep, hi, 0.0).astype(BF16)
        return carry

    lax.fori_loop(0, n_blk, unpack, 0)

    ring = RING_AHEAD + 1

    def gate_up_tile(t, carry):
        buf = t % ring
        wait(gate_up_panels(expert, t, buf))
        nxt = t + RING_AHEAD

        @pl.when(nxt < n_a)
        def _():
            start(gate_up_panels(expert, nxt, nxt % ring))

        @pl.when(nxt >= n_a)
        def _():
            start(down_panels(expert, nxt - n_a, nxt % ring))

        col = pl.multiple_of(t * tw, tw)
        b_gate = bgu_ref[0, :, pl.ds(col, tw)]
        b_up = bgu_ref[0, :, pl.ds(d_ff + col, tw)]

        def gate_up(r0, m):
            gu = jnp.dot(xb_ref[pl.ds(r0, m), :], wbuf_ref[buf].astype(BF16), preferred_element_type=F32)
            gate = jnp.minimum(gu[:, 0:tw] + b_gate, SWIGLU_LIMIT)
            up = jnp.clip(gu[:, tw:2 * tw] + b_up, -SWIGLU_LIMIT, SWIGLU_LIMIT)
            act = (up + 1.0) * (gate * jax.nn.sigmoid(SWIGLU_ALPHA * gate))
            act_ref[pl.ds(r0, m), pl.ds(col, tw)] = act.astype(BF16)

        for_row_blocks(gate_up)
        return carry

    lax.fori_loop(0, n_a, gate_up_tile, 0)

    def down_tile(u, carry):
        buf = (n_a + u) % ring
        wait(down_panels(expert, u, buf))
        nxt = u + RING_AHEAD

        @pl.when(s + 1 < pl.num_programs(0))
        def _():
            start(gate_up_panels(se_ref[s + 1], nxt - n_b, (nxt - n_b) % ring))

        col = pl.multiple_of(u * tw, tw)
        b_lo = bd_ref[0, :, pl.ds(col, tw)]
        b_hi = bd_ref[0, :, pl.ds(half_d + col, tw)]

        def down(r0, m):
            y = jnp.dot(act_ref[pl.ds(r0, m), :], wbuf_ref[buf].astype(BF16), preferred_element_type=F32)
            ys_ref[pl.ds(r0, m), pl.ds(col, tw)] = _pack_bf16_pair(y[:, 0:tw] + b_lo, y[:, tw:2 * tw] + b_hi)

        for_row_blocks(down)
        return carry

    lax.fori_loop(0, n_b, down_tile, 0)

    def fill(rb, carry):
        r0 = pl.multiple_of(rb * ROW_BLOCK, ROW_BLOCK)
        ys_ref[pl.ds(r0, ROW_BLOCK), :] = jnp.zeros((ROW_BLOCK, half_d), jnp.uint32)
        return carry

    lax.fori_loop(n_blk, SLOT_ROWS // ROW_BLOCK, fill, 0)


def _experts(xs, slot_expert, slot_rows, n_used, w_gate_up, b_gate_up, w_down, b_down):
    n_e, d, two_ff = w_gate_up.shape
    d_ff = two_ff // 2
    assert d == d_ff, "the weight ring holds (rows, W_PANEL) panels of both projections"
    tw = W_PANEL
    n_a, n_b = d_ff // tw, (d // 2) // tw
    ring = RING_AHEAD + 1
    assert (n_a + n_b) % ring == 0, "ring slot of tile j must repeat from one expert slot to the next"
    assert n_b == RING_AHEAD <= n_a, "the look-ahead from a down tile must land in the next slot's gate/up tiles"
    total_rows = xs.shape[0]

    grid_spec = pltpu.PrefetchScalarGridSpec(
        num_scalar_prefetch=2,
        grid=(n_used,),
        in_specs=[
            pl.BlockSpec((SLOT_ROWS, d // 2), lambda s, se, sr: (s, 0)),
            pl.BlockSpec(memory_space=pl.ANY),
            pl.BlockSpec(memory_space=pl.ANY),
            pl.BlockSpec((1, 1, two_ff), lambda s, se, sr: (se[s], 0, 0)),
            pl.BlockSpec((1, 1, d), lambda s, se, sr: (se[s], 0, 0)),
        ],
        out_specs=pl.BlockSpec((SLOT_ROWS, d // 2), lambda s, se, sr: (s, 0)),
        scratch_shapes=[pltpu.VMEM((SLOT_ROWS, d), BF16),
                        pltpu.VMEM((SLOT_ROWS, d_ff), BF16),
                        pltpu.VMEM((ring, d, 2 * tw), F32),
                        pltpu.SemaphoreType.DMA((ring, 2))],
    )
    return pl.pallas_call(
        functools.partial(_expert_kernel, n_a=n_a, n_b=n_b),
        grid_spec=grid_spec,
        out_shape=jax.ShapeDtypeStruct((total_rows, d // 2), jnp.uint32),
        compiler_params=_tc_params(("arbitrary",)),
        name="experts",
    )(slot_expert, slot_rows, xs, w_gate_up, w_down,
      b_gate_up.reshape(n_e, 1, two_ff), b_down.reshape(n_e, 1, d))


def _slot_tables(counts, n_tokens):
    max_slots = _max_slots(n_tokens)
    n_slots = (counts + SLOT_ROWS - 1) // SLOT_ROWS
    slot_end = jnp.cumsum(n_slots)
    slot_start = slot_end - n_slots
    n_used = slot_end[-1]
    sid = jnp.arange(max_slots, dtype=jnp.int32)
    expert = jnp.minimum(jnp.sum(sid[:, None] >= slot_end[None, :], axis=1), N_EXPERTS - 1).astype(jnp.int32)
    local = sid - slot_start[expert]
    rows = jnp.clip(counts[expert] - local * SLOT_ROWS, 0, SLOT_ROWS)
    rows = jnp.where(sid < n_used, rows, 0).astype(jnp.int32)
    return expert, rows, n_used.reshape(1).astype(jnp.int32)


def _final_kernel(x1_ref, y0_ref, y1_ref, y2_ref, y3_ref, w_ref, mod_ref, g_ref, o_ref):
    half = y0_ref.shape[1]
    d = 2 * half
    moe_lo = moe_hi = None
    for k, y_ref in enumerate((y0_ref, y1_ref, y2_ref, y3_ref)):
        lo, hi = _unpack_bf16_pair(y_ref[...])
        wk = w_ref[:, k:k + 1]
        moe_lo = wk * lo if k == 0 else moe_lo + wk * lo
        moe_hi = wk * hi if k == 0 else moe_hi + wk * hi
    x_lo = x1_ref[:, 0:half] + mod_ref[0, 5:6, 0:half] * moe_lo
    x_hi = x1_ref[:, half:d] + mod_ref[0, 5:6, half:d] * moe_hi
    ss = jnp.sum(x_lo * x_lo, axis=-1, keepdims=True) + jnp.sum(x_hi * x_hi, axis=-1, keepdims=True)
    r = lax.rsqrt(ss / d + EPS)
    o_ref[:, 0:half] = x_lo * r * g_ref[:, 0:half]
    o_ref[:, half:d] = x_hi * r * g_ref[:, half:d]


def _final(x1, y4, w_tok, mod, final_g, seq):
    n, d = x1.shape
    tm = 256
    per_batch = seq // tm
    n_i = n // tm
    y_specs = [pl.BlockSpec((tm, d // 2), functools.partial(lambda i, k: (k * n_i + i, 0), k=k))
               for k in range(TOP_K)]
    return pl.pallas_call(
        _final_kernel,
        grid=(n_i,),
        in_specs=[pl.BlockSpec((tm, d), lambda i: (i, 0))] + y_specs + [
                  pl.BlockSpec((tm, TOP_K), lambda i: (i, 0)),
                  pl.BlockSpec((1, 6, d), lambda i: (i // per_batch, 0, 0)),
                  pl.BlockSpec((1, d), lambda i: (0, 0))],
        out_specs=pl.BlockSpec((tm, d), lambda i: (i, 0)),
        out_shape=jax.ShapeDtypeStruct((n, d), F32),
        compiler_params=_tc_params(("arbitrary",)),
        name="final",
    )(x1, y4, y4, y4, y4, w_tok, mod, final_g.reshape(1, d))


def kernel(x, c, norm1_g, w_mod, b_mod, w_in, conv_w, w_out, norm2_g, w_router, b_router,
           w_gate_up, b_gate_up, w_down, b_down, final_g):
    b, seq, d = x.shape
    n = b * seq
    x2 = x.reshape(n, d)
    mod = _modulation(c, w_mod[0], b_mod[0])
    proj = _in_projection(x2, mod, norm1_g[0], w_in[0].astype(BF16), seq)
    x1, h2, logits = _mixer(proj, x2, mod, conv_w[0], w_out[0].astype(BF16), norm2_g[0],
                            w_router[0], b_router[0], seq)
    pos, w_top, counts = _route(logits.T)
    slot_expert, slot_rows, n_used = _slot_tables(counts[:, 0].astype(jnp.int32), n)
    xs = _dispatch_rows(h2, pos)
    ys = _experts(xs, slot_expert, slot_rows, n_used[0], w_gate_up[0], b_gate_up[0], w_down[0], b_down[0])
    y4 = _combine_rows(ys, pos.reshape(TOP_K * n))
    out = _final(x1, y4, w_top.T, mod, final_g, seq)
    return out.reshape(b, seq, d)
```

```python
import functools

import jax
import jax.numpy as jnp
import numpy as np
from jax import lax
from jax.experimental import pallas as pl
from jax.experimental.pallas import tpu as pltpu
from jax.experimental.pallas import tpu_sc as plsc

F32 = jnp.float32
BF16 = jnp.bfloat16

CHUNK = 64
N_HEADS = 8
HEAD_DIM = 128
D_RET = N_HEADS * HEAD_DIM
D_CONV = 1024
CONV_WIDTH = 3
N_EXPERTS = 32
TOP_K = 4
SWIGLU_LIMIT = 7.0
SWIGLU_ALPHA = 1.702
ROPE_BASE = 10000.0
EPS = 1e-6

VMEM_LIMIT_BYTES = 58 * 1024 * 1024
SC_WORKERS = 32
SC_ROWS_PER_COPY = 32

MIX_BLOCK = 256
ROW_BLOCK = 128
SLOT_ROWS = 9 * ROW_BLOCK
W_PANEL = 512
RING_AHEAD = 2


def _tc_params(sem):
    return pltpu.CompilerParams(dimension_semantics=sem, vmem_limit_bytes=VMEM_LIMIT_BYTES)


def _bf16_bits(x):
    return lax.bitcast_convert_type(x.astype(BF16).astype(F32), jnp.uint32)


def _pack_bf16_pair(lo, hi):
    return (_bf16_bits(lo) >> 16) | _bf16_bits(hi)


def _unpack_bf16_pair(p):
    lo = lax.bitcast_convert_type(p << 16, F32)
    hi = lax.bitcast_convert_type(p & jnp.uint32(0xFFFF0000), F32)
    return lo, hi


def _mod_kernel(c_ref, w_ref, b_ref, o_ref):
    c = c_ref[...]
    ca = (c * jax.nn.sigmoid(c)).astype(BF16)
    o_ref[...] = jnp.dot(ca, w_ref[...].astype(BF16), preferred_element_type=F32) + b_ref[...]


def _modulation(c, w_mod, b_mod):
    b, d = c.shape
    n = w_mod.shape[1]
    tn = 1024
    c8 = jnp.zeros((8, d), F32).at[:b].set(c)
    out = pl.pallas_call(
        _mod_kernel,
        grid=(n // tn,),
        in_specs=[pl.BlockSpec((8, d), lambda j: (0, 0)),
                  pl.BlockSpec((d, tn), lambda j: (0, j)),
                  pl.BlockSpec((1, tn), lambda j: (0, j))],
        out_specs=pl.BlockSpec((8, tn), lambda j: (0, j)),
        out_shape=jax.ShapeDtypeStruct((8, n), F32),
        compiler_params=_tc_params(("arbitrary",)),
        name="mod",
    )(c8, w_mod, b_mod.reshape(1, n))
    return out[:b].reshape(b, 6, d)


def _norm_mod(x, g, scale, shift):
    y = x * lax.rsqrt(jnp.mean(x * x, axis=-1, keepdims=True) + EPS) * g
    return y * (1.0 + scale) + shift


def _inproj_kernel(x_ref, mod_ref, g_ref, w_ref, o_ref, h_ref, *, rows):
    @pl.when(pl.program_id(1) == 0)
    def _():
        g = g_ref[...]
        scale = mod_ref[0, 1:2, :]
        shift = mod_ref[0, 0:1, :]

        def body(r, carry):
            r0 = pl.multiple_of(r * rows, rows)
            h = _norm_mod(x_ref[pl.ds(r0, rows), :], g, scale, shift)
            h_ref[pl.ds(r0, rows), :] = h.astype(BF16)
            return carry

        lax.fori_loop(0, x_ref.shape[0] // rows, body, 0)

    o_ref[...] = jnp.dot(h_ref[...], w_ref[...], preferred_element_type=F32).astype(o_ref.dtype)


def _in_projection(x2, mod, g1, w_in_bf, seq):
    n, d = x2.shape
    p = w_in_bf.shape[1]
    tm, tn = min(1024, seq), 1792
    per_batch = seq // tm
    return pl.pallas_call(
        functools.partial(_inproj_kernel, rows=128),
        grid=(n // tm, p // tn),
        in_specs=[pl.BlockSpec((tm, d), lambda i, j: (i, 0)),
                  pl.BlockSpec((1, 6, d), lambda i, j: (i // per_batch, 0, 0)),
                  pl.BlockSpec((1, d), lambda i, j: (0, 0)),
                  pl.BlockSpec((d, tn), lambda i, j: (0, j))],
        out_specs=pl.BlockSpec((tm, tn), lambda i, j: (i, j)),
        out_shape=jax.ShapeDtypeStruct((n, p), BF16),
        scratch_shapes=[pltpu.VMEM((tm, d), BF16)],
        compiler_params=_tc_params(("arbitrary", "arbitrary")),
        name="inproj",
    )(x2, mod, g1.reshape(1, d), w_in_bf)


def _mixer_tables(seq, blk):
    f32 = np.float32
    half = HEAD_DIM // 2
    freqs = f32(ROPE_BASE) ** (-np.arange(half, dtype=f32) / f32(half))
    ang = np.arange(seq, dtype=f32)[:, None] * freqs[None, :]
    cos = np.concatenate([np.cos(ang), np.cos(ang)], axis=-1)
    sin = np.concatenate([-np.sin(ang), np.sin(ang)], axis=-1)
    log_gamma = np.log1p(-np.exp2(f32(-5.0) - np.arange(N_HEADS, dtype=f32)))
    idx = np.arange(blk, dtype=f32)
    dist = np.abs(idx[:, None] - idx[None, :])
    ck = np.arange(blk) // CHUNK
    visible = (ck[None, :] <= ck[:, None]).astype(f32)
    dmask = np.exp(log_gamma[:, None, None] * dist) * visible
    ones = np.ones((1, 1, HEAD_DIM), f32)
    dq = np.exp(log_gamma[:, None] * (idx + f32(1.0))[None])[:, :, None] * ones
    dkv = np.exp(log_gamma[:, None] * (f32(blk - 1) - idx)[None])[:, :, None] * ones
    dblk = np.exp(log_gamma * f32(blk))[:, None, None] * ones
    return tuple(jnp.asarray(t, F32) for t in (cos, sin, dmask, dq, dkv, dblk))


def _mixer_kernel(proj_ref, x_ref, mod_ref, cos_ref, sin_ref, dmask_ref, dq_ref, dkv_ref, dblk_ref,
                  convw_ref, wout_ref, g2_ref, wr_ref, br_ref,
                  x1_ref, h2_ref, lg_ref,
                  s_ref, zbuf_ref, y_ref, *, per_batch, n_blocks):
    blk = x_ref.shape[0]
    i = pl.program_id(0)
    cur = i % 2

    @pl.when(jnp.minimum(i, n_blocks - 1) % per_batch == 0)
    def _():
        s_ref[...] = jnp.zeros_like(s_ref)
        zbuf_ref[0:8, :] = jnp.zeros((8, D_CONV), F32)

    @pl.when(i == 0)
    def _():
        y_ref[1] = jnp.zeros(y_ref.shape[1:], BF16)

    cos = cos_ref[...]
    sin = sin_ref[...]
    k_scale = HEAD_DIM ** -0.5
    nt = (((1,), (1,)), ((), ()))
    tn = (((0,), (0,)), ((), ()))
    for h in range(N_HEADS):
        c0 = h * HEAD_DIM
        q = proj_ref[:, c0:c0 + HEAD_DIM].astype(F32)
        k = proj_ref[:, D_RET + c0:D_RET + c0 + HEAD_DIM].astype(F32)
        v = proj_ref[:, 2 * D_RET + c0:2 * D_RET + c0 + HEAD_DIM]
        g = proj_ref[:, 3 * D_RET + c0:3 * D_RET + c0 + HEAD_DIM].astype(F32)
        qr = q * cos + pltpu.roll(q, HEAD_DIM // 2, 1) * sin
        kr = (k * cos + pltpu.roll(k, HEAD_DIM // 2, 1) * sin) * k_scale
        qb = qr.astype(BF16)
        kb = kr.astype(BF16)
        scores = lax.dot_general(qb, kb, nt, preferred_element_type=F32) * dmask_ref[h]
        intra = jnp.dot(scores.astype(BF16), v, preferred_element_type=F32)
        state = s_ref[h]
        cross = jnp.dot(qb, state.astype(BF16), preferred_element_type=F32) * dq_ref[h]
        kd = (kr * dkv_ref[h]).astype(BF16)
        kv = lax.dot_general(kd, v, tn, preferred_element_type=F32)
        s_ref[h] = dblk_ref[h] * state + kv
        o = intra + cross
        mu = jnp.mean(o, axis=-1, keepdims=True)
        dev = o - mu
        var = jnp.mean(dev * dev, axis=-1, keepdims=True)
        on = dev * lax.rsqrt(var + EPS)
        y_ref[cur, :, c0:c0 + HEAD_DIM] = (g * jax.nn.sigmoid(g) * on).astype(BF16)

    cw = 256
    base = 4 * D_RET
    for cb in range(D_CONV // cw):
        lo = cb * cw
        cg = proj_ref[:, base + D_CONV + lo:base + D_CONV + lo + cw].astype(F32)
        u = proj_ref[:, base + 2 * D_CONV + lo:base + 2 * D_CONV + lo + cw].astype(F32)
        zbuf_ref[8:blk + 8, lo:lo + cw] = cg * u
    for cb in range(D_CONV // cw):
        lo = cb * cw
        z0 = zbuf_ref[8:blk + 8, lo:lo + cw]
        z1 = zbuf_ref[7:blk + 7, lo:lo + cw]
        z2 = zbuf_ref[6:blk + 6, lo:lo + cw]
        z = (convw_ref[2:3, lo:lo + cw] * z0 + convw_ref[1:2, lo:lo + cw] * z1
             + convw_ref[0:1, lo:lo + cw] * z2)
        bg = proj_ref[:, base + lo:base + lo + cw].astype(F32)
        y_ref[cur, :, D_RET + lo:D_RET + lo + cw] = (bg * z).astype(BF16)
    zbuf_ref[0:8, :] = zbuf_ref[blk:blk + 8, :]

    mix = jnp.dot(y_ref[1 - cur], wout_ref[...], preferred_element_type=F32)
    x1 = x_ref[...] + mod_ref[0, 2:3, :] * mix
    x1_ref[...] = x1
    h2 = _norm_mod(x1, g2_ref[...], mod_ref[0, 4:5, :], mod_ref[0, 3:4, :])
    half = h2.shape[1] // 2
    h2_ref[...] = _pack_bf16_pair(h2[:, :half], h2[:, half:])
    w2 = wr_ref[...]
    w_hi = w2.astype(BF16)
    w_lo = (w2 - w_hi.astype(F32)).astype(BF16)
    w_split = jnp.where(lax.broadcasted_iota(jnp.int32, w2.shape, 1) < N_EXPERTS, w_hi, w_lo)
    h_hi = h2.astype(BF16)
    h_lo = (h2 - h_hi.astype(F32)).astype(BF16)
    r = (jnp.dot(h_hi, w_split, preferred_element_type=F32)
         + jnp.dot(h_lo, w_split, preferred_element_type=F32))
    lg_ref[...] = r[:, 0:N_EXPERTS] + r[:, N_EXPERTS:2 * N_EXPERTS] + br_ref[...]


def _mixer(proj, x2, mod, conv_w, w_out_bf, g2, w_router, b_router, seq):
    n, d = x2.shape
    blk = MIX_BLOCK
    per_batch = seq // blk
    cos, sin, dmask, dq, dkv, dblk = _mixer_tables(seq, blk)
    const2 = lambda i: (0, 0)
    const3 = lambda i: (0, 0, 0)
    n_blocks = n // blk
    mixed = lambda i: jnp.minimum(i, n_blocks - 1)
    done = lambda i: jnp.maximum(i - 1, 0)
    return pl.pallas_call(
        functools.partial(_mixer_kernel, per_batch=per_batch, n_blocks=n_blocks),
        grid=(n_blocks + 1,),
        in_specs=[pl.BlockSpec((blk, proj.shape[1]), lambda i: (mixed(i), 0)),
                  pl.BlockSpec((blk, d), lambda i: (done(i), 0)),
                  pl.BlockSpec((1, 6, d), lambda i: (done(i) // per_batch, 0, 0)),
                  pl.BlockSpec((blk, HEAD_DIM), lambda i: (mixed(i) % per_batch, 0)),
                  pl.BlockSpec((blk, HEAD_DIM), lambda i: (mixed(i) % per_batch, 0)),
                  pl.BlockSpec((N_HEADS, blk, blk), const3),
                  pl.BlockSpec((N_HEADS, blk, HEAD_DIM), const3),
                  pl.BlockSpec((N_HEADS, blk, HEAD_DIM), const3),
                  pl.BlockSpec((N_HEADS, 1, HEAD_DIM), const3),
                  pl.BlockSpec((CONV_WIDTH, D_CONV), const2),
                  pl.BlockSpec((D_RET + D_CONV, d), const2),
                  pl.BlockSpec((1, d), const2),
                  pl.BlockSpec((d, 2 * N_EXPERTS), const2),
                  pl.BlockSpec((1, N_EXPERTS), const2)],
        out_specs=[pl.BlockSpec((blk, d), lambda i: (done(i), 0)),
                   pl.BlockSpec((blk, d // 2), lambda i: (done(i), 0)),
                   pl.BlockSpec((blk, N_EXPERTS), lambda i: (done(i), 0))],
        out_shape=[jax.ShapeDtypeStruct((n, d), F32),
                   jax.ShapeDtypeStruct((n, d // 2), jnp.uint32),
                   jax.ShapeDtypeStruct((n, N_EXPERTS), F32)],
        scratch_shapes=[pltpu.VMEM((N_HEADS, HEAD_DIM, HEAD_DIM), F32),
                        pltpu.VMEM((blk + 8, D_CONV), F32),
                        pltpu.VMEM((2, blk, D_RET + D_CONV), BF16)],
        compiler_params=_tc_params(("arbitrary",)),
        name="mixer",
    )(proj, x2, mod, cos, sin, dmask, dq, dkv, dblk, conv_w, w_out_bf, g2.reshape(1, d),
      jnp.concatenate([w_router, w_router], axis=1), b_router.reshape(1, N_EXPERTS))


def _top_k_block(lg_ref, tri_ref, run_ref):
    n_e, bw = lg_ref.shape

    @pl.when(pl.program_id(0) == 0)
    def _():
        run_ref[...] = jnp.zeros_like(run_ref)

    l = lg_ref[...]
    eio = lax.broadcasted_iota(jnp.int32, (n_e, bw), 0)
    vals, hots = [], []
    for _ in range(TOP_K):
        m = jnp.max(l, axis=0, keepdims=True)
        idx = jnp.min(jnp.where(l == m, eio, n_e), axis=0, keepdims=True)
        hot = eio == idx
        vals.append(m)
        hots.append(hot)
        l = jnp.where(hot, -jnp.inf, l)
    sel = sum(jnp.where(hot, 1.0, 0.0) for hot in hots)
    incl = jnp.dot(sel.astype(BF16), tri_ref[...], preferred_element_type=F32)
    run = run_ref[:, 0:1]
    rank = incl - sel + run
    run_ref[...] = jnp.broadcast_to(run + incl[:, bw - 1:bw], run_ref.shape)
    return vals, hots, rank


def _count_kernel(lg_ref, tri_ref, cnt_ref, run_ref):
    _top_k_block(lg_ref, tri_ref, run_ref)
    cnt_ref[...] = run_ref[...]


def _route_kernel(lg_ref, tri_ref, low_ref, tot_ref, pos_ref, w_ref, run_ref, *, max_slots):
    vals, hots, rank = _top_k_block(lg_ref, tri_ref, run_ref)
    ex = [jnp.exp(v - vals[0]) for v in vals]
    den = ex[0] + ex[1] + ex[2] + ex[3]
    for k in range(TOP_K):
        w_ref[k:k + 1, :] = ex[k] / den
    tot = tot_ref[...]
    n_slots = sum(jnp.where(tot > float(m * SLOT_ROWS), 1.0, 0.0) for m in range(max_slots))
    start = jnp.dot(low_ref[...], n_slots.astype(BF16), preferred_element_type=F32)
    dest = start[:, 0:1] * float(SLOT_ROWS) + rank
    for k in range(TOP_K):
        pos_ref[k:k + 1, :] = jnp.sum(jnp.where(hots[k], dest, 0.0), axis=0,
                                      keepdims=True).astype(jnp.int32)


def _route(logits_t):
    n_e, n = logits_t.shape
    bw = min(1024, n)
    max_slots = -(-n // SLOT_ROWS)
    tri = jnp.asarray(np.arange(bw)[:, None] <= np.arange(bw)[None, :], BF16)
    low = jnp.asarray(np.arange(n_e)[None, :] < np.arange(n_e)[:, None], BF16)
    lg_spec = pl.BlockSpec((n_e, bw), lambda c: (0, c))
    tri_spec = pl.BlockSpec((bw, bw), lambda c: (0, 0))
    cnt_spec = pl.BlockSpec((n_e, 128), lambda c: (0, 0))
    run = pltpu.VMEM((n_e, 128), F32)
    counts = pl.pallas_call(
        _count_kernel,
        grid=(n // bw,),
        in_specs=[lg_spec, tri_spec],
        out_specs=cnt_spec,
        out_shape=jax.ShapeDtypeStruct((n_e, 128), F32),
        scratch_shapes=[run],
        compiler_params=_tc_params(("arbitrary",)),
        name="route_count",
    )(logits_t, tri)
    pos, w_top = pl.pallas_call(
        functools.partial(_route_kernel, max_slots=max_slots),
        grid=(n // bw,),
        in_specs=[lg_spec, tri_spec, pl.BlockSpec((n_e, n_e), lambda c: (0, 0)), cnt_spec],
        out_specs=[pl.BlockSpec((TOP_K, bw), lambda c: (0, c)),
                   pl.BlockSpec((TOP_K, bw), lambda c: (0, c))],
        out_shape=[jax.ShapeDtypeStruct((TOP_K, n), jnp.int32),
                   jax.ShapeDtypeStruct((TOP_K, n), F32)],
        scratch_shapes=[run],
        compiler_params=_tc_params(("arbitrary",)),
        name="route",
    )(logits_t, tri, low, counts)
    return pos, w_top, counts


def _sc_mesh():
    return plsc.VectorSubcoreMesh(core_axis_name="c", subcore_axis_name="s")


def _sc_worker_id():
    return lax.axis_index("s") * lax.axis_size("c") + lax.axis_index("c")


def _dispatch_rows(h2, pos):
    n, d = h2.shape
    ch = SC_ROWS_PER_COPY
    per_w = n // SC_WORKERS
    n_ch = per_w // ch
    total_rows = _max_slots(n) * SLOT_ROWS
    idx = pos.reshape(TOP_K, SC_WORKERS, n_ch, ch).transpose(1, 2, 0, 3).reshape(
        SC_WORKERS, n_ch * TOP_K, ch)

    @functools.partial(
        pl.kernel, mesh=_sc_mesh(),
        out_type=jax.ShapeDtypeStruct((total_rows, d), h2.dtype),
        scratch_types=[pltpu.VMEM((n_ch * TOP_K, ch), jnp.int32),
                       pltpu.VMEM((ch, d), h2.dtype), pltpu.VMEM((ch, d), h2.dtype),
                       pltpu.SemaphoreType.DMA, pltpu.SemaphoreType.DMA,
                       pltpu.SemaphoreType.DMA, pltpu.SemaphoreType.DMA])
    def scatter(x_hbm, idx_hbm, o_hbm, idx_v, rows0, rows1, rsem0, rsem1, ssem0, ssem1):
        wid = _sc_worker_id()
        base = wid * per_w
        pltpu.sync_copy(idx_hbm.at[wid], idx_v)
        rows, rsem, ssem = (rows0, rows1), (rsem0, rsem1), (ssem0, ssem1)

        def read(c, b):
            r0 = pl.multiple_of(base + c * ch, ch)
            return pltpu.make_async_copy(x_hbm.at[pl.ds(r0, ch)], rows[b], rsem[b])

        def send(c, k, b):
            return pltpu.make_async_copy(rows[b], o_hbm.at[idx_v.at[c * TOP_K + k]], ssem[b])

        read(0, 0).start()

        @pl.loop(0, n_ch, step=2)
        def _(c):
            read(c, 0).wait()

            @pl.when(c > 0)
            def _():
                for k in range(TOP_K):
                    send(c - 1, k, 1).wait()

            read(c + 1, 1).start()
            for k in range(TOP_K):
                send(c, k, 0).start()
            read(c + 1, 1).wait()
            for k in range(TOP_K):
                send(c, k, 0).wait()

            @pl.when(c + 2 < n_ch)
            def _():
                read(c + 2, 0).start()

            for k in range(TOP_K):
                send(c + 1, k, 1).start()

        for k in range(TOP_K):
            send(n_ch - 1, k, 1).wait()

    return scatter(h2, idx)


def _combine_rows(ys, pos_flat):
    n = pos_flat.shape[0]
    d = ys.shape[1]
    ch = SC_ROWS_PER_COPY
    per_w = n // SC_WORKERS
    n_ch = per_w // ch
    idx = pos_flat.reshape(SC_WORKERS, n_ch, ch)

    @functools.partial(
        pl.kernel, mesh=_sc_mesh(),
        out_type=jax.ShapeDtypeStruct((n, d), ys.dtype),
        scratch_types=[pltpu.VMEM((n_ch, ch), jnp.int32),
                       pltpu.VMEM((ch, d), ys.dtype), pltpu.VMEM((ch, d), ys.dtype),
                       pltpu.SemaphoreType.DMA, pltpu.SemaphoreType.DMA,
                       pltpu.SemaphoreType.DMA, pltpu.SemaphoreType.DMA])
    def gather(y_hbm, idx_hbm, o_hbm, idx_v, rows0, rows1, gsem0, gsem1, wsem0, wsem1):
        wid = _sc_worker_id()
        base = wid * per_w
        pltpu.sync_copy(idx_hbm.at[wid], idx_v)
        rows, gsem, wsem = (rows0, rows1), (gsem0, gsem1), (wsem0, wsem1)

        def fetch(c, b):
            return pltpu.make_async_copy(y_hbm.at[idx_v.at[c]], rows[b], gsem[b])

        def write(c, b):
            r0 = pl.multiple_of(base + c * ch, ch)
            return pltpu.make_async_copy(rows[b], o_hbm.at[pl.ds(r0, ch)], wsem[b])

        fetch(0, 0).start()

        @pl.loop(0, n_ch, step=2)
        def _(c):
            fetch(c, 0).wait()

            @pl.when(c > 0)
            def _():
                write(c - 1, 1).wait()

            fetch(c + 1, 1).start()
            write(c, 0).start()
            fetch(c + 1, 1).wait()
            write(c, 0).wait()

            @pl.when(c + 2 < n_ch)
            def _():
                fetch(c + 2, 0).start()

            write(c + 1, 1).start()

        write(n_ch - 1, 1).wait()

    return gather(ys, idx)


def _max_slots(n_tokens):
    return N_EXPERTS + (n_tokens * TOP_K) // SLOT_ROWS


def _expert_kernel(se_ref, sr_ref,
                   xs_ref, wgu_hbm, wd_hbm, bgu_ref, bd_ref,
                   ys_ref,
                   xb_ref, act_ref, wbuf_ref, sem_ref, *, n_a, n_b):
    s = pl.program_id(0)
    rows = sr_ref[s]
    n_blk = (rows + ROW_BLOCK - 1) // ROW_BLOCK
    tw = wbuf_ref.shape[2] // 2
    d_ff = act_ref.shape[1]
    half_d = ys_ref.shape[1]

    def panel_copy(src, buf, p):
        return pltpu.make_async_copy(src, wbuf_ref.at[buf, :, pl.ds(p * tw, tw)], sem_ref.at[buf, p])

    def gate_up_panels(e, t, buf):
        c0 = pl.multiple_of(t * tw, tw)
        return (panel_copy(wgu_hbm.at[e, :, pl.ds(c0, tw)], buf, 0),
                panel_copy(wgu_hbm.at[e, :, pl.ds(d_ff + c0, tw)], buf, 1))

    def down_panels(e, u, buf):
        c0 = pl.multiple_of(u * tw, tw)
        return (panel_copy(wd_hbm.at[e, :, pl.ds(c0, tw)], buf, 0),
                panel_copy(wd_hbm.at[e, :, pl.ds(half_d + c0, tw)], buf, 1))

    def start(panels):
        for cp in panels:
            cp.start()

    def wait(panels):
        for cp in panels:
            cp.wait()

    expert = se_ref[s]

    def for_row_blocks(fn):
        n_big = n_blk // 8

        def body(rb, carry):
            fn(pl.multiple_of(rb * 8 * ROW_BLOCK, 8 * ROW_BLOCK), 8 * ROW_BLOCK)
            return carry

        lax.fori_loop(0, n_big, body, 0)
        for size in (4, 2, 1):
            @pl.when((n_blk // size) % 2 == 1)
            def _(size=size):
                first = (n_blk // (2 * size)) * 2 * size * ROW_BLOCK
                fn(pl.multiple_of(first, 2 * size * ROW_BLOCK), size * ROW_BLOCK)

    @pl.when(s == 0)
    def _():
        for k in range(RING_AHEAD):
            start(gate_up_panels(expert, k, k))

    def unpack(rb, carry):
        r0 = pl.multiple_of(rb * ROW_BLOCK, ROW_BLOCK)
        keep = (r0 + lax.broadcasted_iota(jnp.int32, (ROW_BLOCK, 1), 0)) < rows
        lo, hi = _unpack_bf16_pair(xs_ref[pl.ds(r0, ROW_BLOCK), :])
        xb_ref[pl.ds(r0, ROW_BLOCK), 0:half_d] = jnp.where(keep, lo, 0.0).astype(BF16)
        xb_ref[pl.ds(r0, ROW_BLOCK), half_d:2 * half_d] = jnp.where(keep, hi, 0.0).astype(BF16)
        return carry

    lax.fori_loop(0, n_blk, unpack, 0)

    ring = RING_AHEAD + 1

    def gate_up_tile(t, carry):
        buf = t % ring
        wait(gate_up_panels(expert, t, buf))
        nxt = t + RING_AHEAD

        @pl.when(nxt < n_a)
        def _():
            start(gate_up_panels(expert, nxt, nxt % ring))

        @pl.when(nxt >= n_a)
        def _():
            start(down_panels(expert, nxt - n_a, nxt % ring))

        col = pl.multiple_of(t * tw, tw)
        b_gate = bgu_ref[0, :, pl.ds(col, tw)]
        b_up = bgu_ref[0, :, pl.ds(d_ff + col, tw)]

        def gate_up(r0, m):
            gu = jnp.dot(xb_ref[pl.ds(r0, m), :], wbuf_ref[buf].astype(BF16), preferred_element_type=F32)
            gate = jnp.minimum(gu[:, 0:tw] + b_gate, SWIGLU_LIMIT)
            up = jnp.clip(gu[:, tw:2 * tw] + b_up, -SWIGLU_LIMIT, SWIGLU_LIMIT)
            act = (up + 1.0) * (gate * jax.nn.sigmoid(SWIGLU_ALPHA * gate))
            act_ref[pl.ds(r0, m), pl.ds(col, tw)] = act.astype(BF16)

        for_row_blocks(gate_up)
        return carry

    lax.fori_loop(0, n_a, gate_up_tile, 0)

    def down_tile(u, carry):
        buf = (n_a + u) % ring
        wait(down_panels(expert, u, buf))
        nxt = u + RING_AHEAD

        @pl.when(s + 1 < pl.num_programs(0))
        def _():
            start(gate_up_panels(se_ref[s + 1], nxt - n_b, (nxt - n_b) % ring))

        col = pl.multiple_of(u * tw, tw)
        b_lo = bd_ref[0, :, pl.ds(col, tw)]
        b_hi = bd_ref[0, :, pl.ds(half_d + col, tw)]

        def down(r0, m):
            y = jnp.dot(act_ref[pl.ds(r0, m), :], wbuf_ref[buf].astype(BF16), preferred_element_type=F32)
            ys_ref[pl.ds(r0, m), pl.ds(col, tw)] = _pack_bf16_pair(y[:, 0:tw] + b_lo, y[:, tw:2 * tw] + b_hi)

        for_row_blocks(down)
        return carry

    lax.fori_loop(0, n_b, down_tile, 0)

    def fill(rb, carry):
        r0 = pl.multiple_of(rb * ROW_BLOCK, ROW_BLOCK)
        ys_ref[pl.ds(r0, ROW_BLOCK), :] = jnp.zeros((ROW_BLOCK, half_d), jnp.uint32)
        return carry

    lax.fori_loop(n_blk, SLOT_ROWS // ROW_BLOCK, fill, 0)


def _experts(xs, slot_expert, slot_rows, n_used, w_gate_up, b_gate_up, w_down, b_down):
    n_e, d, two_ff = w_gate_up.shape
    d_ff = two_ff // 2
    assert d == d_ff, "the weight ring holds (rows, W_PANEL) panels of both projections"
    tw = W_PANEL
    n_a, n_b = d_ff // tw, (d // 2) // tw
    ring = RING_AHEAD + 1
    assert (n_a + n_b) % ring == 0, "ring slot of tile j must repeat from one expert slot to the next"
    assert n_b == RING_AHEAD <= n_a, "the look-ahead from a down tile must land in the next slot's gate/up tiles"
    total_rows = xs.shape[0]

    grid_spec = pltpu.PrefetchScalarGridSpec(
        num_scalar_prefetch=2,
        grid=(n_used,),
        in_specs=[
            pl.BlockSpec((SLOT_ROWS, d // 2), lambda s, se, sr: (s, 0)),
            pl.BlockSpec(memory_space=pl.ANY),
            pl.BlockSpec(memory_space=pl.ANY),
            pl.BlockSpec((1, 1, two_ff), lambda s, se, sr: (se[s], 0, 0)),
            pl.BlockSpec((1, 1, d), lambda s, se, sr: (se[s], 0, 0)),
        ],
        out_specs=pl.BlockSpec((SLOT_ROWS, d // 2), lambda s, se, sr: (s, 0)),
        scratch_shapes=[pltpu.VMEM((SLOT_ROWS, d), BF16),
                        pltpu.VMEM((SLOT_ROWS, d_ff), BF16),
                        pltpu.VMEM((ring, d, 2 * tw), F32),
                        pltpu.SemaphoreType.DMA((ring, 2))],
    )
    return pl.pallas_call(
        functools.partial(_expert_kernel, n_a=n_a, n_b=n_b),
        grid_spec=grid_spec,
        out_shape=jax.ShapeDtypeStruct((total_rows, d // 2), jnp.uint32),
        compiler_params=_tc_params(("arbitrary",)),
        name="experts",
    )(slot_expert, slot_rows, xs, w_gate_up, w_down,
      b_gate_up.reshape(n_e, 1, two_ff), b_down.reshape(n_e, 1, d))


def _slot_tables(counts, n_tokens):
    max_slots = _max_slots(n_tokens)
    n_slots = (counts + SLOT_ROWS - 1) // SLOT_ROWS
    slot_end = jnp.cumsum(n_slots)
    slot_start = slot_end - n_slots
    n_used = slot_end[-1]
    sid = jnp.arange(max_slots, dtype=jnp.int32)
    expert = jnp.minimum(jnp.sum(sid[:, None] >= slot_end[None, :], axis=1), N_EXPERTS - 1).astype(jnp.int32)
    local = sid - slot_start[expert]
    rows = jnp.clip(counts[expert] - local * SLOT_ROWS, 0, SLOT_ROWS)
    rows = jnp.where(sid < n_used, rows, 0).astype(jnp.int32)
    return expert, rows, n_used.reshape(1).astype(jnp.int32)


def _final_kernel(x1_ref, y0_ref, y1_ref, y2_ref, y3_ref, w_ref, mod_ref, g_ref, o_ref):
    half = y0_ref.shape[1]
    d = 2 * half
    moe_lo = moe_hi = None
    for k, y_ref in enumerate((y0_ref, y1_ref, y2_ref, y3_ref)):
        lo, hi = _unpack_bf16_pair(y_ref[...])
        wk = w_ref[:, k:k + 1]
        moe_lo = wk * lo if k == 0 else moe_lo + wk * lo
        moe_hi = wk * hi if k == 0 else moe_hi + wk * hi
    x_lo = x1_ref[:, 0:half] + mod_ref[0, 5:6, 0:half] * moe_lo
    x_hi = x1_ref[:, half:d] + mod_ref[0, 5:6, half:d] * moe_hi
    ss = jnp.sum(x_lo * x_lo, axis=-1, keepdims=True) + jnp.sum(x_hi * x_hi, axis=-1, keepdims=True)
    r = lax.rsqrt(ss / d + EPS)
    o_ref[:, 0:half] = x_lo * r * g_ref[:, 0:half]
    o_ref[:, half:d] = x_hi * r * g_ref[:, half:d]


def _final(x1, y4, w_tok, mod, final_g, seq):
    n, d = x1.shape
    tm = 512
    per_batch = seq // tm
    n_i = n // tm
    y_specs = [pl.BlockSpec((tm, d // 2), functools.partial(lambda i, k: (k * n_i + i, 0), k=k))
               for k in range(TOP_K)]
    return pl.pallas_call(
        _final_kernel,
        grid=(n_i,),
        in_specs=[pl.BlockSpec((tm, d), lambda i: (i, 0))] + y_specs + [
                  pl.BlockSpec((tm, TOP_K), lambda i: (i, 0)),
                  pl.BlockSpec((1, 6, d), lambda i: (i // per_batch, 0, 0)),
                  pl.BlockSpec((1, d), lambda i: (0, 0))],
        out_specs=pl.BlockSpec((tm, d), lambda i: (i, 0)),
        out_shape=jax.ShapeDtypeStruct((n, d), F32),
        compiler_params=_tc_params(("arbitrary",)),
        name="final",
    )(x1, y4, y4, y4, y4, w_tok, mod, final_g.reshape(1, d))


def kernel(x, c, norm1_g, w_mod, b_mod, w_in, conv_w, w_out, norm2_g, w_router, b_router,
           w_gate_up, b_gate_up, w_down, b_down, final_g):
    b, seq, d = x.shape
    n = b * seq
    x2 = x.reshape(n, d)
    mod = _modulation(c, w_mod[0], b_mod[0])
    proj = _in_projection(x2, mod, norm1_g[0], w_in[0].astype(BF16), seq)
    x1, h2, logits = _mixer(proj, x2, mod, conv_w[0], w_out[0].astype(BF16), norm2_g[0],
                            w_router[0], b_router[0], seq)
    pos, w_top, counts = _route(logits.T)
    slot_expert, slot_rows, n_used = _slot_tables(counts[:, 0].astype(jnp.int32), n)
    xs = _dispatch_rows(h2, pos)
    ys = _experts(xs, slot_expert, slot_rows, n_used[0], w_gate_up[0], b_gate_up[0], w_down[0], b_down[0])
    y4 = _combine_rows(ys, pos.reshape(TOP_K * n))
    out = _final(x1, y4, w_top.T, mod, final_g, seq)
    return out.reshape(b, seq, d)
```

```python
import functools

import jax
import jax.numpy as jnp
import numpy as np
from jax import lax
from jax.experimental import pallas as pl
from jax.experimental.pallas import tpu as pltpu
from jax.experimental.pallas import tpu_sc as plsc

F32 = jnp.float32
BF16 = jnp.bfloat16

CHUNK = 64
N_HEADS = 8
HEAD_DIM = 128
D_RET = N_HEADS * HEAD_DIM
D_CONV = 1024
CONV_WIDTH = 3
N_EXPERTS = 32
TOP_K = 4
SWIGLU_LIMIT = 7.0
SWIGLU_ALPHA = 1.702
ROPE_BASE = 10000.0
EPS = 1e-6

VMEM_LIMIT_BYTES = 58 * 1024 * 1024
SC_WORKERS = 32
SC_ROWS_PER_COPY = 32

MIX_BLOCK = 256
ROW_BLOCK = 128
SLOT_ROWS = 9 * ROW_BLOCK
W_PANEL = 512
RING_AHEAD = 2


def _tc_params(sem):
    return pltpu.CompilerParams(dimension_semantics=sem, vmem_limit_bytes=VMEM_LIMIT_BYTES)


def _bf16_bits(x):
    return lax.bitcast_convert_type(x.astype(BF16).astype(F32), jnp.uint32)


def _pack_bf16_pair(lo, hi):
    return (_bf16_bits(lo) >> 16) | _bf16_bits(hi)


def _unpack_bf16_pair(p):
    lo = lax.bitcast_convert_type(p << 16, F32)
    hi = lax.bitcast_convert_type(p & jnp.uint32(0xFFFF0000), F32)
    return lo, hi


def _mod_kernel(c_ref, w_ref, b_ref, o_ref):
    c = c_ref[...]
    ca = (c * jax.nn.sigmoid(c)).astype(BF16)
    o_ref[...] = jnp.dot(ca, w_ref[...].astype(BF16), preferred_element_type=F32) + b_ref[...]


def _modulation(c, w_mod, b_mod):
    b, d = c.shape
    n = w_mod.shape[1]
    tn = 1024
    c8 = jnp.zeros((8, d), F32).at[:b].set(c)
    out = pl.pallas_call(
        _mod_kernel,
        grid=(n // tn,),
        in_specs=[pl.BlockSpec((8, d), lambda j: (0, 0)),
                  pl.BlockSpec((d, tn), lambda j: (0, j)),
                  pl.BlockSpec((1, tn), lambda j: (0, j))],
        out_specs=pl.BlockSpec((8, tn), lambda j: (0, j)),
        out_shape=jax.ShapeDtypeStruct((8, n), F32),
        compiler_params=_tc_params(("arbitrary",)),
        name="mod",
    )(c8, w_mod, b_mod.reshape(1, n))
    return out[:b].reshape(b, 6, d)


def _norm_mod(x, g, scale, shift):
    y = x * lax.rsqrt(jnp.mean(x * x, axis=-1, keepdims=True) + EPS) * g
    return y * (1.0 + scale) + shift


def _inproj_kernel(x_ref, mod_ref, g_ref, w_ref, o_ref, h_ref, *, rows):
    @pl.when(pl.program_id(1) == 0)
    def _():
        g = g_ref[...]
        scale = mod_ref[0, 1:2, :]
        shift = mod_ref[0, 0:1, :]

        def body(r, carry):
            r0 = pl.multiple_of(r * rows, rows)
            h = _norm_mod(x_ref[pl.ds(r0, rows), :], g, scale, shift)
            h_ref[pl.ds(r0, rows), :] = h.astype(BF16)
            return carry

        lax.fori_loop(0, x_ref.shape[0] // rows, body, 0)

    o_ref[...] = jnp.dot(h_ref[...], w_ref[...], preferred_element_type=F32).astype(o_ref.dtype)


def _in_projection(x2, mod, g1, w_in_bf, seq):
    n, d = x2.shape
    p = w_in_bf.shape[1]
    tm, tn = min(1024, seq), 1792
    per_batch = seq // tm
    return pl.pallas_call(
        functools.partial(_inproj_kernel, rows=128),
        grid=(n // tm, p // tn),
        in_specs=[pl.BlockSpec((tm, d), lambda i, j: (i, 0)),
                  pl.BlockSpec((1, 6, d), lambda i, j: (i // per_batch, 0, 0)),
                  pl.BlockSpec((1, d), lambda i, j: (0, 0)),
                  pl.BlockSpec((d, tn), lambda i, j: (0, j))],
        out_specs=pl.BlockSpec((tm, tn), lambda i, j: (i, j)),
        out_shape=jax.ShapeDtypeStruct((n, p), BF16),
        scratch_shapes=[pltpu.VMEM((tm, d), BF16)],
        compiler_params=_tc_params(("arbitrary", "arbitrary")),
        name="inproj",
    )(x2, mod, g1.reshape(1, d), w_in_bf)


def _mixer_tables(seq, blk):
    f32 = np.float32
    half = HEAD_DIM // 2
    freqs = f32(ROPE_BASE) ** (-np.arange(half, dtype=f32) / f32(half))
    ang = np.arange(seq, dtype=f32)[:, None] * freqs[None, :]
    cos = np.concatenate([np.cos(ang), np.cos(ang)], axis=-1)
    sin = np.concatenate([-np.sin(ang), np.sin(ang)], axis=-1)
    log_gamma = np.log1p(-np.exp2(f32(-5.0) - np.arange(N_HEADS, dtype=f32)))
    idx = np.arange(blk, dtype=f32)
    dist = np.abs(idx[:, None] - idx[None, :])
    ck = np.arange(blk) // CHUNK
    visible = (ck[None, :] <= ck[:, None]).astype(f32)
    dmask = np.exp(log_gamma[:, None, None] * dist) * visible
    ones = np.ones((1, 1, HEAD_DIM), f32)
    dq = np.exp(log_gamma[:, None] * (idx + f32(1.0))[None])[:, :, None] * ones
    dkv = np.exp(log_gamma[:, None] * (f32(blk - 1) - idx)[None])[:, :, None] * ones
    dblk = np.exp(log_gamma * f32(blk))[:, None, None] * ones
    return tuple(jnp.asarray(t, F32) for t in (cos, sin, dmask, dq, dkv, dblk))


def _mixer_kernel(proj_ref, x_ref, mod_ref, cos_ref, sin_ref, dmask_ref, dq_ref, dkv_ref, dblk_ref,
                  convw_ref, wout_ref, g2_ref, wr_ref, br_ref,
                  x1_ref, h2_ref, lg_ref,
                  s_ref, zbuf_ref, y_ref, *, per_batch, n_blocks):
    blk = x_ref.shape[0]
    i = pl.program_id(0)
    cur = i % 2

    @pl.when(jnp.minimum(i, n_blocks - 1) % per_batch == 0)
    def _():
        s_ref[...] = jnp.zeros_like(s_ref)
        zbuf_ref[0:8, :] = jnp.zeros((8, D_CONV), F32)

    @pl.when(i == 0)
    def _():
        y_ref[1] = jnp.zeros(y_ref.shape[1:], BF16)

    cos = cos_ref[...]
    sin = sin_ref[...]
    k_scale = HEAD_DIM ** -0.5
    nt = (((1,), (1,)), ((), ()))
    tn = (((0,), (0,)), ((), ()))
    for h in range(N_HEADS):
        c0 = h * HEAD_DIM
        q = proj_ref[:, c0:c0 + HEAD_DIM].astype(F32)
        k = proj_ref[:, D_RET + c0:D_RET + c0 + HEAD_DIM].astype(F32)
        v = proj_ref[:, 2 * D_RET + c0:2 * D_RET + c0 + HEAD_DIM]
        g = proj_ref[:, 3 * D_RET + c0:3 * D_RET + c0 + HEAD_DIM].astype(F32)
        qr = q * cos + pltpu.roll(q, HEAD_DIM // 2, 1) * sin
        kr = (k * cos + pltpu.roll(k, HEAD_DIM // 2, 1) * sin) * k_scale
        qb = qr.astype(BF16)
        kb = kr.astype(BF16)
        scores = lax.dot_general(qb, kb, nt, preferred_element_type=F32) * dmask_ref[h]
        intra = jnp.dot(scores.astype(BF16), v, preferred_element_type=F32)
        state = s_ref[h]
        cross = jnp.dot(qb, state.astype(BF16), preferred_element_type=F32) * dq_ref[h]
        kd = (kr * dkv_ref[h]).astype(BF16)
        kv = lax.dot_general(kd, v, tn, preferred_element_type=F32)
        s_ref[h] = dblk_ref[h] * state + kv
        o = intra + cross
        mu = jnp.mean(o, axis=-1, keepdims=True)
        dev = o - mu
        var = jnp.mean(dev * dev, axis=-1, keepdims=True)
        on = dev * lax.rsqrt(var + EPS)
        y_ref[cur, :, c0:c0 + HEAD_DIM] = (g * jax.nn.sigmoid(g) * on).astype(BF16)

    cw = 256
    base = 4 * D_RET
    for cb in range(D_CONV // cw):
        lo = cb * cw
        cg = proj_ref[:, base + D_CONV + lo:base + D_CONV + lo + cw].astype(F32)
        u = proj_ref[:, base + 2 * D_CONV + lo:base + 2 * D_CONV + lo + cw].astype(F32)
        zbuf_ref[8:blk + 8, lo:lo + cw] = cg * u
    for cb in range(D_CONV // cw):
        lo = cb * cw
        z0 = zbuf_ref[8:blk + 8, lo:lo + cw]
        z1 = zbuf_ref[7:blk + 7, lo:lo + cw]
        z2 = zbuf_ref[6:blk + 6, lo:lo + cw]
        z = (convw_ref[2:3, lo:lo + cw] * z0 + convw_ref[1:2, lo:lo + cw] * z1
             + convw_ref[0:1, lo:lo + cw] * z2)
        bg = proj_ref[:, base + lo:base + lo + cw].astype(F32)
        y_ref[cur, :, D_RET + lo:D_RET + lo + cw] = (bg * z).astype(BF16)
    zbuf_ref[0:8, :] = zbuf_ref[blk:blk + 8, :]

    mix = jnp.dot(y_ref[1 - cur], wout_ref[...], preferred_element_type=F32)
    x1 = x_ref[...] + mod_ref[0, 2:3, :] * mix
    x1_ref[...] = x1
    h2 = _norm_mod(x1, g2_ref[...], mod_ref[0, 4:5, :], mod_ref[0, 3:4, :])
    half = h2.shape[1] // 2
    h2_ref[...] = _pack_bf16_pair(h2[:, :half], h2[:, half:])
    w2 = wr_ref[...]
    w_hi = w2.astype(BF16)
    w_lo = (w2 - w_hi.astype(F32)).astype(BF16)
    w_split = jnp.where(lax.broadcasted_iota(jnp.int32, w2.shape, 1) < N_EXPERTS, w_hi, w_lo)
    h_hi = h2.astype(BF16)
    h_lo = (h2 - h_hi.astype(F32)).astype(BF16)
    r = (jnp.dot(h_hi, w_split, preferred_element_type=F32)
         + jnp.dot(h_lo, w_split, preferred_element_type=F32))
    lg_ref[...] = r[:, 0:N_EXPERTS] + r[:, N_EXPERTS:2 * N_EXPERTS] + br_ref[...]


def _mixer(proj, x2, mod, conv_w, w_out_bf, g2, w_router, b_router, seq):
    n, d = x2.shape
    blk = MIX_BLOCK
    per_batch = seq // blk
    cos, sin, dmask, dq, dkv, dblk = _mixer_tables(seq, blk)
    const2 = lambda i: (0, 0)
    const3 = lambda i: (0, 0, 0)
    n_blocks = n // blk
    mixed = lambda i: jnp.minimum(i, n_blocks - 1)
    done = lambda i: jnp.maximum(i - 1, 0)
    return pl.pallas_call(
        functools.partial(_mixer_kernel, per_batch=per_batch, n_blocks=n_blocks),
        grid=(n_blocks + 1,),
        in_specs=[pl.BlockSpec((blk, proj.shape[1]), lambda i: (mixed(i), 0)),
                  pl.BlockSpec((blk, d), lambda i: (done(i), 0)),
                  pl.BlockSpec((1, 6, d), lambda i: (done(i) // per_batch, 0, 0)),
                  pl.BlockSpec((blk, HEAD_DIM), lambda i: (mixed(i) % per_batch, 0)),
                  pl.BlockSpec((blk, HEAD_DIM), lambda i: (mixed(i) % per_batch, 0)),
                  pl.BlockSpec((N_HEADS, blk, blk), const3),
                  pl.BlockSpec((N_HEADS, blk, HEAD_DIM), const3),
                  pl.BlockSpec((N_HEADS, blk, HEAD_DIM), const3),
                  pl.BlockSpec((N_HEADS, 1, HEAD_DIM), const3),
                  pl.BlockSpec((CONV_WIDTH, D_CONV), const2),
                  pl.BlockSpec((D_RET + D_CONV, d), const2),
                  pl.BlockSpec((1, d), const2),
                  pl.BlockSpec((d, 2 * N_EXPERTS), const2),
                  pl.BlockSpec((1, N_EXPERTS), const2)],
        out_specs=[pl.BlockSpec((blk, d), lambda i: (done(i), 0)),
                   pl.BlockSpec((blk, d // 2), lambda i: (done(i), 0)),
                   pl.BlockSpec((blk, N_EXPERTS), lambda i: (done(i), 0))],
        out_shape=[jax.ShapeDtypeStruct((n, d), F32),
                   jax.ShapeDtypeStruct((n, d // 2), jnp.uint32),
                   jax.ShapeDtypeStruct((n, N_EXPERTS), F32)],
        scratch_shapes=[pltpu.VMEM((N_HEADS, HEAD_DIM, HEAD_DIM), F32),
                        pltpu.VMEM((blk + 8, D_CONV), F32),
                        pltpu.VMEM((2, blk, D_RET + D_CONV), BF16)],
        compiler_params=_tc_params(("arbitrary",)),
        name="mixer",
    )(proj, x2, mod, cos, sin, dmask, dq, dkv, dblk, conv_w, w_out_bf, g2.reshape(1, d),
      jnp.concatenate([w_router, w_router], axis=1), b_router.reshape(1, N_EXPERTS))


def _top_k_block(lg_ref, tri_ref, run_ref):
    n_e, bw = lg_ref.shape

    @pl.when(pl.program_id(0) == 0)
    def _():
        run_ref[...] = jnp.zeros_like(run_ref)

    l = lg_ref[...]
    eio = lax.broadcasted_iota(jnp.int32, (n_e, bw), 0)
    vals, hots = [], []
    for _ in range(TOP_K):
        m = jnp.max(l, axis=0, keepdims=True)
        idx = jnp.min(jnp.where(l == m, eio, n_e), axis=0, keepdims=True)
        hot = eio == idx
        vals.append(m)
        hots.append(hot)
        l = jnp.where(hot, -jnp.inf, l)
    sel = sum(jnp.where(hot, 1.0, 0.0) for hot in hots)
    incl = jnp.dot(sel.astype(BF16), tri_ref[...], preferred_element_type=F32)
    run = run_ref[:, 0:1]
    rank = incl - sel + run
    run_ref[...] = jnp.broadcast_to(run + incl[:, bw - 1:bw], run_ref.shape)
    return vals, hots, rank


def _count_kernel(lg_ref, tri_ref, cnt_ref, run_ref):
    _top_k_block(lg_ref, tri_ref, run_ref)
    cnt_ref[...] = run_ref[...]


def _route_kernel(lg_ref, tri_ref, low_ref, tot_ref, pos_ref, w_ref, run_ref, *, max_slots):
    vals, hots, rank = _top_k_block(lg_ref, tri_ref, run_ref)
    ex = [jnp.exp(v - vals[0]) for v in vals]
    den = ex[0] + ex[1] + ex[2] + ex[3]
    for k in range(TOP_K):
        w_ref[k:k + 1, :] = ex[k] / den
    tot = tot_ref[...]
    n_slots = sum(jnp.where(tot > float(m * SLOT_ROWS), 1.0, 0.0) for m in range(max_slots))
    start = jnp.dot(low_ref[...], n_slots.astype(BF16), preferred_element_type=F32)
    dest = start[:, 0:1] * float(SLOT_ROWS) + rank
    for k in range(TOP_K):
        pos_ref[k:k + 1, :] = jnp.sum(jnp.where(hots[k], dest, 0.0), axis=0,
                                      keepdims=True).astype(jnp.int32)


def _route(logits_t):
    n_e, n = logits_t.shape
    bw = min(1024, n)
    max_slots = -(-n // SLOT_ROWS)
    tri = jnp.asarray(np.arange(bw)[:, None] <= np.arange(bw)[None, :], BF16)
    low = jnp.asarray(np.arange(n_e)[None, :] < np.arange(n_e)[:, None], BF16)
    lg_spec = pl.BlockSpec((n_e, bw), lambda c: (0, c))
    tri_spec = pl.BlockSpec((bw, bw), lambda c: (0, 0))
    cnt_spec = pl.BlockSpec((n_e, 128), lambda c: (0, 0))
    run = pltpu.VMEM((n_e, 128), F32)
    counts = pl.pallas_call(
        _count_kernel,
        grid=(n // bw,),
        in_specs=[lg_spec, tri_spec],
        out_specs=cnt_spec,
        out_shape=jax.ShapeDtypeStruct((n_e, 128), F32),
        scratch_shapes=[run],
        compiler_params=_tc_params(("arbitrary",)),
        name="route_count",
    )(logits_t, tri)
    pos, w_top = pl.pallas_call(
        functools.partial(_route_kernel, max_slots=max_slots),
        grid=(n // bw,),
        in_specs=[lg_spec, tri_spec, pl.BlockSpec((n_e, n_e), lambda c: (0, 0)), cnt_spec],
        out_specs=[pl.BlockSpec((TOP_K, bw), lambda c: (0, c)),
                   pl.BlockSpec((TOP_K, bw), lambda c: (0, c))],
        out_shape=[jax.ShapeDtypeStruct((TOP_K, n), jnp.int32),
                   jax.ShapeDtypeStruct((TOP_K, n), F32)],
        scratch_shapes=[run],
        compiler_params=_tc_params(("arbitrary",)),
        name="route",
    )(logits_t, tri, low, counts)
    return pos, w_top, counts


def _sc_mesh():
    return plsc.VectorSubcoreMesh(core_axis_name="c", subcore_axis_name="s")


def _sc_worker_id():
    return lax.axis_index("s") * lax.axis_size("c") + lax.axis_index("c")


def _dispatch_rows(h2, pos):
    n, d = h2.shape
    ch = SC_ROWS_PER_COPY
    per_w = n // SC_WORKERS
    n_ch = per_w // ch
    total_rows = _max_slots(n) * SLOT_ROWS
    idx = pos.reshape(TOP_K, SC_WORKERS, n_ch, ch).transpose(1, 2, 0, 3).reshape(
        SC_WORKERS, n_ch * TOP_K, ch)

    @functools.partial(
        pl.kernel, mesh=_sc_mesh(),
        out_type=jax.ShapeDtypeStruct((total_rows, d), h2.dtype),
        scratch_types=[pltpu.VMEM((n_ch * TOP_K, ch), jnp.int32),
                       pltpu.VMEM((ch, d), h2.dtype), pltpu.VMEM((ch, d), h2.dtype),
                       pltpu.SemaphoreType.DMA, pltpu.SemaphoreType.DMA,
                       pltpu.SemaphoreType.DMA, pltpu.SemaphoreType.DMA])
    def scatter(x_hbm, idx_hbm, o_hbm, idx_v, rows0, rows1, rsem0, rsem1, ssem0, ssem1):
        wid = _sc_worker_id()
        base = wid * per_w
        pltpu.sync_copy(idx_hbm.at[wid], idx_v)
        rows, rsem, ssem = (rows0, rows1), (rsem0, rsem1), (ssem0, ssem1)

        def read(c, b):
            r0 = pl.multiple_of(base + c * ch, ch)
            return pltpu.make_async_copy(x_hbm.at[pl.ds(r0, ch)], rows[b], rsem[b])

        def send(c, k, b):
            return pltpu.make_async_copy(rows[b], o_hbm.at[idx_v.at[c * TOP_K + k]], ssem[b])

        read(0, 0).start()

        @pl.loop(0, n_ch, step=2)
        def _(c):
            read(c, 0).wait()

            @pl.when(c > 0)
            def _():
                for k in range(TOP_K):
                    send(c - 1, k, 1).wait()

            read(c + 1, 1).start()
            for k in range(TOP_K):
                send(c, k, 0).start()
            read(c + 1, 1).wait()
            for k in range(TOP_K):
                send(c, k, 0).wait()

            @pl.when(c + 2 < n_ch)
            def _():
                read(c + 2, 0).start()

            for k in range(TOP_K):
                send(c + 1, k, 1).start()

        for k in range(TOP_K):
            send(n_ch - 1, k, 1).wait()

    return scatter(h2, idx)


def _combine_rows(ys, pos_flat):
    n = pos_flat.shape[0]
    d = ys.shape[1]
    ch = SC_ROWS_PER_COPY
    per_w = n // SC_WORKERS
    n_ch = per_w // ch
    idx = pos_flat.reshape(SC_WORKERS, n_ch, ch)

    @functools.partial(
        pl.kernel, mesh=_sc_mesh(),
        out_type=jax.ShapeDtypeStruct((n, d), ys.dtype),
        scratch_types=[pltpu.VMEM((n_ch, ch), jnp.int32),
                       pltpu.VMEM((ch, d), ys.dtype), pltpu.VMEM((ch, d), ys.dtype),
                       pltpu.SemaphoreType.DMA, pltpu.SemaphoreType.DMA,
                       pltpu.SemaphoreType.DMA, pltpu.SemaphoreType.DMA])
    def gather(y_hbm, idx_hbm, o_hbm, idx_v, rows0, rows1, gsem0, gsem1, wsem0, wsem1):
        wid = _sc_worker_id()
        base = wid * per_w
        pltpu.sync_copy(idx_hbm.at[wid], idx_v)
        rows, gsem, wsem = (rows0, rows1), (gsem0, gsem1), (wsem0, wsem1)

        def fetch(c, b):
            return pltpu.make_async_copy(y_hbm.at[idx_v.at[c]], rows[b], gsem[b])

        def write(c, b):
            r0 = pl.multiple_of(base + c * ch, ch)
            return pltpu.make_async_copy(rows[b], o_hbm.at[pl.ds(r0, ch)], wsem[b])

        fetch(0, 0).start()

        @pl.loop(0, n_ch, step=2)
        def _(c):
            fetch(c, 0).wait()

            @pl.when(c > 0)
            def _():
                write(c - 1, 1).wait()

            fetch(c + 1, 1).start()
            write(c, 0).start()
            fetch(c + 1, 1).wait()
            write(c, 0).wait()

            @pl.when(c + 2 < n_ch)
            def _():
                fetch(c + 2, 0).start()

            write(c + 1, 1).start()

        write(n_ch - 1, 1).wait()

    return gather(ys, idx)


def _max_slots(n_tokens):
    return N_EXPERTS + (n_tokens * TOP_K) // SLOT_ROWS


def _expert_kernel(se_ref, sr_ref,
                   xs_ref, wgu_hbm, wd_hbm, bgu_ref, bd_ref,
                   ys_ref,
                   xb_ref, act_ref, wbuf_ref, sem_ref, *, n_a, n_b):
    s = pl.program_id(0)
    rows = sr_ref[s]
    n_blk = (rows + ROW_BLOCK - 1) // ROW_BLOCK
    tw = wbuf_ref.shape[2] // 2
    d_ff = act_ref.shape[1]
    half_d = ys_ref.shape[1]

    def panel_copy(src, buf, p):
        return pltpu.make_async_copy(src, wbuf_ref.at[buf, :, pl.ds(p * tw, tw)], sem_ref.at[buf, p])

    def gate_up_panels(e, t, buf):
        c0 = pl.multiple_of(t * tw, tw)
        return (panel_copy(wgu_hbm.at[e, :, pl.ds(c0, tw)], buf, 0),
                panel_copy(wgu_hbm.at[e, :, pl.ds(d_ff + c0, tw)], buf, 1))

    def down_panels(e, u, buf):
        c0 = pl.multiple_of(u * tw, tw)
        return (panel_copy(wd_hbm.at[e, :, pl.ds(c0, tw)], buf, 0),
                panel_copy(wd_hbm.at[e, :, pl.ds(half_d + c0, tw)], buf, 1))

    def start(panels):
        for p, cp in enumerate(panels):
            cp.start(priority=p)

    def wait(panels):
        for cp in panels:
            cp.wait()

    expert = se_ref[s]

    def for_row_blocks(fn):
        n_big = n_blk // 8

        def body(rb, carry):
            fn(pl.multiple_of(rb * 8 * ROW_BLOCK, 8 * ROW_BLOCK), 8 * ROW_BLOCK)
            return carry

        lax.fori_loop(0, n_big, body, 0)
        for size in (4, 2, 1):
            @pl.when((n_blk // size) % 2 == 1)
            def _(size=size):
                first = (n_blk // (2 * size)) * 2 * size * ROW_BLOCK
                fn(pl.multiple_of(first, 2 * size * ROW_BLOCK), size * ROW_BLOCK)

    @pl.when(s == 0)
    def _():
        for k in range(RING_AHEAD):
            start(gate_up_panels(expert, k, k))

    def unpack(rb, carry):
        r0 = pl.multiple_of(rb * ROW_BLOCK, ROW_BLOCK)
        keep = (r0 + lax.broadcasted_iota(jnp.int32, (ROW_BLOCK, 1), 0)) < rows
        lo, hi = _unpack_bf16_pair(xs_ref[pl.ds(r0, ROW_BLOCK), :])
        xb_ref[pl.ds(r0, ROW_BLOCK), 0:half_d] = jnp.where(keep, lo, 0.0).astype(BF16)
        xb_ref[pl.ds(r0, ROW_BLOCK), half_d:2 * half_d] = jnp.where(keep, hi, 0.0).astype(BF16)
        return carry

    lax.fori_loop(0, n_blk, unpack, 0)

    ring = RING_AHEAD + 1

    def gate_up_tile(t, carry):
        buf = t % ring
        wait(gate_up_panels(expert, t, buf))
        nxt = t + RING_AHEAD

        @pl.when(nxt < n_a)
        def _():
            start(gate_up_panels(expert, nxt, nxt % ring))

        @pl.when(nxt >= n_a)
        def _():
            start(down_panels(expert, nxt - n_a, nxt % ring))

        col = pl.multiple_of(t * tw, tw)
        b_gate = bgu_ref[0, :, pl.ds(col, tw)]
        b_up = bgu_ref[0, :, pl.ds(d_ff + col, tw)]

        def gate_up(r0, m):
            gu = jnp.dot(xb_ref[pl.ds(r0, m), :], wbuf_ref[buf].astype(BF16), preferred_element_type=F32)
            gate = jnp.minimum(gu[:, 0:tw] + b_gate, SWIGLU_LIMIT)
            up = jnp.clip(gu[:, tw:2 * tw] + b_up, -SWIGLU_LIMIT, SWIGLU_LIMIT)
            act = (up + 1.0) * (gate * jax.nn.sigmoid(SWIGLU_ALPHA * gate))
            act_ref[pl.ds(r0, m), pl.ds(col, tw)] = act.astype(BF16)

        for_row_blocks(gate_up)
        return carry

    lax.fori_loop(0, n_a, gate_up_tile, 0)

    def down_tile(u, carry):
        buf = (n_a + u) % ring
        wait(down_panels(expert, u, buf))
        nxt = u + RING_AHEAD

        @pl.when(s + 1 < pl.num_programs(0))
        def _():
            start(gate_up_panels(se_ref[s + 1], nxt - n_b, (nxt - n_b) % ring))

        col = pl.multiple_of(u * tw, tw)
        b_lo = bd_ref[0, :, pl.ds(col, tw)]
        b_hi = bd_ref[0, :, pl.ds(half_d + col, tw)]

        def down(r0, m):
            y = jnp.dot(act_ref[pl.ds(r0, m), :], wbuf_ref[buf].astype(BF16), preferred_element_type=F32)
            ys_ref[pl.ds(r0, m), pl.ds(col, tw)] = _pack_bf16_pair(y[:, 0:tw] + b_lo, y[:, tw:2 * tw] + b_hi)

        for_row_blocks(down)
        return carry

    lax.fori_loop(0, n_b, down_tile, 0)

    def fill(rb, carry):
        r0 = pl.multiple_of(rb * ROW_BLOCK, ROW_BLOCK)
        ys_ref[pl.ds(r0, ROW_BLOCK), :] = jnp.zeros((ROW_BLOCK, half_d), jnp.uint32)
        return carry

    lax.fori_loop(n_blk, SLOT_ROWS // ROW_BLOCK, fill, 0)


def _experts(xs, slot_expert, slot_rows, n_used, w_gate_up, b_gate_up, w_down, b_down):
    n_e, d, two_ff = w_gate_up.shape
    d_ff = two_ff // 2
    assert d == d_ff, "the weight ring holds (rows, W_PANEL) panels of both projections"
    tw = W_PANEL
    n_a, n_b = d_ff // tw, (d // 2) // tw
    ring = RING_AHEAD + 1
    assert (n_a + n_b) % ring == 0, "ring slot of tile j must repeat from one expert slot to the next"
    assert n_b == RING_AHEAD <= n_a, "the look-ahead from a down tile must land in the next slot's gate/up tiles"
    total_rows = xs.shape[0]

    grid_spec = pltpu.PrefetchScalarGridSpec(
        num_scalar_prefetch=2,
        grid=(n_used,),
        in_specs=[
            pl.BlockSpec((SLOT_ROWS, d // 2), lambda s, se, sr: (s, 0)),
            pl.BlockSpec(memory_space=pl.ANY),
            pl.BlockSpec(memory_space=pl.ANY),
            pl.BlockSpec((1, 1, two_ff), lambda s, se, sr: (se[s], 0, 0)),
            pl.BlockSpec((1, 1, d), lambda s, se, sr: (se[s], 0, 0)),
        ],
        out_specs=pl.BlockSpec((SLOT_ROWS, d // 2), lambda s, se, sr: (s, 0)),
        scratch_shapes=[pltpu.VMEM((SLOT_ROWS, d), BF16),
                        pltpu.VMEM((SLOT_ROWS, d_ff), BF16),
                        pltpu.VMEM((ring, d, 2 * tw), F32),
                        pltpu.SemaphoreType.DMA((ring, 2))],
    )
    return pl.pallas_call(
        functools.partial(_expert_kernel, n_a=n_a, n_b=n_b),
        grid_spec=grid_spec,
        out_shape=jax.ShapeDtypeStruct((total_rows, d // 2), jnp.uint32),
        compiler_params=_tc_params(("arbitrary",)),
        name="experts",
    )(slot_expert, slot_rows, xs, w_gate_up, w_down,
      b_gate_up.reshape(n_e, 1, two_ff), b_down.reshape(n_e, 1, d))


def _slot_tables(counts, n_tokens):
    max_slots = _max_slots(n_tokens)
    n_slots = (counts + SLOT_ROWS - 1) // SLOT_ROWS
    slot_end = jnp.cumsum(n_slots)
    slot_start = slot_end - n_slots
    n_used = slot_end[-1]
    sid = jnp.arange(max_slots, dtype=jnp.int32)
    expert = jnp.minimum(jnp.sum(sid[:, None] >= slot_end[None, :], axis=1), N_EXPERTS - 1).astype(jnp.int32)
    local = sid - slot_start[expert]
    rows = jnp.clip(counts[expert] - local * SLOT_ROWS, 0, SLOT_ROWS)
    rows = jnp.where(sid < n_used, rows, 0).astype(jnp.int32)
    return expert, rows, n_used.reshape(1).astype(jnp.int32)


def _final_kernel(x1_ref, y0_ref, y1_ref, y2_ref, y3_ref, w_ref, mod_ref, g_ref, o_ref):
    half = y0_ref.shape[1]
    d = 2 * half
    moe_lo = moe_hi = None
    for k, y_ref in enumerate((y0_ref, y1_ref, y2_ref, y3_ref)):
        lo, hi = _unpack_bf16_pair(y_ref[...])
        wk = w_ref[:, k:k + 1]
        moe_lo = wk * lo if k == 0 else moe_lo + wk * lo
        moe_hi = wk * hi if k == 0 else moe_hi + wk * hi
    x_lo = x1_ref[:, 0:half] + mod_ref[0, 5:6, 0:half] * moe_lo
    x_hi = x1_ref[:, half:d] + mod_ref[0, 5:6, half:d] * moe_hi
    ss = jnp.sum(x_lo * x_lo, axis=-1, keepdims=True) + jnp.sum(x_hi * x_hi, axis=-1, keepdims=True)
    r = lax.rsqrt(ss / d + EPS)
    o_ref[:, 0:half] = x_lo * r * g_ref[:, 0:half]
    o_ref[:, half:d] = x_hi * r * g_ref[:, half:d]


def _final(x1, y4, w_tok, mod, final_g, seq):
    n, d = x1.shape
    tm = 512
    per_batch = seq // tm
    n_i = n // tm
    y_specs = [pl.BlockSpec((tm, d // 2), functools.partial(lambda i, k: (k * n_i + i, 0), k=k))
               for k in range(TOP_K)]
    return pl.pallas_call(
        _final_kernel,
        grid=(n_i,),
        in_specs=[pl.BlockSpec((tm, d), lambda i: (i, 0))] + y_specs + [
                  pl.BlockSpec((tm, TOP_K), lambda i: (i, 0)),
                  pl.BlockSpec((1, 6, d), lambda i: (i // per_batch, 0, 0)),
                  pl.BlockSpec((1, d), lambda i: (0, 0))],
        out_specs=pl.BlockSpec((tm, d), lambda i: (i, 0)),
        out_shape=jax.ShapeDtypeStruct((n, d), F32),
        compiler_params=_tc_params(("arbitrary",)),
        name="final",
    )(x1, y4, y4, y4, y4, w_tok, mod, final_g.reshape(1, d))


def kernel(x, c, norm1_g, w_mod, b_mod, w_in, conv_w, w_out, norm2_g, w_router, b_router,
           w_gate_up, b_gate_up, w_down, b_down, final_g):
    b, seq, d = x.shape
    n = b * seq
    x2 = x.reshape(n, d)
    mod = _modulation(c, w_mod[0], b_mod[0])
    proj = _in_projection(x2, mod, norm1_g[0], w_in[0].astype(BF16), seq)
    x1, h2, logits = _mixer(proj, x2, mod, conv_w[0], w_out[0].astype(BF16), norm2_g[0],
                            w_router[0], b_router[0], seq)
    pos, w_top, counts = _route(logits.T)
    slot_expert, slot_rows, n_used = _slot_tables(counts[:, 0].astype(jnp.int32), n)
    xs = _dispatch_rows(h2, pos)
    ys = _experts(xs, slot_expert, slot_rows, n_used[0], w_gate_up[0], b_gate_up[0], w_down[0], b_down[0])
    y4 = _combine_rows(ys, pos.reshape(TOP_K * n))
    out = _final(x1, y4, w_top.T, mod, final_g, seq)
    return out.reshape(b, seq, d)
```
